```python
import math
import functools
import jax
import jax.numpy as jnp
from jax import lax
import numpy as np

D_MODEL = 2048
BATCH = 16
SEQ = 256
DEPTH = 4
DEC_BATCH = 4
DEC_SEQ = 1024
PAST_LEN = 256

GRID_W = 64
BRANCH_WIDTH = 512
N_BRANCH = 3
NA_HEADS = 4
NA_HEAD_DIM = 128
NA_WIDTH = NA_HEADS * NA_HEAD_DIM
NA_WIN_ROWS = 8
NA_WIN_COLS = 16
NA_QCOL_BLOCK = 16
NA_KCOL_BLOCK = NA_QCOL_BLOCK + NA_WIN_COLS
NA_REL_ROWS = 2 * NA_WIN_ROWS - 1
NA_REL_COLS = 2 * NA_WIN_COLS - 1
DIFF_HEADS = 4
DIFF_QK_DIM = 64
DIFF_V_DIM = 128
DIFF_QK_WIDTH = DIFF_HEADS * 2 * DIFF_QK_DIM
DIFF_V_WIDTH = DIFF_HEADS * DIFF_V_DIM
RWKV_HEADS = 8
RWKV_HEAD_DIM = 64
RWKV_WIDTH = RWKV_HEADS * RWKV_HEAD_DIM
RWKV_DECAY_RANK = 64
RWKV_ICL_RANK = 64
RWKV_GATE_RANK = 128
RWKV_SIZES = (RWKV_WIDTH, RWKV_WIDTH, RWKV_WIDTH, RWKV_DECAY_RANK, RWKV_DECAY_RANK, RWKV_ICL_RANK, RWKV_ICL_RANK, RWKV_GATE_RANK)
RWKV_FEAT = 3 * RWKV_WIDTH + 2 * RWKV_DECAY_RANK + 2 * RWKV_ICL_RANK + RWKV_GATE_RANK
RWKV_GN_EPS = 64e-5
IN_SIZES = (NA_WIDTH, NA_WIDTH, NA_WIDTH, DIFF_QK_WIDTH, DIFF_QK_WIDTH, DIFF_V_WIDTH, RWKV_FEAT, N_BRANCH * D_MODEL)
N_IN = 3 * NA_WIDTH + 2 * DIFF_QK_WIDTH + DIFF_V_WIDTH + RWKV_FEAT + N_BRANCH * D_MODEL
FFN_HIDDEN = -(-8 * D_MODEL // (3 * 256)) * 256
ROPE_THETA = 10000.0
Q_BLOCK = 128
LN_EPS = 1e-5

kernel_name = 'hybrid_diffusion_na_diff_rwkv7_step'


def _split(x, sizes):
    return jnp.split(x, np.cumsum(sizes)[:-1].tolist(), axis=-1)


def _normalize(x):
    xf = x.astype(jnp.float32)
    mu = jnp.mean(xf, axis=-1, keepdims=True)
    var = jnp.mean(jnp.square(xf - mu), axis=-1, keepdims=True)
    return (xf - mu) * lax.rsqrt(var + LN_EPS)


def _layer_norm(x, g, b):
    return (_normalize(x) * g + b).astype(x.dtype)


def _modulate(x, shift, scale):
    return (_normalize(x) * (1.0 + scale) + shift).astype(x.dtype)


def _adaln(cond, w_ada_l, b_ada_l):
    m = jax.nn.silu(cond) @ w_ada_l + b_ada_l
    m = m.reshape(m.shape[0], 1, 6, D_MODEL)
    return tuple(m[:, :, i] for i in range(6))


def _swiglu(h, w_in, w_out):
    gate, up = jnp.split(h @ w_in, 2, axis=-1)
    return (jax.nn.silu(gate) * up) @ w_out


def _dense_attend(q, k, v):
    b, t, h, d = q.shape
    qb = q.reshape(b, t // Q_BLOCK, Q_BLOCK, h, d).swapaxes(0, 1)
    def block(qi):
        s = jnp.einsum('bqhd,bkhd->bhqk', qi, k, preferred_element_type=jnp.float32) * (d ** -0.5)
        p = jax.nn.softmax(s, axis=-1).astype(v.dtype)
        return jnp.einsum('bhqk,bkhe->bqhe', p, v)
    o = lax.map(block, qb)
    return o.swapaxes(0, 1).reshape(b, t, h, v.shape[-1])


def _diff_attend(q, k, v, lam):
    b, t, h, _, d = q.shape
    qb = q.reshape(b, t // Q_BLOCK, Q_BLOCK, h, 2, d).swapaxes(0, 1)
    def block(qi):
        s = jnp.einsum('bqhcd,bkhcd->bhcqk', qi, k, preferred_element_type=jnp.float32) * (d ** -0.5)
        p = jax.nn.softmax(s, axis=-1)
        w = (p[:, :, 0] - lam * p[:, :, 1]).astype(v.dtype)
        return jnp.einsum('bhqk,bkhe->bqhe', w, v)
    o = lax.map(block, qb)
    return o.swapaxes(0, 1).reshape(b, t, h, v.shape[-1])


def _diff_lambda(lam_p, lam_init):
    lp = lam_p.astype(jnp.float32)
    return jnp.exp(jnp.sum(lp[0] * lp[1])) - jnp.exp(jnp.sum(lp[2] * lp[3])) + lam_init


def _diff_post(o, g, lam_init):
    b, t, h, e = o.shape
    of = o.astype(jnp.float32)
    of = of * lax.rsqrt(jnp.mean(jnp.square(of), axis=-1, keepdims=True) + LN_EPS)
    return (of.reshape(b, t, h * e) * g * (1.0 - lam_init)).astype(o.dtype)


def _axial_rope_tables(n_tokens):
    t = jnp.arange(n_tokens)
    rows = (t // GRID_W).astype(jnp.float32)
    cols = (t % GRID_W).astype(jnp.float32)
    half = DIFF_QK_DIM // 2
    inv = ROPE_THETA ** (-jnp.arange(0, half, 2, dtype=jnp.float32) / half)
    ang_r = rows[:, None] * inv
    ang_c = cols[:, None] * inv
    ang = jnp.concatenate([ang_r, ang_r, ang_c, ang_c], axis=-1)
    return jnp.cos(ang), jnp.sin(ang)


def _rotate_half(x):
    x1, x2 = jnp.split(x, 2, axis=-1)
    return jnp.concatenate([-x2, x1], axis=-1)


def _apply_axial_rope(x, cos, sin):
    half = x.shape[-1] // 2
    xf = x.astype(jnp.float32)
    rot = jnp.concatenate([_rotate_half(xf[..., :half]), _rotate_half(xf[..., half:])], axis=-1)
    return (xf * cos[:, None, None, :] + rot * sin[:, None, None, :]).astype(x.dtype)


def _na_static_indices(rows):
    win_r = min(NA_WIN_ROWS, rows)
    r = np.arange(rows)
    row_start = np.clip(r - win_r // 2, 0, rows - win_r)
    row_idx = row_start[:, None] + np.arange(win_r)[None, :]
    rel_r = row_idx - r[:, None]
    n_cb = GRID_W // NA_QCOL_BLOCK
    j = np.arange(n_cb)
    kcol_start = np.clip(j * NA_QCOL_BLOCK - NA_WIN_COLS // 2, 0, GRID_W - NA_KCOL_BLOCK)
    col_idx = kcol_start[:, None] + np.arange(NA_KCOL_BLOCK)[None, :]
    qcol = j[:, None] * NA_QCOL_BLOCK + np.arange(NA_QCOL_BLOCK)[None, :]
    win_c0 = np.clip(qcol - NA_WIN_COLS // 2, 0, GRID_W - NA_WIN_COLS)
    kc = col_idx[:, None, :]
    in_win = (kc >= win_c0[:, :, None]) & (kc < win_c0[:, :, None] + NA_WIN_COLS)
    rel_c = np.clip(kc - qcol[:, :, None], -(NA_WIN_COLS - 1), NA_WIN_COLS - 1)
    return row_idx, rel_r, col_idx, rel_c, in_win


def _na_latent(q, k, v, k_ctx, v_ctx, rpb):
    b, s, h, d = q.shape
    rows = s // GRID_W
    row_idx, rel_r, col_idx, rel_c, in_win = _na_static_indices(rows)
    wr = row_idx.shape[1]
    n_cb = GRID_W // NA_QCOL_BLOCK
    ri = row_idx[:, None, :, None]
    ci = col_idx[None, :, None, :]
    kb = k.reshape(b, rows, GRID_W, h, d)[:, ri, ci]
    vb = v.reshape(b, rows, GRID_W, h, d)[:, ri, ci]
    qb = q.reshape(b, rows, n_cb, NA_QCOL_BLOCK, h, d)
    scale = d ** -0.5
    bias = rpb.astype(jnp.float32)[:, (rel_r + NA_WIN_ROWS - 1)[:, None, None, :, None], (rel_c + NA_WIN_COLS - 1)[None, :, :, None, :]]
    s_loc = jnp.einsum('brjqhd,brjikhd->bhrjqik', qb, kb, preferred_element_type=jnp.float32) * scale + bias[None]
    s_loc = jnp.where(in_win[None, None, None, :, :, None, :], s_loc, -jnp.inf)
    n_loc = wr * NA_KCOL_BLOCK
    s_loc = s_loc.reshape(b, h, rows, n_cb, NA_QCOL_BLOCK, n_loc)
    s_ctx = jnp.einsum('brjqhd,blhd->bhrjql', qb, k_ctx, preferred_element_type=jnp.float32) * scale
    p = jax.nn.softmax(jnp.concatenate([s_loc, s_ctx], axis=-1), axis=-1).astype(v.dtype)
    p_loc = p[..., :n_loc].reshape(b, h, rows, n_cb, NA_QCOL_BLOCK, wr, NA_KCOL_BLOCK)
    p_ctx = p[..., n_loc:]
    o = jnp.einsum('bhrjqik,brjikhe->brjqhe', p_loc, vb) + jnp.einsum('bhrjql,blhe->brjqhe', p_ctx, v_ctx)
    return o.reshape(b, s, h, d)


def _token_shift(f, mix):
    prev = jnp.pad(f[:, :-1], ((0, 0), (1, 0), (0, 0)))
    nxt = jnp.pad(f[:, 1:], ((0, 0), (0, 1), (0, 0)))
    return f + mix[0] * (prev - f) + mix[1] * (nxt - f)


def _wkv_scan(s0, r, w, k, v, a, b, reverse):
    def step(state, xs):
        r_t, w_t, k_t, v_t, a_t, b_t = xs
        sa = jnp.einsum('bhvk,bhk->bhv', state, a_t)
        state = state * w_t[:, :, None, :] + sa[..., None] * b_t[:, :, None, :] + v_t[..., None] * k_t[:, :, None, :]
        return state, jnp.einsum('bhvk,bhk->bhv', state, r_t)
    xs = tuple(jnp.moveaxis(z, 1, 0) for z in (r, w, k, v, a, b))
    s_fin, y = lax.scan(step, s0, xs, reverse=reverse)
    return s_fin, jnp.moveaxis(y, 0, 1)


def _rwkv_mixer(f, s_f0, s_b0, lp):
    bsz, t, _ = f.shape
    f = _token_shift(f, lp['rwkv_mix'])
    r, k, v, wd_f, wd_b, ad_f, ad_b, gd = _split(f, RWKV_SIZES)
    def hd(z):
        return z.astype(jnp.float32).reshape(bsz, t, RWKV_HEADS, RWKV_HEAD_DIM)
    rh, kh, vh = hd(r), hd(k), hd(v)
    kk = kh * lp['rwkv_kk'].astype(jnp.float32).reshape(RWKV_HEADS, RWKV_HEAD_DIM)
    kk = kk / jnp.maximum(jnp.linalg.norm(kk, axis=-1, keepdims=True), 1e-12)
    ka = lp['rwkv_ka'].astype(jnp.float32).reshape(RWKV_HEADS, RWKV_HEAD_DIM)
    rk = lp['rwkv_rk'].astype(jnp.float32)
    ys, bonus, finals = [], [], []
    for d, (wd, ad, s0) in enumerate(((wd_f, ad_f, s_f0), (wd_b, ad_b, s_b0))):
        w_log = -jax.nn.softplus(-(lp['rwkv_w0'][d] + jnp.tanh(wd) @ lp['rwkv_w2'][d])) - 0.5
        decay = hd(jnp.exp(-jnp.exp(w_log.astype(jnp.float32))))
        a = hd(jax.nn.sigmoid(lp['rwkv_a0'][d] + ad @ lp['rwkv_a2'][d]))
        k_d = kh * (1.0 + (a - 1.0) * ka)
        s_fin, y_d = _wkv_scan(s0.astype(jnp.float32), rh, decay, k_d, vh, -kk, kk * a, reverse=(d == 1))
        ys.append(y_d)
        bonus.append(jnp.sum(rh * k_d * rk, axis=-1, keepdims=True) * vh)
        finals.append(s_fin.astype(f.dtype))
    y = ys[0] + ys[1]
    mu = jnp.mean(y, axis=-1, keepdims=True)
    var = jnp.mean(jnp.square(y - mu), axis=-1, keepdims=True)
    y = ((y - mu) * lax.rsqrt(var + RWKV_GN_EPS)).reshape(bsz, t, RWKV_WIDTH) * lp['rwkv_lnx_g'] + lp['rwkv_lnx_b']
    y = y + (bonus[0] + bonus[1]).reshape(bsz, t, RWKV_WIDTH)
    g = jax.nn.sigmoid(gd) @ lp['rwkv_g2']
    return (y * g).astype(f.dtype), finals[0], finals[1]


def _merge(gate_pre, oa, ob, oc, lp):
    o = jnp.stack([oa, ob, oc], axis=2)
    branches = jnp.einsum('btie,ied->btid', o, lp['w_branch'])
    g = jax.nn.sigmoid(gate_pre.reshape(gate_pre.shape[0], gate_pre.shape[1], N_BRANCH, D_MODEL))
    return jnp.sum(g * branches, axis=2) @ lp['w_out']


def _mixer_context(h, lp, lam_init):
    b, t, _ = h.shape
    na_q, na_k, na_v, df_q, df_k, df_v, rw, gate_pre = _split(h @ lp['w_in'], IN_SIZES)
    qa = na_q.reshape(b, t, NA_HEADS, NA_HEAD_DIM)
    ka = na_k.reshape(b, t, NA_HEADS, NA_HEAD_DIM)
    va = na_v.reshape(b, t, NA_HEADS, NA_HEAD_DIM)
    oa = _dense_attend(qa, ka, va).reshape(b, t, NA_WIDTH)
    qd = df_q.reshape(b, t, DIFF_HEADS, 2, DIFF_QK_DIM)
    kd = df_k.reshape(b, t, DIFF_HEADS, 2, DIFF_QK_DIM)
    vd = df_v.reshape(b, t, DIFF_HEADS, DIFF_V_DIM)
    lam = _diff_lambda(lp['diff_lambda'], lam_init)
    ob = _diff_post(_diff_attend(qd, kd, vd, lam), lp['diff_subln'], lam_init)
    zeros = jnp.zeros((b, RWKV_HEADS, RWKV_HEAD_DIM, RWKV_HEAD_DIM), jnp.float32)
    oc, s_f, s_b = _rwkv_mixer(rw, zeros, zeros, lp)
    return _merge(gate_pre, oa, ob, oc, lp), (ka, va, kd, vd, s_f, s_b)


def _mixer_latent(h, cache_l, lp, lam_init):
    k_na_ctx, v_na_ctx, k_df_ctx, v_df_ctx, s_f0, s_b0 = cache_l
    b, t, _ = h.shape
    na_q, na_k, na_v, df_q, df_k, df_v, rw, gate_pre = _split(h @ lp['w_in'], IN_SIZES)
    qa = na_q.reshape(b, t, NA_HEADS, NA_HEAD_DIM)
    ka = na_k.reshape(b, t, NA_HEADS, NA_HEAD_DIM)
    va = na_v.reshape(b, t, NA_HEADS, NA_HEAD_DIM)
    oa = _na_latent(qa, ka, va, k_na_ctx, v_na_ctx, lp['na_rpb']).reshape(b, t, NA_WIDTH)
    cos, sin = _axial_rope_tables(t)
    qd = _apply_axial_rope(df_q.reshape(b, t, DIFF_HEADS, 2, DIFF_QK_DIM), cos, sin)
    kd = _apply_axial_rope(df_k.reshape(b, t, DIFF_HEADS, 2, DIFF_QK_DIM), cos, sin)
    vd = df_v.reshape(b, t, DIFF_HEADS, DIFF_V_DIM)
    k_all = jnp.concatenate([kd, k_df_ctx.astype(kd.dtype)], axis=1)
    v_all = jnp.concatenate([vd, v_df_ctx.astype(vd.dtype)], axis=1)
    lam = _diff_lambda(lp['diff_lambda'], lam_init)
    ob = _diff_post(_diff_attend(qd, k_all, v_all, lam), lp['diff_subln'], lam_init)
    oc, _, _ = _rwkv_mixer(rw, s_f0, s_b0, lp)
    return _merge(gate_pre, oa, ob, oc, lp), None


def _block(x, cond, mixer, lp):
    alpha = (2.0 * DEPTH) ** 0.25
    sh1, sc1, g1, sh2, sc2, g2 = _adaln(cond, lp['w_ada'], lp['b_ada'])
    mixed, ctx_tensors = mixer(_modulate(x, sh1, sc1))
    x = _layer_norm(alpha * x + g1 * mixed, lp['ln1_g'], lp['ln1_b'])
    ff = _swiglu(_modulate(x, sh2, sc2), lp['w_ffn_in'], lp['w_ffn_out'])
    x = _layer_norm(alpha * x + g2 * ff, lp['ln2_g'], lp['ln2_b'])
    return x, ctx_tensors


def setup_inputs(seed: int = 0) -> dict:
    key = jax.random.key(seed)
    ks = iter(jax.random.split(key, 40))
    def nrm(shape, scale):
        return scale * jax.random.normal(next(ks), shape, jnp.float32)
    def uni(shape, lo, hi):
        return jax.random.uniform(next(ks), shape, jnp.float32, lo, hi)
    beta = (8.0 * DEPTH) ** -0.25
    L, D = DEPTH, D_MODEL
    return {
        'x_prompt': nrm((BATCH, SEQ, D), 1.0),
        'x_sample': nrm((DEC_BATCH, DEC_SEQ, D), 1.0),
        'cache_na_k': nrm((DEC_BATCH, L, PAST_LEN, NA_HEADS, NA_HEAD_DIM), 1.0),
        'cache_na_v': nrm((DEC_BATCH, L, PAST_LEN, NA_HEADS, NA_HEAD_DIM), 1.0),
        'cache_diff_k': nrm((DEC_BATCH, L, PAST_LEN, DIFF_HEADS, 2, DIFF_QK_DIM), 1.0),
        'cache_diff_v': nrm((DEC_BATCH, L, PAST_LEN, DIFF_HEADS, DIFF_V_DIM), 1.0),
        'state_rwkv_fwd': nrm((DEC_BATCH, L, RWKV_HEADS, RWKV_HEAD_DIM, RWKV_HEAD_DIM), 1.0),
        'state_rwkv_bwd': nrm((DEC_BATCH, L, RWKV_HEADS, RWKV_HEAD_DIM, RWKV_HEAD_DIM), 1.0),
        'c': nrm((DEC_BATCH, D), 1.0),
        'c_ctx': nrm((D,), 1.0),
        'w_ada': nrm((L, D, 6 * D), 0.5 * D ** -0.5),
        'b_ada': nrm((L, 6 * D), 0.01),
        'w_in': nrm((L, D, N_IN), D ** -0.5),
        'na_rpb': nrm((L, NA_HEADS, NA_REL_ROWS, NA_REL_COLS), 0.1),
        'diff_lambda': nrm((L, 4, DIFF_QK_DIM), 0.1),
        'diff_subln': 1.0 + nrm((L, DIFF_V_WIDTH), 0.02),
        'rwkv_mix': uni((L, 2, RWKV_FEAT), 0.0, 0.5),
        'rwkv_w0': uni((L, 2, RWKV_WIDTH), -5.5, -0.5),
        'rwkv_w2': nrm((L, 2, RWKV_DECAY_RANK, RWKV_WIDTH), 0.5 * RWKV_DECAY_RANK ** -0.5),
        'rwkv_a0': nrm((L, 2, RWKV_WIDTH), 0.1),
        'rwkv_a2': nrm((L, 2, RWKV_ICL_RANK, RWKV_WIDTH), 0.5 * RWKV_ICL_RANK ** -0.5),
        'rwkv_g2': nrm((L, RWKV_GATE_RANK, RWKV_WIDTH), RWKV_GATE_RANK ** -0.5),
        'rwkv_kk': 0.85 + nrm((L, RWKV_WIDTH), 0.05),
        'rwkv_ka': 1.0 + nrm((L, RWKV_WIDTH), 0.05),
        'rwkv_rk': nrm((L, RWKV_HEADS, RWKV_HEAD_DIM), 0.1),
        'rwkv_lnx_g': 1.0 + nrm((L, RWKV_WIDTH), 0.02),
        'rwkv_lnx_b': nrm((L, RWKV_WIDTH), 0.02),
        'w_branch': nrm((L, N_BRANCH, BRANCH_WIDTH, D), BRANCH_WIDTH ** -0.5),
        'w_out': nrm((L, D, D), beta * D ** -0.5),
        'ln1_g': 1.0 + nrm((L, D), 0.02),
        'ln1_b': nrm((L, D), 0.02),
        'w_ffn_in': nrm((L, D, 2 * FFN_HIDDEN), D ** -0.5),
        'w_ffn_out': nrm((L, FFN_HIDDEN, D), beta * FFN_HIDDEN ** -0.5),
        'ln2_g': 1.0 + nrm((L, D), 0.02),
        'ln2_b': nrm((L, D), 0.02),
    }


def reference(x_prompt, x_sample, cache_na_k, cache_na_v, cache_diff_k, cache_diff_v,
              state_rwkv_fwd, state_rwkv_bwd, c, c_ctx, w_ada, b_ada, w_in, na_rpb,
              diff_lambda, diff_subln, rwkv_mix, rwkv_w0, rwkv_w2, rwkv_a0, rwkv_a2,
              rwkv_g2, rwkv_kk, rwkv_ka, rwkv_rk, rwkv_lnx_g, rwkv_lnx_b, w_branch,
              w_out, ln1_g, ln1_b, w_ffn_in, w_ffn_out, ln2_g, ln2_b):
    x_ctx, x_lat = x_prompt, x_sample
    cond_ctx = c_ctx[None, :]
    new_t = ([], [], [], [], [], [])
    for l in range(DEPTH):
        lp = {
            'w_ada': w_ada[l], 'b_ada': b_ada[l], 'w_in': w_in[l], 'na_rpb': na_rpb[l],
            'diff_lambda': diff_lambda[l], 'diff_subln': diff_subln[l],
            'rwkv_mix': rwkv_mix[l], 'rwkv_w0': rwkv_w0[l], 'rwkv_w2': rwkv_w2[l],
            'rwkv_a0': rwkv_a0[l], 'rwkv_a2': rwkv_a2[l], 'rwkv_g2': rwkv_g2[l],
            'rwkv_kk': rwkv_kk[l], 'rwkv_ka': rwkv_ka[l], 'rwkv_rk': rwkv_rk[l],
            'rwkv_lnx_g': rwkv_lnx_g[l], 'rwkv_lnx_b': rwkv_lnx_b[l],
            'w_branch': w_branch[l], 'w_out': w_out[l],
            'ln1_g': ln1_g[l], 'ln1_b': ln1_b[l],
            'w_ffn_in': w_ffn_in[l], 'w_ffn_out': w_ffn_out[l],
            'ln2_g': ln2_g[l], 'ln2_b': ln2_b[l],
        }
        lam_init = 0.8 - 0.6 * math.exp(-0.3 * l)
        x_ctx, ctx_t = _block(x_ctx, cond_ctx, functools.partial(_mixer_context, lp=lp, lam_init=lam_init), lp)
        for store, tensor in zip(new_t, ctx_t):
            store.append(tensor)
        cache_l = (cache_na_k[:, l], cache_na_v[:, l], cache_diff_k[:, l], cache_diff_v[:, l],
                   state_rwkv_fwd[:, l], state_rwkv_bwd[:, l])
        x_lat, _ = _block(x_lat, c, functools.partial(_mixer_latent, cache_l=cache_l, lp=lp, lam_init=lam_init), lp)
    new_na_k = jnp.stack(new_t[0], axis=1)
    new_na_v = jnp.stack(new_t[1], axis=1)
    new_diff_k = jnp.stack(new_t[2], axis=1)
    new_diff_v = jnp.stack(new_t[3], axis=1)
    new_rwkv_fwd = jnp.stack(new_t[4], axis=1)
    new_rwkv_bwd = jnp.stack(new_t[5], axis=1)
    return (x_ctx, x_lat, new_na_k, new_na_v, new_diff_k, new_diff_v, new_rwkv_fwd, new_rwkv_bwd)
```

```python
import functools
import math

import numpy as np
import jax
import jax.numpy as jnp
from jax import lax
from jax.experimental import pallas as pl
from jax.experimental.pallas import tpu as pltpu

F32 = jnp.float32
BF16 = jnp.bfloat16

D_MODEL = 2048
BATCH = 16
SEQ = 256
DEPTH = 4
DEC_BATCH = 4
DEC_SEQ = 1024
PAST_LEN = 256
GRID_W = 64
GRID_ROWS = DEC_SEQ // GRID_W
BRANCH_WIDTH = 512
N_BRANCH = 3
NA_HEADS = 4
NA_HEAD_DIM = 128
NA_WIN_ROWS = 8
NA_WIN_COLS = 16
NA_REL_ROWS = 2 * NA_WIN_ROWS - 1
NA_REL_COLS = 2 * NA_WIN_COLS - 1
DIFF_HEADS = 4
DIFF_QK_DIM = 64
DIFF_V_DIM = 128
RWKV_HEADS = 8
RWKV_HEAD_DIM = 64
RWKV_WIDTH = RWKV_HEADS * RWKV_HEAD_DIM
RWKV_DECAY_RANK = 64
RWKV_ICL_RANK = 64
RWKV_GATE_RANK = 128
RWKV_FEAT = 3 * RWKV_WIDTH + 2 * RWKV_DECAY_RANK + 2 * RWKV_ICL_RANK + RWKV_GATE_RANK
RWKV_GN_EPS = 64e-5
ATT_WIDTH = 6 * BRANCH_WIDTH
GATE_WIDTH = N_BRANCH * D_MODEL
FFN_HIDDEN = -(-8 * D_MODEL // (3 * 256)) * 256
ROPE_THETA = 10000.0
LN_EPS = 1e-5
ALPHA = (2.0 * DEPTH) ** 0.25

N_CTX = BATCH * SEQ
N_LAT = DEC_BATCH * DEC_SEQ
N_TOK = N_CTX + N_LAT
N_COND = 8
RWKV_CHUNK = 64
NEG_BIG = -1e30

NN = ((1,), (0,))
NT = ((1,), (1,))
TN = ((0,), (0,))


def _cparams(sem, vmem_mb=48):
    return pltpu.CompilerParams(dimension_semantics=sem, vmem_limit_bytes=vmem_mb * 1024 * 1024)


def _dot(a, b, dims=NN):
    return lax.dot_general(a, b, (dims, ((), ())), preferred_element_type=F32)


def _split2(x):
    hi = x.astype(BF16)
    lo = (x - hi.astype(F32)).astype(BF16)
    return hi, lo


def _dot3(a, b, dims=NN):
    ah, al = _split2(a)
    bh, bl = _split2(b)
    return _dot(ah, bh, dims) + (_dot(ah, bl, dims) + _dot(al, bh, dims))


def _dot_exact_lhs(a_bf16, b, dims=NN):
    b1 = b.astype(BF16)
    r1 = b - b1.astype(F32)
    b2 = r1.astype(BF16)
    b3 = (r1 - b2.astype(F32)).astype(BF16)
    return _dot(a_bf16, b1, dims) + (_dot(a_bf16, b2, dims) + _dot(a_bf16, b3, dims))


def _dot_exact_rhs(a, b_bf16, dims=NN):
    a1 = a.astype(BF16)
    r1 = a - a1.astype(F32)
    a2 = r1.astype(BF16)
    a3 = (r1 - a2.astype(F32)).astype(BF16)
    return _dot(a1, b_bf16, dims) + (_dot(a2, b_bf16, dims) + _dot(a3, b_bf16, dims))


def _normalize(x):
    mu = jnp.mean(x, axis=-1, keepdims=True)
    xc = x - mu
    var = jnp.mean(xc * xc, axis=-1, keepdims=True)
    return xc * lax.rsqrt(var + LN_EPS)


def _cond_of_row(row):
    return jnp.where(row < N_CTX, 0, 1 + (row - N_CTX) // DEC_SEQ)


def _adaln_kernel(c_ref, w_ref, b_ref, o_ref):
    c = c_ref[...]
    s = (c * jax.nn.sigmoid(c)).astype(BF16)
    o_ref[...] = _dot(s, w_ref[...].astype(BF16)) + b_ref[...]


def _adaln_all(cond, w_ada, b_ada):
    tn = 1024
    n = 6 * D_MODEL
    return pl.pallas_call(
        _adaln_kernel,
        grid=(DEPTH, n // tn),
        in_specs=[
            pl.BlockSpec((N_COND, D_MODEL), lambda l, j: (0, 0)),
            pl.BlockSpec((None, D_MODEL, tn), lambda l, j: (l, 0, j)),
            pl.BlockSpec((None, 1, tn), lambda l, j: (l, 0, j)),
        ],
        out_specs=pl.BlockSpec((None, N_COND, tn), lambda l, j: (l, 0, j)),
        out_shape=jax.ShapeDtypeStruct((DEPTH, N_COND, n), F32),
        compiler_params=_cparams(("parallel", "parallel")),
        name="adaln",
    )(cond, w_ada, b_ada.reshape(DEPTH, 1, n))


def _mod_spec(layer, which, tm):
    base = layer * 6 * N_COND + which * N_COND
    return pl.BlockSpec((None, 1, D_MODEL), lambda i: (base + _cond_of_row(i * tm), 0, 0))


def _modulate_kernel(x_ref, sh_ref, sc_ref, h_ref):
    h_ref[...] = (_normalize(x_ref[...]) * (1.0 + sc_ref[...]) + sh_ref[...]).astype(h_ref.dtype)


def _modulate(x, mods, layer):
    tm = 256
    return pl.pallas_call(
        _modulate_kernel,
        grid=(N_TOK // tm,),
        in_specs=[
            pl.BlockSpec((tm, D_MODEL), lambda i: (i, 0)),
            _mod_spec(layer, 0, tm),
            _mod_spec(layer, 1, tm),
        ],
        out_specs=pl.BlockSpec((tm, D_MODEL), lambda i: (i, 0)),
        out_shape=jax.ShapeDtypeStruct((N_TOK, D_MODEL), BF16),
        compiler_params=_cparams(("parallel",)),
        name="modulate",
    )(x, mods, mods)


def _res_ln_kernel(x_ref, z_ref, gate_ref, g_ref, b_ref, *rest, with_mod):
    y = ALPHA * x_ref[...] + gate_ref[...] * z_ref[...]
    xn = _normalize(y) * g_ref[...] + b_ref[...]
    if with_mod:
        sh_ref, sc_ref, xo_ref, h_ref = rest
        xo_ref[...] = xn
        h_ref[...] = (_normalize(xn) * (1.0 + sc_ref[...]) + sh_ref[...]).astype(h_ref.dtype)
    else:
        (xo_ref,) = rest
        xo_ref[...] = xn


def _res_ln(x, z, mods, layer, gate_idx, ln_g, ln_b, mod_layer, mod_idx):
    tm = 256
    with_mod = mod_layer is not None
    row = pl.BlockSpec((tm, D_MODEL), lambda i: (i, 0))
    vec = pl.BlockSpec((1, D_MODEL), lambda i: (0, 0))
    in_specs = [row, row, _mod_spec(layer, gate_idx, tm), vec, vec]
    args = [x, z, mods, ln_g.reshape(1, D_MODEL), ln_b.reshape(1, D_MODEL)]
    out_specs = [row]
    out_shape = [jax.ShapeDtypeStruct((N_TOK, D_MODEL), F32)]
    if with_mod:
        in_specs += [_mod_spec(mod_layer, mod_idx, tm), _mod_spec(mod_layer, mod_idx + 1, tm)]
        args += [mods, mods]
        out_specs.append(row)
        out_shape.append(jax.ShapeDtypeStruct((N_TOK, D_MODEL), BF16))
    out = pl.pallas_call(
        functools.partial(_res_ln_kernel, with_mod=with_mod),
        grid=(N_TOK // tm,),
        in_specs=in_specs,
        out_specs=out_specs,
        out_shape=out_shape,
        compiler_params=_cparams(("parallel",)),
        name="res_ln",
    )(*args)
    return (out[0], out[1]) if with_mod else (out[0], None)


def _mm_kernel(a_ref, b_ref, o_ref):
    o_ref[...] = _dot(a_ref[...], b_ref[...]).astype(o_ref.dtype)


def _matmul(a, b, out_dtype, tm=512, tn=512, name="matmul"):
    m, k = a.shape
    n = b.shape[1]
    assert m % tm == 0 and n % tn == 0
    return pl.pallas_call(
        _mm_kernel,
        grid=(m // tm, n // tn),
        in_specs=[
            pl.BlockSpec((tm, k), lambda i, j: (i, 0)),
            pl.BlockSpec((k, tn), lambda i, j: (0, j)),
        ],
        out_specs=pl.BlockSpec((tm, tn), lambda i, j: (i, j)),
        out_shape=jax.ShapeDtypeStruct((m, n), out_dtype),
        compiler_params=_cparams(("parallel", "parallel")),
        name=name,
    )(a, b)


def _swiglu_kernel(a_ref, bg_ref, bu_ref, o_ref):
    a = a_ref[...]
    g = _dot(a, bg_ref[...])
    u = _dot(a, bu_ref[...])
    o_ref[...] = (g * jax.nn.sigmoid(g) * u).astype(o_ref.dtype)


def _ffn_in(h, w_ffn_in):
    tm, tn = 512, 512
    nb = FFN_HIDDEN // tn
    return pl.pallas_call(
        _swiglu_kernel,
        grid=(N_TOK // tm, nb),
        in_specs=[
            pl.BlockSpec((tm, D_MODEL), lambda i, j: (i, 0)),
            pl.BlockSpec((D_MODEL, tn), lambda i, j: (0, j)),
            pl.BlockSpec((D_MODEL, tn), lambda i, j: (0, j + nb)),
        ],
        out_specs=pl.BlockSpec((tm, tn), lambda i, j: (i, j)),
        out_shape=jax.ShapeDtypeStruct((N_TOK, FFN_HIDDEN), BF16),
        compiler_params=_cparams(("parallel", "parallel")),
        name="ffn_in",
    )(h, w_ffn_in, w_ffn_in)


def _merge_kernel(oa_ref, ob_ref, oc_ref, ga_ref, gb_ref, gc_ref, wb_ref, o_ref):
    acc = jax.nn.sigmoid(ga_ref[...]) * _dot(oa_ref[...], wb_ref[0])
    acc += jax.nn.sigmoid(gb_ref[...]) * _dot(ob_ref[...], wb_ref[1])
    acc += jax.nn.sigmoid(gc_ref[...]) * _dot(oc_ref[...], wb_ref[2])
    o_ref[...] = acc.astype(o_ref.dtype)


def _merge(oa, ob, oc, gate_pre, w_branch):
    tm, tn = 512, 512
    nb = D_MODEL // tn
    o_spec = pl.BlockSpec((tm, BRANCH_WIDTH), lambda i, j: (i, 0))
    return pl.pallas_call(
        _merge_kernel,
        grid=(N_TOK // tm, nb),
        in_specs=[
            o_spec, o_spec, o_spec,
            pl.BlockSpec((tm, tn), lambda i, j: (i, j)),
            pl.BlockSpec((tm, tn), lambda i, j: (i, j + nb)),
            pl.BlockSpec((tm, tn), lambda i, j: (i, j + 2 * nb)),
            pl.BlockSpec((N_BRANCH, BRANCH_WIDTH, tn), lambda i, j: (0, 0, j)),
        ],
        out_specs=pl.BlockSpec((tm, tn), lambda i, j: (i, j)),
        out_shape=jax.ShapeDtypeStruct((N_TOK, D_MODEL), BF16),
        compiler_params=_cparams(("parallel", "parallel")),
        name="merge",
    )(oa, ob, oc, gate_pre, gate_pre, gate_pre, w_branch)


def _softmax_rows(s):
    m = jnp.max(s, axis=-1, keepdims=True)
    e = jnp.exp(s - m)
    return e * (1.0 / jnp.sum(e, axis=-1, keepdims=True))


def _na_ctx_kernel(q_ref, k_ref, v_ref, o_ref):
    scale = NA_HEAD_DIM ** -0.5
    for h in range(NA_HEADS):
        sl = slice(h * NA_HEAD_DIM, (h + 1) * NA_HEAD_DIM)
        q = q_ref[:, sl].astype(BF16)
        k = k_ref[:, sl].astype(BF16)
        v = v_ref[:, sl].astype(BF16)
        p = _softmax_rows(_dot(q, k, NT) * scale)
        o_ref[:, sl] = _dot(p.astype(BF16), v).astype(o_ref.dtype)


def _na_ctx(p_att):
    w = BRANCH_WIDTH
    return pl.pallas_call(
        _na_ctx_kernel,
        grid=(BATCH,),
        in_specs=[
            pl.BlockSpec((SEQ, w), lambda n: (n, 0)),
            pl.BlockSpec((SEQ, w), lambda n: (n, 1)),
            pl.BlockSpec((SEQ, w), lambda n: (n, 2)),
        ],
        out_specs=pl.BlockSpec((SEQ, w), lambda n: (n, 0)),
        out_shape=jax.ShapeDtypeStruct((N_CTX, w), BF16),
        compiler_params=_cparams(("parallel",)),
        name="na_ctx",
    )(p_att, p_att, p_att)


def _na_row_start(r):
    return jnp.clip(r - NA_WIN_ROWS // 2, 0, GRID_ROWS - NA_WIN_ROWS)


def _na_lat_kernel(q_ref, k_ref, v_ref, kc_ref, vc_ref, bias_ref, o_ref):
    scale = NA_HEAD_DIM ** -0.5
    n_loc = NA_WIN_ROWS * GRID_W
    start = pl.multiple_of(_na_row_start(pl.program_id(1)) * GRID_W, GRID_W)
    k_loc = k_ref[pl.ds(start, n_loc), :]
    v_loc = v_ref[pl.ds(start, n_loc), :]
    for h in range(NA_HEADS):
        sl = slice(h * NA_HEAD_DIM, (h + 1) * NA_HEAD_DIM)
        q = q_ref[:, sl].astype(BF16)
        s_loc = _dot(q, k_loc[:, sl].astype(BF16), NT) * scale + bias_ref[h]
        s_ctx = _dot(q, kc_ref[:, sl].astype(BF16), NT) * scale
        m = jnp.maximum(jnp.max(s_loc, axis=-1, keepdims=True), jnp.max(s_ctx, axis=-1, keepdims=True))
        e_loc = jnp.exp(s_loc - m)
        e_ctx = jnp.exp(s_ctx - m)
        inv = 1.0 / (jnp.sum(e_loc, axis=-1, keepdims=True) + jnp.sum(e_ctx, axis=-1, keepdims=True))
        o = _dot((e_loc * inv).astype(BF16), v_loc[:, sl].astype(BF16))
        o += _dot((e_ctx * inv).astype(BF16), vc_ref[:, sl].astype(BF16))
        o_ref[:, sl] = o.astype(o_ref.dtype)


def _na_bias_table(rpb):
    pat = np.arange(NA_WIN_ROWS)[:, None, None, None]
    qc = np.arange(GRID_W)[None, :, None, None]
    ki = np.arange(NA_WIN_ROWS)[None, None, :, None]
    kc = np.arange(GRID_W)[None, None, None, :]
    win_c0 = np.clip(qc - NA_WIN_COLS // 2, 0, GRID_W - NA_WIN_COLS)
    valid = (kc >= win_c0) & (kc < win_c0 + NA_WIN_COLS)
    rel_r = np.broadcast_to(ki - pat + NA_WIN_ROWS - 1, (NA_WIN_ROWS, GRID_W, NA_WIN_ROWS, GRID_W))
    rel_c = np.broadcast_to(np.clip(kc - qc, -(NA_WIN_COLS - 1), NA_WIN_COLS - 1) + NA_WIN_COLS - 1, rel_r.shape)
    valid = np.broadcast_to(valid, rel_r.shape)
    bias = rpb.astype(F32)[:, rel_r, rel_c]
    bias = jnp.where(valid[None], bias, NEG_BIG)
    return bias.reshape(NA_HEADS, NA_WIN_ROWS, GRID_W, NA_WIN_ROWS * GRID_W)


def _na_lat(p_att, cache_k, cache_v, layer, bias):
    w = BRANCH_WIDTH
    blk0 = N_CTX // DEC_SEQ
    q0 = N_CTX // GRID_W
    rows_per = DEC_SEQ // GRID_W
    kv_cache = pl.BlockSpec((None, None, PAST_LEN, w), lambda n, r: (n, layer, 0, 0))
    return pl.pallas_call(
        _na_lat_kernel,
        grid=(DEC_BATCH, GRID_ROWS),
        in_specs=[
            pl.BlockSpec((GRID_W, w), lambda n, r: (q0 + n * rows_per + r, 0)),
            pl.BlockSpec((DEC_SEQ, w), lambda n, r: (blk0 + n, 1)),
            pl.BlockSpec((DEC_SEQ, w), lambda n, r: (blk0 + n, 2)),
            kv_cache, kv_cache,
            pl.BlockSpec((NA_HEADS, None, GRID_W, NA_WIN_ROWS * GRID_W),
                         lambda n, r: (0, r - _na_row_start(r), 0, 0)),
        ],
        out_specs=pl.BlockSpec((GRID_W, w), lambda n, r: (n * rows_per + r, 0)),
        out_shape=jax.ShapeDtypeStruct((N_LAT, w), BF16),
        compiler_params=_cparams(("parallel", "arbitrary")),
        name="na_lat",
    )(p_att, p_att, p_att, cache_k, cache_v, bias)


def _rope(x, cos, sin):
    lane = lax.broadcasted_iota(jnp.int32, x.shape, 1)
    first = (lane % 32) < 16
    rot = jnp.where(first, -pltpu.roll(x, 128 - 16, 1), pltpu.roll(x, 16, 1))
    return x * cos + rot * sin


def _diff_kernel(*refs, rope, cache, lam_init, tq):
    if cache:
        q_ref, k_ref, v_ref, kc_ref, vc_ref, cos_ref, sin_ref, lam_ref, g_ref, o_ref = refs
    else:
        q_ref, k_ref, v_ref, lam_ref, g_ref, o_ref = refs
    scale = DIFF_QK_DIM ** -0.5
    lp = lam_ref[...]
    lam = (jnp.exp(jnp.sum(lp[0:1] * lp[1:2], axis=-1, keepdims=True))
           - jnp.exp(jnp.sum(lp[2:3] * lp[3:4], axis=-1, keepdims=True)) + lam_init)
    k = k_ref[...]
    v = v_ref[...]
    if rope:
        k = _rope(k, cos_ref[...], sin_ref[...])
    if cache:
        k = jnp.concatenate([k, kc_ref[...]], axis=0)
        v = jnp.concatenate([v, vc_ref[...]], axis=0)
    k = k.astype(BF16)
    v = v.astype(BF16)
    t = q_ref.shape[0]
    for i in range(t // tq):
        rows = slice(i * tq, (i + 1) * tq)
        q = q_ref[rows, :]
        if rope:
            q = _rope(q, cos_ref[rows, :], sin_ref[rows, :])
        lane = lax.broadcasted_iota(jnp.int32, q.shape, 1)
        q1 = jnp.where(lane < DIFF_QK_DIM, q, 0.0).astype(BF16)
        q2 = jnp.where(lane >= DIFF_QK_DIM, q, 0.0).astype(BF16)
        p1 = _softmax_rows(_dot(q1, k, NT) * scale)
        p2 = _softmax_rows(_dot(q2, k, NT) * scale)
        o = _dot((p1 - lam * p2).astype(BF16), v)
        o = o * lax.rsqrt(jnp.mean(o * o, axis=-1, keepdims=True) + LN_EPS)
        o_ref[rows, :] = (o * g_ref[...] * (1.0 - lam_init)).astype(o_ref.dtype)


def _diff_attn(p_att, lam_p, subln, lam_init, *, t, n_seq, blk0, cache=None, tables=None):
    hd = DIFF_V_DIM
    q_col, k_col, v_col = 3 * NA_HEADS, 4 * NA_HEADS, 5 * NA_HEADS
    in_specs = [
        pl.BlockSpec((t, hd), lambda n, h: (blk0 + n, q_col + h)),
        pl.BlockSpec((t, hd), lambda n, h: (blk0 + n, k_col + h)),
        pl.BlockSpec((t, hd), lambda n, h: (blk0 + n, v_col + h)),
    ]
    args = [p_att, p_att, p_att]
    if cache is not None:
        cache_k, cache_v, layer = cache
        spec = pl.BlockSpec((None, None, PAST_LEN, hd), lambda n, h: (n, layer, 0, h))
        tab = pl.BlockSpec((t, hd), lambda n, h: (0, 0))
        in_specs += [spec, spec, tab, tab]
        args += [cache_k, cache_v, tables[0], tables[1]]
    in_specs += [
        pl.BlockSpec((4, DIFF_QK_DIM), lambda n, h: (0, 0)),
        pl.BlockSpec((1, hd), lambda n, h: (0, h)),
    ]
    args += [lam_p, subln.reshape(1, DIFF_HEADS * hd)]
    return pl.pallas_call(
        functools.partial(_diff_kernel, rope=cache is not None, cache=cache is not None,
                          lam_init=lam_init, tq=256),
        grid=(n_seq, DIFF_HEADS),
        in_specs=in_specs,
        out_specs=pl.BlockSpec((t, hd), lambda n, h: (n, h)),
        out_shape=jax.ShapeDtypeStruct((n_seq * t, DIFF_HEADS * hd), BF16),
        compiler_params=_cparams(("parallel", "parallel")),
        name="diff_lat" if cache is not None else "diff_ctx",
    )(*args)


def _rope_tables():
    t = np.arange(DEC_SEQ)
    rows = (t // GRID_W).astype(np.float32)
    cols = (t % GRID_W).astype(np.float32)
    half = DIFF_QK_DIM // 2
    inv = jnp.asarray(ROPE_THETA, F32) ** (-jnp.arange(0, half, 2, dtype=F32) / half)
    ang_r = jnp.asarray(rows)[:, None] * inv
    ang_c = jnp.asarray(cols)[:, None] * inv
    ang = jnp.concatenate([ang_r, ang_r, ang_c, ang_c] * 2, axis=-1)
    return jnp.cos(ang), jnp.sin(ang)


def _head_sum(x, ones_bd):
    return _dot_exact_rhs(x, ones_bd)


def _rwkv_prep_kernel(cur_ref, prev_ref, next_ref, mix_ref, w0_ref, w2_ref, a0_ref, a2_ref, g2_ref,
                      kk_ref, ka_ref, rk_ref, ones_ref,
                      r_ref, v_ref, kkn_ref, lwf_ref, lwb_ref, kf_ref, kb_ref, bf_ref, bb_ref, bonus_ref, g_ref,
                      *, tm):
    i = pl.program_id(0)
    blocks_per_seq = DEC_SEQ // tm
    j = i - N_CTX // tm
    is_ctx = i < N_CTX // tm
    at_start = jnp.logical_or(is_ctx, j % blocks_per_seq == 0)
    at_end = jnp.logical_or(is_ctx, j % blocks_per_seq == blocks_per_seq - 1)
    cur = cur_ref[...]
    row = lax.broadcasted_iota(jnp.int32, cur.shape, 0)
    prev_row = jnp.where(at_start, 0.0, prev_ref[7:8, :])
    next_row = jnp.where(at_end, 0.0, next_ref[0:1, :])
    prev = jnp.where(row == 0, prev_row, pltpu.roll(cur, 1, 0))
    nxt = jnp.where(row == tm - 1, next_row, pltpu.roll(cur, tm - 1, 0))
    f = cur + mix_ref[0:1, :] * (prev - cur) + mix_ref[1:2, :] * (nxt - cur)

    w = RWKV_WIDTH
    r = f[:, 0:w]
    k = f[:, w:2 * w]
    v = f[:, 2 * w:3 * w]
    c0 = 3 * w
    wd = f[:, c0:c0 + 2 * RWKV_DECAY_RANK]
    ad = f[:, c0 + 2 * RWKV_DECAY_RANK:c0 + 2 * RWKV_DECAY_RANK + 2 * RWKV_ICL_RANK]
    gd = f[:, c0 + 2 * RWKV_DECAY_RANK + 2 * RWKV_ICL_RANK:]
    ones_bd = ones_ref[...]

    kkv = k * kk_ref[...]
    norm = jnp.sqrt(_head_sum(kkv * kkv, ones_bd))
    kkn = kkv / jnp.maximum(norm, 1e-12)
    wlin = _dot(jnp.tanh(wd).astype(BF16), w2_ref[...]) + w0_ref[...]
    alin = _dot(ad.astype(BF16), a2_ref[...]) + a0_ref[...]
    g = _dot(jax.nn.sigmoid(gd).astype(BF16), g2_ref[...])

    r_ref[...] = r
    v_ref[...] = v
    kkn_ref[...] = kkn
    g_ref[...] = g
    bonus = jnp.zeros_like(v)
    for d, (lw_ref, kd_ref, bd_ref) in enumerate(((lwf_ref, kf_ref, bf_ref), (lwb_ref, kb_ref, bb_ref))):
        z = -wlin[:, d * w:(d + 1) * w]
        softplus = jnp.maximum(z, 0.0) + jnp.log(1.0 + jnp.exp(-jnp.abs(z)))
        lw_ref[...] = -jnp.exp(-softplus - 0.5)
        a = jax.nn.sigmoid(alin[:, d * w:(d + 1) * w])
        k_d = k * (1.0 + (a - 1.0) * ka_ref[...])
        kd_ref[...] = k_d
        bd_ref[...] = kkn * a
        bonus += _head_sum(r * k_d * rk_ref[...], ones_bd) * v
    bonus_ref[...] = bonus


def _block_diag2(a, b):
    za = jnp.zeros((a.shape[0], b.shape[1]), a.dtype)
    zb = jnp.zeros((b.shape[0], a.shape[1]), b.dtype)
    return jnp.concatenate([jnp.concatenate([a, za], axis=1), jnp.concatenate([zb, b], axis=1)], axis=0)


def _rwkv_prep(p_rw, lp):
    tm = 256
    w = RWKV_WIDTH
    sub = tm // 8
    full = lambda shape: pl.BlockSpec(shape, lambda i: (0,) * len(shape))
    out_spec = pl.BlockSpec((tm, w), lambda i: (i, 0))
    ones_bd = jnp.asarray(np.kron(np.eye(RWKV_HEADS), np.ones((RWKV_HEAD_DIM, RWKV_HEAD_DIM))), BF16)
    w2_bd = _block_diag2(lp['rwkv_w2'][0], lp['rwkv_w2'][1]).astype(BF16)
    a2_bd = _block_diag2(lp['rwkv_a2'][0], lp['rwkv_a2'][1]).astype(BF16)
    n_out = 11
    return pl.pallas_call(
        functools.partial(_rwkv_prep_kernel, tm=tm),
        grid=(N_TOK // tm,),
        in_specs=[
            pl.BlockSpec((tm, RWKV_FEAT), lambda i: (i, 0)),
            pl.BlockSpec((8, RWKV_FEAT), lambda i: (jnp.maximum(i * sub - 1, 0), 0)),
            pl.BlockSpec((8, RWKV_FEAT), lambda i: (jnp.minimum((i + 1) * sub, N_TOK // 8 - 1), 0)),
            full((2, RWKV_FEAT)),
            full((1, 2 * w)), full((2 * RWKV_DECAY_RANK, 2 * w)),
            full((1, 2 * w)), full((2 * RWKV_ICL_RANK, 2 * w)),
            full((RWKV_GATE_RANK, w)),
            full((1, w)), full((1, w)), full((1, w)),
            full((w, w)),
        ],
        out_specs=[out_spec] * n_out,
        out_shape=[jax.ShapeDtypeStruct((N_TOK, w), F32)] * n_out,
        compiler_params=_cparams(("parallel",)),
        name="rwkv_prep",
    )(p_rw, p_rw, p_rw, lp['rwkv_mix'],
      lp['rwkv_w0'].reshape(1, 2 * w), w2_bd, lp['rwkv_a0'].reshape(1, 2 * w), a2_bd,
      lp['rwkv_g2'].astype(BF16),
      lp['rwkv_kk'].reshape(1, w), lp['rwkv_ka'].reshape(1, w), lp['rwkv_rk'].reshape(1, w),
      ones_bd)


def _rwkv_chunk(r, k, v, kkn, b, logw, st, forward):
    c = r.shape[0]
    row = lax.broadcasted_iota(jnp.int32, (c, c), 0)
    col = lax.broadcasted_iota(jnp.int32, (c, c), 1)
    incl = (row >= col) if forward else (row <= col)
    strict = (row > col) if forward else (row < col)
    tri = jnp.where(incl, 1.0, 0.0).astype(BF16)
    cum = _dot_exact_lhs(tri, logw)
    last = c - 1 if forward else 0
    tot = cum[last:last + 1, :]
    e_pos = jnp.exp(cum)
    e_neg = jnp.exp(-cum)
    e_tail = jnp.exp(tot - cum)
    a_t = -kkn * jnp.exp(cum - logw)
    r_t = r * e_pos
    b_t = b * e_neg
    k_t = k * e_neg
    ar = jnp.concatenate([a_t, r_t], axis=0)
    with_b = _dot3(ar, b_t, NT)
    with_k = _dot3(ar, k_t, NT)
    l_ab = jnp.where(strict, with_b[:c], 0.0)
    l_ak = jnp.where(strict, with_k[:c], 0.0)
    m_rb = jnp.where(incl, with_b[c:], 0.0)
    m_rk = jnp.where(incl, with_k[c:], 0.0)
    inv = jnp.where(row == col, 1.0, 0.0) + l_ab
    power = l_ab
    for _ in range(int(math.log2(c)) - 1):
        power = _dot3(power, power)
        inv = inv + _dot3(power, inv)
    u = _dot3(inv, _dot3(a_t, st) + _dot3(l_ak, v))
    y = _dot3(r_t, st) + _dot3(m_rb, u) + _dot3(m_rk, v)
    ones = jnp.ones((c, st.shape[1]), BF16)
    decay = jnp.exp(_dot_exact_rhs(logw, ones, TN))
    st_new = decay * st + _dot3(b * e_tail, u, TN) + _dot3(k * e_tail, v, TN)
    return y, st_new


def _rwkv_scan_kernel(rf, vf, af, wf, kf, bf, rb, vb, ab, wb, kb, bb, s0f, s0b,
                      yf, yb, sff, sfb, st_ref, *, nc):
    c = pl.program_id(2)
    hd = RWKV_HEAD_DIM

    @pl.when(c == 0)
    def _():
        st_ref[0:2] = s0f[...]
        st_ref[2:4] = s0b[...]

    dirs = ((rf, vf, af, wf, kf, bf, yf, True), (rb, vb, ab, wb, kb, bb, yb, False))
    for d, (r_ref, v_ref, a_ref, w_ref, k_ref, b_ref, y_ref, forward) in enumerate(dirs):
        ys = []
        for hh in range(2):
            sl = slice(hh * hd, (hh + 1) * hd)
            y, st_new = _rwkv_chunk(r_ref[:, sl], k_ref[:, sl], v_ref[:, sl], a_ref[:, sl], b_ref[:, sl],
                                    w_ref[:, sl], st_ref[2 * d + hh], forward)
            st_ref[2 * d + hh] = st_new
            ys.append(y)
        y_ref[...] = jnp.concatenate(ys, axis=1)

    @pl.when(c == nc - 1)
    def _():
        sff[...] = st_ref[0:2]
        sfb[...] = st_ref[2:4]


def _rwkv_scan(feats, s0f_t, s0b_t, *, t, n_seq, row0):
    r, v, kkn, lwf, lwb, kf, kb, bf, bb = feats
    cs = RWKV_CHUNK
    nc = t // cs
    blk0 = row0 // cs
    fwd = pl.BlockSpec((cs, 128), lambda n, p, c: (blk0 + n * nc + c, p))
    bwd = pl.BlockSpec((cs, 128), lambda n, p, c: (blk0 + n * nc + nc - 1 - c, p))
    st_spec = pl.BlockSpec((None, 2, RWKV_HEAD_DIM, RWKV_HEAD_DIM), lambda n, p, c: (n, p, 0, 0))
    y_f = pl.BlockSpec((cs, 128), lambda n, p, c: (n * nc + c, p))
    y_b = pl.BlockSpec((cs, 128), lambda n, p, c: (n * nc + nc - 1 - c, p))
    st_shape = jax.ShapeDtypeStruct((n_seq, RWKV_HEADS, RWKV_HEAD_DIM, RWKV_HEAD_DIM), F32)
    y_shape = jax.ShapeDtypeStruct((n_seq * t, RWKV_WIDTH), F32)
    return pl.pallas_call(
        functools.partial(_rwkv_scan_kernel, nc=nc),
        grid=(n_seq, RWKV_HEADS // 2, nc),
        in_specs=[fwd] * 6 + [bwd] * 6 + [st_spec, st_spec],
        out_specs=[y_f, y_b, st_spec, st_spec],
        out_shape=[y_shape, y_shape, st_shape, st_shape],
        scratch_shapes=[pltpu.VMEM((4, RWKV_HEAD_DIM, RWKV_HEAD_DIM), F32)],
        compiler_params=_cparams(("parallel", "parallel", "arbitrary")),
        name="rwkv_scan",
    )(r, v, kkn, lwf, kf, bf, r, v, kkn, lwb, kb, bb, s0f_t, s0b_t)


def _rwkv_out_kernel(yf_ref, yb_ref, bonus_ref, g_ref, lg_ref, lb_ref, ones_ref, o_ref):
    ones_bd = ones_ref[...]
    y = yf_ref[...] + yb_ref[...]
    mu = _head_sum(y, ones_bd) * (1.0 / RWKV_HEAD_DIM)
    yc = y - mu
    var = _head_sum(yc * yc, ones_bd) * (1.0 / RWKV_HEAD_DIM)
    yn = yc * lax.rsqrt(var + RWKV_GN_EPS) * lg_ref[...] + lb_ref[...]
    o_ref[...] = ((yn + bonus_ref[...]) * g_ref[...]).astype(o_ref.dtype)


def _rwkv_out(y_f, y_b, bonus, g, lnx_g, lnx_b):
    tm = 512
    w = RWKV_WIDTH
    row = pl.BlockSpec((tm, w), lambda i: (i, 0))
    vec = pl.BlockSpec((1, w), lambda i: (0, 0))
    ones_bd = jnp.asarray(np.kron(np.eye(RWKV_HEADS), np.ones((RWKV_HEAD_DIM, RWKV_HEAD_DIM))), BF16)
    return pl.pallas_call(
        _rwkv_out_kernel,
        grid=(N_TOK // tm,),
        in_specs=[row, row, row, row, vec, vec, pl.BlockSpec((w, w), lambda i: (0, 0))],
        out_specs=row,
        out_shape=jax.ShapeDtypeStruct((N_TOK, w), BF16),
        compiler_params=_cparams(("parallel",)),
        name="rwkv_out",
    )(y_f, y_b, bonus, g, lnx_g.reshape(1, w), lnx_b.reshape(1, w), ones_bd)


def kernel(x_prompt, x_sample, cache_na_k, cache_na_v, cache_diff_k, cache_diff_v, state_rwkv_fwd, state_rwkv_bwd, c, c_ctx, w_ada, b_ada, w_in, na_rpb, diff_lambda, diff_subln, rwkv_mix, rwkv_w0, rwkv_w2, rwkv_a0, rwkv_a2, rwkv_g2, rwkv_kk, rwkv_ka, rwkv_rk, rwkv_lnx_g, rwkv_lnx_b, w_branch, w_out, ln1_g, ln1_b, w_ffn_in, w_ffn_out, ln2_g, ln2_b):
    x = jnp.concatenate([x_prompt.reshape(N_CTX, D_MODEL), x_sample.reshape(N_LAT, D_MODEL)], axis=0)
    cond = jnp.concatenate([c_ctx[None, :], c, jnp.zeros((N_COND - 1 - DEC_BATCH, D_MODEL), F32)], axis=0)
    mods = _adaln_all(cond, w_ada, b_ada)
    mods = mods.reshape(DEPTH, N_COND, 6, D_MODEL).transpose(0, 2, 1, 3).reshape(DEPTH * 6 * N_COND, 1, D_MODEL)

    cache_na_k = cache_na_k.reshape(DEC_BATCH, DEPTH, PAST_LEN, BRANCH_WIDTH)
    cache_na_v = cache_na_v.reshape(DEC_BATCH, DEPTH, PAST_LEN, BRANCH_WIDTH)
    cache_diff_k = cache_diff_k.reshape(DEC_BATCH, DEPTH, PAST_LEN, BRANCH_WIDTH)
    cache_diff_v = cache_diff_v.reshape(DEC_BATCH, DEPTH, PAST_LEN, BRANCH_WIDTH)
    s_lat_f = jnp.swapaxes(state_rwkv_fwd, -1, -2)
    s_lat_b = jnp.swapaxes(state_rwkv_bwd, -1, -2)
    s_zero = jnp.zeros((BATCH, RWKV_HEADS, RWKV_HEAD_DIM, RWKV_HEAD_DIM), F32)
    rope_tables = _rope_tables()

    h = _modulate(x, mods, 0)
    new_na_k, new_na_v, new_diff_k, new_diff_v, new_f, new_b = [], [], [], [], [], []
    for l in range(DEPTH):
        lam_init = 0.8 - 0.6 * math.exp(-0.3 * l)
        lp = {'rwkv_mix': rwkv_mix[l], 'rwkv_w0': rwkv_w0[l], 'rwkv_w2': rwkv_w2[l], 'rwkv_a0': rwkv_a0[l],
              'rwkv_a2': rwkv_a2[l], 'rwkv_g2': rwkv_g2[l], 'rwkv_kk': rwkv_kk[l], 'rwkv_ka': rwkv_ka[l],
              'rwkv_rk': rwkv_rk[l]}
        w_in_l = w_in[l]
        c1 = ATT_WIDTH + RWKV_FEAT
        p_att = _matmul(h, w_in_l[:, :ATT_WIDTH].astype(BF16), F32, name="in_att")
        p_rw = _matmul(h, w_in_l[:, ATT_WIDTH:c1].astype(BF16), F32, tn=RWKV_FEAT // 3, name="in_rwkv")
        gate_pre = _matmul(h, w_in_l[:, c1:].astype(BF16), F32, name="in_gate")

        oa_ctx = _na_ctx(p_att)
        oa_lat = _na_lat(p_att, cache_na_k, cache_na_v, l, _na_bias_table(na_rpb[l]))
        ob_ctx = _diff_attn(p_att, diff_lambda[l], diff_subln[l], lam_init, t=SEQ, n_seq=BATCH, blk0=0)
        ob_lat = _diff_attn(p_att, diff_lambda[l], diff_subln[l], lam_init, t=DEC_SEQ, n_seq=DEC_BATCH,
                            blk0=N_CTX // DEC_SEQ, cache=(cache_diff_k, cache_diff_v, l), tables=rope_tables)
        r, v, kkn, lwf, lwb, kf, kb, bf, bb, bonus, g = _rwkv_prep(p_rw, lp)
        feats = (r, v, kkn, lwf, lwb, kf, kb, bf, bb)
        yf_c, yb_c, sf_c, sb_c = _rwkv_scan(feats, s_zero, s_zero, t=SEQ, n_seq=BATCH, row0=0)
        yf_l, yb_l, _, _ = _rwkv_scan(feats, s_lat_f[:, l], s_lat_b[:, l], t=DEC_SEQ, n_seq=DEC_BATCH, row0=N_CTX)
        oc = _rwkv_out(jnp.concatenate([yf_c, yf_l], axis=0), jnp.concatenate([yb_c, yb_l], axis=0),
                       bonus, g, rwkv_lnx_g[l], rwkv_lnx_b[l])

        oa = jnp.concatenate([oa_ctx, oa_lat], axis=0)
        ob = jnp.concatenate([ob_ctx, ob_lat], axis=0)
        merged = _merge(oa, ob, oc, gate_pre, w_branch[l].astype(BF16))
        mixed = _matmul(merged, w_out[l].astype(BF16), F32, name="w_out")
        x, h2 = _res_ln(x, mixed, mods, l, 2, ln1_g[l], ln1_b[l], l, 3)
        hid = _ffn_in(h2, w_ffn_in[l].astype(BF16))
        ff = _matmul(hid, w_ffn_out[l].astype(BF16), F32, name="ffn_out")
        nxt = l + 1 if l + 1 < DEPTH else None
        x, h = _res_ln(x, ff, mods, l, 5, ln2_g[l], ln2_b[l], nxt, 0)

        ctx = p_att[:N_CTX]
        w = BRANCH_WIDTH
        new_na_k.append(ctx[:, w:2 * w].reshape(BATCH, SEQ, NA_HEADS, NA_HEAD_DIM))
        new_na_v.append(ctx[:, 2 * w:3 * w].reshape(BATCH, SEQ, NA_HEADS, NA_HEAD_DIM))
        new_diff_k.append(ctx[:, 4 * w:5 * w].reshape(BATCH, SEQ, DIFF_HEADS, 2, DIFF_QK_DIM))
        new_diff_v.append(ctx[:, 5 * w:6 * w].reshape(BATCH, SEQ, DIFF_HEADS, DIFF_V_DIM))
        new_f.append(jnp.swapaxes(sf_c, -1, -2))
        new_b.append(jnp.swapaxes(sb_c, -1, -2))

    y_prompt = x[:N_CTX].reshape(BATCH, SEQ, D_MODEL)
    y_sample = x[N_CTX:].reshape(DEC_BATCH, DEC_SEQ, D_MODEL)
    return (y_prompt, y_sample, jnp.stack(new_na_k, axis=1), jnp.stack(new_na_v, axis=1),
            jnp.stack(new_diff_k, axis=1), jnp.stack(new_diff_v, axis=1),
            jnp.stack(new_f, axis=1), jnp.stack(new_b, axis=1))
```

```python
import functools
import math

import numpy as np
import jax
import jax.numpy as jnp
from jax import lax
from jax.experimental import pallas as pl
from jax.experimental.pallas import tpu as pltpu

F32 = jnp.float32
BF16 = jnp.bfloat16

D_MODEL = 2048
BATCH = 16
SEQ = 256
DEPTH = 4
DEC_BATCH = 4
DEC_SEQ = 1024
PAST_LEN = 256
GRID_W = 64
GRID_ROWS = DEC_SEQ // GRID_W
BRANCH_WIDTH = 512
N_BRANCH = 3
NA_HEADS = 4
NA_HEAD_DIM = 128
NA_WIN_ROWS = 8
NA_WIN_COLS = 16
NA_REL_ROWS = 2 * NA_WIN_ROWS - 1
NA_REL_COLS = 2 * NA_WIN_COLS - 1
DIFF_HEADS = 4
DIFF_QK_DIM = 64
DIFF_V_DIM = 128
RWKV_HEADS = 8
RWKV_HEAD_DIM = 64
RWKV_WIDTH = RWKV_HEADS * RWKV_HEAD_DIM
RWKV_DECAY_RANK = 64
RWKV_ICL_RANK = 64
RWKV_GATE_RANK = 128
RWKV_FEAT = 3 * RWKV_WIDTH + 2 * RWKV_DECAY_RANK + 2 * RWKV_ICL_RANK + RWKV_GATE_RANK
RWKV_GN_EPS = 64e-5
ATT_WIDTH = 6 * BRANCH_WIDTH
GATE_WIDTH = N_BRANCH * D_MODEL
FFN_HIDDEN = -(-8 * D_MODEL // (3 * 256)) * 256
ROPE_THETA = 10000.0
LN_EPS = 1e-5
ALPHA = (2.0 * DEPTH) ** 0.25

N_CTX = BATCH * SEQ
N_LAT = DEC_BATCH * DEC_SEQ
N_TOK = N_CTX + N_LAT
N_COND = 8
RWKV_CHUNK = 64
RWKV_SUPER = 256
RWKV_PASSES_PAIR = 1
RWKV_PASSES_SOLVE = 1
RWKV_PASSES_APPLY = 3
RWKV_PASSES_OUT = 1
RWKV_PASSES_STATE = 3
NEG_BIG = -1e30

NN = ((1,), (0,))
NT = ((1,), (1,))
TN = ((0,), (0,))


def _cparams(sem, vmem_mb=48):
    return pltpu.CompilerParams(dimension_semantics=sem, vmem_limit_bytes=vmem_mb * 1024 * 1024)


def _dot(a, b, dims=NN):
    return lax.dot_general(a, b, (dims, ((), ())), preferred_element_type=F32)


def _split2(x):
    hi = x.astype(BF16)
    lo = (x - hi.astype(F32)).astype(BF16)
    return hi, lo


def _dot3(a, b, dims=NN):
    ah, al = _split2(a)
    bh, bl = _split2(b)
    return _dot(ah, bh, dims) + (_dot(ah, bl, dims) + _dot(al, bh, dims))


def _dotp(a, b, dims, passes):
    if passes == 1:
        return _dot(a.astype(BF16), b.astype(BF16), dims)
    assert passes == 3
    return _dot3(a, b, dims)


def _dot_exact_lhs(a_bf16, b, dims=NN):
    b1 = b.astype(BF16)
    r1 = b - b1.astype(F32)
    b2 = r1.astype(BF16)
    b3 = (r1 - b2.astype(F32)).astype(BF16)
    return _dot(a_bf16, b1, dims) + (_dot(a_bf16, b2, dims) + _dot(a_bf16, b3, dims))


def _dot_exact_rhs(a, b_bf16, dims=NN):
    a1 = a.astype(BF16)
    r1 = a - a1.astype(F32)
    a2 = r1.astype(BF16)
    a3 = (r1 - a2.astype(F32)).astype(BF16)
    return _dot(a1, b_bf16, dims) + (_dot(a2, b_bf16, dims) + _dot(a3, b_bf16, dims))


def _normalize(x):
    mu = jnp.mean(x, axis=-1, keepdims=True)
    xc = x - mu
    var = jnp.mean(xc * xc, axis=-1, keepdims=True)
    return xc * lax.rsqrt(var + LN_EPS)


def _cond_of_row(row):
    return jnp.where(row < N_CTX, 0, 1 + (row - N_CTX) // DEC_SEQ)


def _adaln_kernel(c_ref, w_ref, b_ref, o_ref):
    c = c_ref[...]
    s = (c * jax.nn.sigmoid(c)).astype(BF16)
    o_ref[...] = _dot(s, w_ref[...].astype(BF16)) + b_ref[...]


def _adaln_all(cond, w_ada, b_ada):
    tn = 1024
    n = 6 * D_MODEL
    return pl.pallas_call(
        _adaln_kernel,
        grid=(DEPTH, n // tn),
        in_specs=[
            pl.BlockSpec((N_COND, D_MODEL), lambda l, j: (0, 0)),
            pl.BlockSpec((None, D_MODEL, tn), lambda l, j: (l, 0, j)),
            pl.BlockSpec((None, 1, tn), lambda l, j: (l, 0, j)),
        ],
        out_specs=pl.BlockSpec((None, N_COND, tn), lambda l, j: (l, 0, j)),
        out_shape=jax.ShapeDtypeStruct((DEPTH, N_COND, n), F32),
        compiler_params=_cparams(("parallel", "parallel")),
        name="adaln",
    )(cond, w_ada, b_ada.reshape(DEPTH, 1, n))


def _mod_spec(layer, which, tm):
    base = layer * 6 * N_COND + which * N_COND
    return pl.BlockSpec((None, 1, D_MODEL), lambda i: (base + _cond_of_row(i * tm), 0, 0))


def _modulate_kernel(x_ref, sh_ref, sc_ref, h_ref):
    h_ref[...] = (_normalize(x_ref[...]) * (1.0 + sc_ref[...]) + sh_ref[...]).astype(h_ref.dtype)


def _modulate(x, mods, layer):
    tm = 256
    return pl.pallas_call(
        _modulate_kernel,
        grid=(N_TOK // tm,),
        in_specs=[
            pl.BlockSpec((tm, D_MODEL), lambda i: (i, 0)),
            _mod_spec(layer, 0, tm),
            _mod_spec(layer, 1, tm),
        ],
        out_specs=pl.BlockSpec((tm, D_MODEL), lambda i: (i, 0)),
        out_shape=jax.ShapeDtypeStruct((N_TOK, D_MODEL), BF16),
        compiler_params=_cparams(("parallel",)),
        name="modulate",
    )(x, mods, mods)


def _res_ln_kernel(x_ref, z_ref, gate_ref, g_ref, b_ref, *rest, with_mod):
    y = ALPHA * x_ref[...] + gate_ref[...] * z_ref[...]
    xn = _normalize(y) * g_ref[...] + b_ref[...]
    if with_mod:
        sh_ref, sc_ref, xo_ref, h_ref = rest
        xo_ref[...] = xn
        h_ref[...] = (_normalize(xn) * (1.0 + sc_ref[...]) + sh_ref[...]).astype(h_ref.dtype)
    else:
        (xo_ref,) = rest
        xo_ref[...] = xn


def _res_ln(x, z, mods, layer, gate_idx, ln_g, ln_b, mod_layer, mod_idx):
    tm = 256
    with_mod = mod_layer is not None
    row = pl.BlockSpec((tm, D_MODEL), lambda i: (i, 0))
    vec = pl.BlockSpec((1, D_MODEL), lambda i: (0, 0))
    in_specs = [row, row, _mod_spec(layer, gate_idx, tm), vec, vec]
    args = [x, z, mods, ln_g.reshape(1, D_MODEL), ln_b.reshape(1, D_MODEL)]
    out_specs = [row]
    out_shape = [jax.ShapeDtypeStruct((N_TOK, D_MODEL), F32)]
    if with_mod:
        in_specs += [_mod_spec(mod_layer, mod_idx, tm), _mod_spec(mod_layer, mod_idx + 1, tm)]
        args += [mods, mods]
        out_specs.append(row)
        out_shape.append(jax.ShapeDtypeStruct((N_TOK, D_MODEL), BF16))
    out = pl.pallas_call(
        functools.partial(_res_ln_kernel, with_mod=with_mod),
        grid=(N_TOK // tm,),
        in_specs=in_specs,
        out_specs=out_specs,
        out_shape=out_shape,
        compiler_params=_cparams(("parallel",)),
        name="res_ln",
    )(*args)
    return (out[0], out[1]) if with_mod else (out[0], None)


def _mm_kernel(a_ref, b_ref, o_ref, *, sigmoid):
    acc = _dot(a_ref[...], b_ref[...])
    if sigmoid:
        acc = jax.nn.sigmoid(acc)
    o_ref[...] = acc.astype(o_ref.dtype)


def _matmul(a, b, out_dtype, tm=1024, tn=1024, name="matmul", sigmoid=False, vmem_mb=48):
    m, k = a.shape
    n = b.shape[1]
    assert m % tm == 0 and n % tn == 0
    return pl.pallas_call(
        functools.partial(_mm_kernel, sigmoid=sigmoid),
        grid=(m // tm, n // tn),
        in_specs=[
            pl.BlockSpec((tm, k), lambda i, j: (i, 0)),
            pl.BlockSpec((k, tn), lambda i, j: (0, j)),
        ],
        out_specs=pl.BlockSpec((tm, tn), lambda i, j: (i, j)),
        out_shape=jax.ShapeDtypeStruct((m, n), out_dtype),
        compiler_params=_cparams(("parallel", "parallel"), vmem_mb),
        name=name,
    )(a, b)


def _swiglu_kernel(a_ref, bg_ref, bu_ref, o_ref):
    a = a_ref[...]
    g = _dot(a, bg_ref[...])
    u = _dot(a, bu_ref[...])
    o_ref[...] = (g * jax.nn.sigmoid(g) * u).astype(o_ref.dtype)


def _ffn_in(h, w_ffn_in):
    tm, tn = 1024, 512
    nb = FFN_HIDDEN // tn
    return pl.pallas_call(
        _swiglu_kernel,
        grid=(N_TOK // tm, nb),
        in_specs=[
            pl.BlockSpec((tm, D_MODEL), lambda i, j: (i, 0)),
            pl.BlockSpec((D_MODEL, tn), lambda i, j: (0, j)),
            pl.BlockSpec((D_MODEL, tn), lambda i, j: (0, j + nb)),
        ],
        out_specs=pl.BlockSpec((tm, tn), lambda i, j: (i, j)),
        out_shape=jax.ShapeDtypeStruct((N_TOK, FFN_HIDDEN), BF16),
        compiler_params=_cparams(("parallel", "parallel")),
        name="ffn_in",
    )(h, w_ffn_in, w_ffn_in)


def _merge_kernel(oa_ref, ob_ref, oc_ref, ga_ref, gb_ref, gc_ref, wb_ref, o_ref):
    acc = ga_ref[...].astype(F32) * _dot(oa_ref[...], wb_ref[0])
    acc += gb_ref[...].astype(F32) * _dot(ob_ref[...], wb_ref[1])
    acc += gc_ref[...].astype(F32) * _dot(oc_ref[...], wb_ref[2])
    o_ref[...] = acc.astype(o_ref.dtype)


def _merge(oa, ob, oc, gate_pre, w_branch):
    tm, tn = 1024, 512
    nb = D_MODEL // tn
    o_spec = pl.BlockSpec((tm, BRANCH_WIDTH), lambda i, j: (i, 0))
    return pl.pallas_call(
        _merge_kernel,
        grid=(N_TOK // tm, nb),
        in_specs=[
            o_spec, o_spec, o_spec,
            pl.BlockSpec((tm, tn), lambda i, j: (i, j)),
            pl.BlockSpec((tm, tn), lambda i, j: (i, j + nb)),
            pl.BlockSpec((tm, tn), lambda i, j: (i, j + 2 * nb)),
            pl.BlockSpec((N_BRANCH, BRANCH_WIDTH, tn), lambda i, j: (0, 0, j)),
        ],
        out_specs=pl.BlockSpec((tm, tn), lambda i, j: (i, j)),
        out_shape=jax.ShapeDtypeStruct((N_TOK, D_MODEL), BF16),
        compiler_params=_cparams(("parallel", "parallel")),
        name="merge",
    )(oa, ob, oc, gate_pre, gate_pre, gate_pre, w_branch)


def _softmax_rows(s):
    m = jnp.max(s, axis=-1, keepdims=True)
    e = jnp.exp(s - m)
    return e * (1.0 / jnp.sum(e, axis=-1, keepdims=True))


def _na_ctx_kernel(q_ref, k_ref, v_ref, o_ref):
    scale = NA_HEAD_DIM ** -0.5
    for h in range(NA_HEADS):
        sl = slice(h * NA_HEAD_DIM, (h + 1) * NA_HEAD_DIM)
        q = q_ref[:, sl].astype(BF16)
        k = k_ref[:, sl].astype(BF16)
        v = v_ref[:, sl].astype(BF16)
        p = _softmax_rows(_dot(q, k, NT) * scale)
        o_ref[:, sl] = _dot(p.astype(BF16), v).astype(o_ref.dtype)


def _na_ctx(p_att):
    w = BRANCH_WIDTH
    return pl.pallas_call(
        _na_ctx_kernel,
        grid=(BATCH,),
        in_specs=[
            pl.BlockSpec((SEQ, w), lambda n: (n, 0)),
            pl.BlockSpec((SEQ, w), lambda n: (n, 1)),
            pl.BlockSpec((SEQ, w), lambda n: (n, 2)),
        ],
        out_specs=pl.BlockSpec((SEQ, w), lambda n: (n, 0)),
        out_shape=jax.ShapeDtypeStruct((N_CTX, w), BF16),
        compiler_params=_cparams(("parallel",)),
        name="na_ctx",
    )(p_att, p_att, p_att)


def _na_row_start(r):
    return jnp.clip(r - NA_WIN_ROWS // 2, 0, GRID_ROWS - NA_WIN_ROWS)


def _na_lat_kernel(q_ref, k_ref, v_ref, kc_ref, vc_ref, bias_ref, o_ref):
    scale = NA_HEAD_DIM ** -0.5
    n_loc = NA_WIN_ROWS * GRID_W
    start = pl.multiple_of(_na_row_start(pl.program_id(1)) * GRID_W, GRID_W)
    k_loc = k_ref[pl.ds(start, n_loc), :]
    v_loc = v_ref[pl.ds(start, n_loc), :]
    for h in range(NA_HEADS):
        sl = slice(h * NA_HEAD_DIM, (h + 1) * NA_HEAD_DIM)
        q = q_ref[:, sl].astype(BF16)
        s_loc = _dot(q, k_loc[:, sl].astype(BF16), NT) * scale + bias_ref[h]
        s_ctx = _dot(q, kc_ref[:, sl].astype(BF16), NT) * scale
        m = jnp.maximum(jnp.max(s_loc, axis=-1, keepdims=True), jnp.max(s_ctx, axis=-1, keepdims=True))
        e_loc = jnp.exp(s_loc - m)
        e_ctx = jnp.exp(s_ctx - m)
        inv = 1.0 / (jnp.sum(e_loc, axis=-1, keepdims=True) + jnp.sum(e_ctx, axis=-1, keepdims=True))
        o = _dot((e_loc * inv).astype(BF16), v_loc[:, sl].astype(BF16))
        o += _dot((e_ctx * inv).astype(BF16), vc_ref[:, sl].astype(BF16))
        o_ref[:, sl] = o.astype(o_ref.dtype)


def _na_bias_kernel(rows_ref, onehot_ref, mask_ref, o_ref):
    o_ref[...] = _dot_exact_rhs(rows_ref[...], onehot_ref[...]) + mask_ref[...]


def _na_bias_table(rpb):
    nr, nd = NA_WIN_ROWS, NA_REL_COLS + 1
    rows = jnp.stack([rpb[:, nr - 1 - p:2 * nr - 1 - p, :] for p in range(nr)], axis=1)
    rows = jnp.pad(rows.astype(F32), ((0, 0), (0, 0), (0, 0), (0, nd - NA_REL_COLS)))
    qc = np.arange(GRID_W)[:, None]
    kc = np.arange(GRID_W)[None, :]
    rel_c = np.clip(kc - qc, -(NA_WIN_COLS - 1), NA_WIN_COLS - 1) + NA_WIN_COLS - 1
    onehot = (rel_c[None] == np.arange(nd)[:, None, None]).reshape(nd, GRID_W * GRID_W)
    win_c0 = np.clip(qc - NA_WIN_COLS // 2, 0, GRID_W - NA_WIN_COLS)
    valid = ((kc >= win_c0) & (kc < win_c0 + NA_WIN_COLS)).reshape(1, GRID_W * GRID_W)
    n_rows = NA_HEADS * nr * nr
    full = lambda shape: pl.BlockSpec(shape, lambda: (0,) * len(shape))
    bias = pl.pallas_call(
        _na_bias_kernel,
        in_specs=[full((n_rows, nd)), full((nd, GRID_W * GRID_W)), full((1, GRID_W * GRID_W))],
        out_specs=full((n_rows, GRID_W * GRID_W)),
        out_shape=jax.ShapeDtypeStruct((n_rows, GRID_W * GRID_W), F32),
        name="na_bias",
    )(rows.reshape(n_rows, nd), jnp.asarray(onehot, BF16), jnp.asarray(np.where(valid, 0.0, NEG_BIG), F32))
    bias = bias.reshape(NA_HEADS, nr, nr, GRID_W, GRID_W).transpose(0, 1, 3, 2, 4)
    return bias.reshape(NA_HEADS, nr, GRID_W, nr * GRID_W)


def _na_lat(p_att, cache_k, cache_v, layer, bias):
    w = BRANCH_WIDTH
    blk0 = N_CTX // DEC_SEQ
    q0 = N_CTX // GRID_W
    rows_per = DEC_SEQ // GRID_W
    kv_cache = pl.BlockSpec((None, None, PAST_LEN, w), lambda n, r: (n, layer, 0, 0))
    return pl.pallas_call(
        _na_lat_kernel,
        grid=(DEC_BATCH, GRID_ROWS),
        in_specs=[
            pl.BlockSpec((GRID_W, w), lambda n, r: (q0 + n * rows_per + r, 0)),
            pl.BlockSpec((DEC_SEQ, w), lambda n, r: (blk0 + n, 1)),
            pl.BlockSpec((DEC_SEQ, w), lambda n, r: (blk0 + n, 2)),
            kv_cache, kv_cache,
            pl.BlockSpec((NA_HEADS, None, GRID_W, NA_WIN_ROWS * GRID_W),
                         lambda n, r: (0, r - _na_row_start(r), 0, 0)),
        ],
        out_specs=pl.BlockSpec((GRID_W, w), lambda n, r: (n * rows_per + r, 0)),
        out_shape=jax.ShapeDtypeStruct((N_LAT, w), BF16),
        compiler_params=_cparams(("parallel", "arbitrary")),
        name="na_lat",
    )(p_att, p_att, p_att, cache_k, cache_v, bias)


def _rope(x, cos, sin):
    lane = lax.broadcasted_iota(jnp.int32, x.shape, 1)
    first = (lane % 32) < 16
    rot = jnp.where(first, -pltpu.roll(x, 128 - 16, 1), pltpu.roll(x, 16, 1))
    return x * cos + rot * sin


def _diff_kernel(*refs, rope, cache, lam_init, tq):
    if cache:
        q_ref, k_ref, v_ref, kc_ref, vc_ref, cos_ref, sin_ref, lam_ref, g_ref, o_ref = refs
    else:
        q_ref, k_ref, v_ref, lam_ref, g_ref, o_ref = refs
    scale = DIFF_QK_DIM ** -0.5
    lp = lam_ref[...]
    lam = (jnp.exp(jnp.sum(lp[0:1] * lp[1:2], axis=-1, keepdims=True))
           - jnp.exp(jnp.sum(lp[2:3] * lp[3:4], axis=-1, keepdims=True)) + lam_init)
    k = k_ref[...]
    v = v_ref[...]
    if rope:
        k = _rope(k, cos_ref[...], sin_ref[...])
    if cache:
        k = jnp.concatenate([k, kc_ref[...]], axis=0)
        v = jnp.concatenate([v, vc_ref[...]], axis=0)
    k = k.astype(BF16)
    v = v.astype(BF16)
    t = q_ref.shape[0]
    for i in range(t // tq):
        rows = slice(i * tq, (i + 1) * tq)
        q = q_ref[rows, :]
        if rope:
            q = _rope(q, cos_ref[rows, :], sin_ref[rows, :])
        lane = lax.broadcasted_iota(jnp.int32, q.shape, 1)
        q1 = jnp.where(lane < DIFF_QK_DIM, q, 0.0).astype(BF16)
        q2 = jnp.where(lane >= DIFF_QK_DIM, q, 0.0).astype(BF16)
        p1 = _softmax_rows(_dot(q1, k, NT) * scale)
        p2 = _softmax_rows(_dot(q2, k, NT) * scale)
        o = _dot((p1 - lam * p2).astype(BF16), v)
        o = o * lax.rsqrt(jnp.mean(o * o, axis=-1, keepdims=True) + LN_EPS)
        o_ref[rows, :] = (o * g_ref[...] * (1.0 - lam_init)).astype(o_ref.dtype)


def _diff_attn(p_att, lam_p, subln, lam_init, *, t, n_seq, blk0, cache=None, tables=None):
    hd = DIFF_V_DIM
    q_col, k_col, v_col = 3 * NA_HEADS, 4 * NA_HEADS, 5 * NA_HEADS
    in_specs = [
        pl.BlockSpec((t, hd), lambda n, h: (blk0 + n, q_col + h)),
        pl.BlockSpec((t, hd), lambda n, h: (blk0 + n, k_col + h)),
        pl.BlockSpec((t, hd), lambda n, h: (blk0 + n, v_col + h)),
    ]
    args = [p_att, p_att, p_att]
    if cache is not None:
        cache_k, cache_v, layer = cache
        spec = pl.BlockSpec((None, None, PAST_LEN, hd), lambda n, h: (n, layer, 0, h))
        tab = pl.BlockSpec((t, hd), lambda n, h: (0, 0))
        in_specs += [spec, spec, tab, tab]
        args += [cache_k, cache_v, tables[0], tables[1]]
    in_specs += [
        pl.BlockSpec((4, DIFF_QK_DIM), lambda n, h: (0, 0)),
        pl.BlockSpec((1, hd), lambda n, h: (0, h)),
    ]
    args += [lam_p, subln.reshape(1, DIFF_HEADS * hd)]
    return pl.pallas_call(
        functools.partial(_diff_kernel, rope=cache is not None, cache=cache is not None,
                          lam_init=lam_init, tq=256),
        grid=(n_seq, DIFF_HEADS),
        in_specs=in_specs,
        out_specs=pl.BlockSpec((t, hd), lambda n, h: (n, h)),
        out_shape=jax.ShapeDtypeStruct((n_seq * t, DIFF_HEADS * hd), BF16),
        compiler_params=_cparams(("parallel", "parallel")),
        name="diff_lat" if cache is not None else "diff_ctx",
    )(*args)


def _rope_tables():
    t = np.arange(DEC_SEQ)
    rows = (t // GRID_W).astype(np.float32)
    cols = (t % GRID_W).astype(np.float32)
    half = DIFF_QK_DIM // 2
    inv = jnp.asarray(ROPE_THETA, F32) ** (-jnp.arange(0, half, 2, dtype=F32) / half)
    ang_r = jnp.asarray(rows)[:, None] * inv
    ang_c = jnp.asarray(cols)[:, None] * inv
    ang = jnp.concatenate([ang_r, ang_r, ang_c, ang_c] * 2, axis=-1)
    return jnp.cos(ang), jnp.sin(ang)


def _head_sum(x, ones_bd):
    return _dot_exact_rhs(x, ones_bd)


def _rwkv_prep_kernel(cur_ref, prev_ref, next_ref, mix_ref, w0_ref, w2_ref, a0_ref, a2_ref, g2_ref,
                      kk_ref, ka_ref, rk_ref, ones_ref,
                      r_ref, v_ref, kkn_ref, lwf_ref, lwb_ref, kf_ref, kb_ref, bf_ref, bb_ref, bonus_ref, g_ref,
                      *, tm):
    i = pl.program_id(0)
    blocks_per_seq = DEC_SEQ // tm
    j = i - N_CTX // tm
    is_ctx = i < N_CTX // tm
    at_start = jnp.logical_or(is_ctx, j % blocks_per_seq == 0)
    at_end = jnp.logical_or(is_ctx, j % blocks_per_seq == blocks_per_seq - 1)
    cur = cur_ref[...]
    row = lax.broadcasted_iota(jnp.int32, cur.shape, 0)
    prev_row = jnp.where(at_start, 0.0, prev_ref[7:8, :])
    next_row = jnp.where(at_end, 0.0, next_ref[0:1, :])
    prev = jnp.where(row == 0, prev_row, pltpu.roll(cur, 1, 0))
    nxt = jnp.where(row == tm - 1, next_row, pltpu.roll(cur, tm - 1, 0))
    f = cur + mix_ref[0:1, :] * (prev - cur) + mix_ref[1:2, :] * (nxt - cur)

    w = RWKV_WIDTH
    r = f[:, 0:w]
    k = f[:, w:2 * w]
    v = f[:, 2 * w:3 * w]
    c0 = 3 * w
    wd = f[:, c0:c0 + 2 * RWKV_DECAY_RANK]
    ad = f[:, c0 + 2 * RWKV_DECAY_RANK:c0 + 2 * RWKV_DECAY_RANK + 2 * RWKV_ICL_RANK]
    gd = f[:, c0 + 2 * RWKV_DECAY_RANK + 2 * RWKV_ICL_RANK:]
    ones_bd = ones_ref[...]

    kkv = k * kk_ref[...]
    norm = jnp.sqrt(_head_sum(kkv * kkv, ones_bd))
    kkn = kkv / jnp.maximum(norm, 1e-12)
    wlin = _dot(jnp.tanh(wd).astype(BF16), w2_ref[...]) + w0_ref[...]
    alin = _dot(ad.astype(BF16), a2_ref[...]) + a0_ref[...]
    g = _dot(jax.nn.sigmoid(gd).astype(BF16), g2_ref[...])

    r_ref[...] = r
    v_ref[...] = v
    kkn_ref[...] = kkn
    g_ref[...] = g
    bonus = jnp.zeros_like(v)
    for d, (lw_ref, kd_ref, bd_ref) in enumerate(((lwf_ref, kf_ref, bf_ref), (lwb_ref, kb_ref, bb_ref))):
        z = -wlin[:, d * w:(d + 1) * w]
        softplus = jnp.maximum(z, 0.0) + jnp.log(1.0 + jnp.exp(-jnp.abs(z)))
        lw_ref[...] = -jnp.exp(-softplus - 0.5)
        a = jax.nn.sigmoid(alin[:, d * w:(d + 1) * w])
        k_d = k * (1.0 + (a - 1.0) * ka_ref[...])
        kd_ref[...] = k_d
        bd_ref[...] = kkn * a
        bonus += _head_sum(r * k_d * rk_ref[...], ones_bd) * v
    bonus_ref[...] = bonus


def _block_diag2(a, b):
    za = jnp.zeros((a.shape[0], b.shape[1]), a.dtype)
    zb = jnp.zeros((b.shape[0], a.shape[1]), b.dtype)
    return jnp.concatenate([jnp.concatenate([a, za], axis=1), jnp.concatenate([zb, b], axis=1)], axis=0)


def _rwkv_prep(p_rw, lp):
    tm = 256
    w = RWKV_WIDTH
    sub = tm // 8
    full = lambda shape: pl.BlockSpec(shape, lambda i: (0,) * len(shape))
    out_spec = pl.BlockSpec((tm, w), lambda i: (i, 0))
    ones_bd = jnp.asarray(np.kron(np.eye(RWKV_HEADS), np.ones((RWKV_HEAD_DIM, RWKV_HEAD_DIM))), BF16)
    w2_bd = _block_diag2(lp['rwkv_w2'][0], lp['rwkv_w2'][1]).astype(BF16)
    a2_bd = _block_diag2(lp['rwkv_a2'][0], lp['rwkv_a2'][1]).astype(BF16)
    n_out = 11
    return pl.pallas_call(
        functools.partial(_rwkv_prep_kernel, tm=tm),
        grid=(N_TOK // tm,),
        in_specs=[
            pl.BlockSpec((tm, RWKV_FEAT), lambda i: (i, 0)),
            pl.BlockSpec((8, RWKV_FEAT), lambda i: (jnp.maximum(i * sub - 1, 0), 0)),
            pl.BlockSpec((8, RWKV_FEAT), lambda i: (jnp.minimum((i + 1) * sub, N_TOK // 8 - 1), 0)),
            full((2, RWKV_FEAT)),
            full((1, 2 * w)), full((2 * RWKV_DECAY_RANK, 2 * w)),
            full((1, 2 * w)), full((2 * RWKV_ICL_RANK, 2 * w)),
            full((RWKV_GATE_RANK, w)),
            full((1, w)), full((1, w)), full((1, w)),
            full((w, w)),
        ],
        out_specs=[out_spec] * n_out,
        out_shape=[jax.ShapeDtypeStruct((N_TOK, w), F32)] * n_out,
        compiler_params=_cparams(("parallel",)),
        name="rwkv_prep",
    )(p_rw, p_rw, p_rw, lp['rwkv_mix'],
      lp['rwkv_w0'].reshape(1, 2 * w), w2_bd, lp['rwkv_a0'].reshape(1, 2 * w), a2_bd,
      lp['rwkv_g2'].astype(BF16),
      lp['rwkv_kk'].reshape(1, w), lp['rwkv_ka'].reshape(1, w), lp['rwkv_rk'].reshape(1, w),
      ones_bd)


def _rwkv_chunk_maps(chains):
    n = len(chains)
    rs, ks, vs, kkns, bs, lws, fwds = (list(z) for z in zip(*chains))
    c, nk = rs[0].shape
    row = lax.broadcasted_iota(jnp.int32, (c, c), 0)
    col = lax.broadcasted_iota(jnp.int32, (c, c), 1)
    eye = row == col
    incl_d = {True: row >= col, False: row <= col}
    strict_d = {True: row > col, False: row < col}
    tri_d = {f: jnp.where(m, 1.0, 0.0).astype(BF16) for f, m in incl_d.items()}
    idx = range(n)
    cum = [_dot_exact_lhs(tri_d[fwds[i]], lws[i]) for i in idx]
    tot = [cum[i][c - 1:c, :] if fwds[i] else cum[i][0:1, :] for i in idx]
    e_neg = [jnp.exp(-cum[i]) for i in idx]
    a_t = [-kkns[i] * jnp.exp(cum[i] - lws[i]) for i in idx]
    r_t = [rs[i] * jnp.exp(cum[i]) for i in idx]
    ar = [jnp.concatenate([a_t[i], r_t[i]], axis=0) for i in idx]
    with_b = [_dotp(ar[i], bs[i] * e_neg[i], NT, RWKV_PASSES_PAIR) for i in idx]
    with_k = [_dotp(ar[i], ks[i] * e_neg[i], NT, RWKV_PASSES_PAIR) for i in idx]
    l_ab = [jnp.where(strict_d[fwds[i]], with_b[i][:c], 0.0) for i in idx]
    l_ak = [jnp.where(strict_d[fwds[i]], with_k[i][:c], 0.0) for i in idx]
    m_rb = [jnp.where(incl_d[fwds[i]], with_b[i][c:], 0.0) for i in idx]
    m_rk = [jnp.where(incl_d[fwds[i]], with_k[i][c:], 0.0) for i in idx]
    lakv = [_dotp(l_ak[i], vs[i], NN, RWKV_PASSES_APPLY) for i in idx]
    mrkv = [_dotp(m_rk[i], vs[i], NN, RWKV_PASSES_OUT) for i in idx]
    e_tail = [jnp.exp(tot[i] - cum[i]) for i in idx]
    kwv = [_dotp(ks[i] * e_tail[i], vs[i], TN, RWKV_PASSES_APPLY) for i in idx]
    same = lambda s: (row // s) == (col // s)
    inv = [jnp.where(eye, 1.0, 0.0) + jnp.where(same(2), l_ab[i], 0.0) for i in idx]
    size = 4
    while size <= c:
        part = jnp.logical_and(same(size), jnp.logical_not(same(size // 2)))
        half = [_dotp(inv[i], jnp.where(part, l_ab[i], 0.0), NN, RWKV_PASSES_SOLVE) for i in idx]
        inv = [inv[i] + _dotp(half[i], inv[i], NN, RWKV_PASSES_SOLVE) for i in idx]
        size *= 2
    pq1 = [_dotp(inv[i], jnp.concatenate([a_t[i], lakv[i]], axis=1), NN, RWKV_PASSES_APPLY) for i in idx]
    rb = [_dotp(m_rb[i], pq1[i], NN, RWKV_PASSES_OUT) for i in idx]
    sb = [_dotp(bs[i] * e_tail[i], pq1[i], TN, RWKV_PASSES_APPLY) for i in idx]
    out = []
    for i in idx:
        p2 = r_t[i] + rb[i][:, :nk]
        q2 = rb[i][:, nk:] + mrkv[i]
        p3 = jnp.where(eye, jnp.exp(tot[i]), 0.0) + sb[i][:, :nk]
        q3 = sb[i][:, nk:] + kwv[i]
        out.append((jnp.concatenate([p2, p3], axis=0), q2, q3))
    return out


def _rwkv_scan_kernel(rf, vf, af, wf, kf, bf, rb, vb, ab, wb, kb, bb, s0f, s0b,
                      yf, yb, sff, sfb, st_ref, *, n_super, n_chunk):
    sup = pl.program_id(2)
    hd = RWKV_HEAD_DIM
    cs = RWKV_CHUNK

    @pl.when(sup == 0)
    def _():
        st_ref[0:2] = s0f[...]
        st_ref[2:4] = s0b[...]

    dirs = ((rf, vf, af, wf, kf, bf, yf, True), (rb, vb, ab, wb, kb, bb, yb, False))
    keys, chains = [], []
    for d, (r_ref, v_ref, a_ref, w_ref, k_ref, b_ref, _, forward) in enumerate(dirs):
        for ci in range(n_chunk):
            rows = slice(ci * cs, (ci + 1) * cs)
            for hh in range(2):
                sl = slice(hh * hd, (hh + 1) * hd)
                keys.append((d, ci, hh))
                chains.append((r_ref[rows, sl], k_ref[rows, sl], v_ref[rows, sl],
                               a_ref[rows, sl], b_ref[rows, sl], w_ref[rows, sl], forward))
    maps = dict(zip(keys, _rwkv_chunk_maps(chains)))

    seqs = [(d, hh) for d in range(2) for hh in range(2)]
    st = {s: st_ref[2 * s[0] + s[1]] for s in seqs}
    ys = {}
    for step in range(n_chunk):
        for d, hh in seqs:
            ci = step if dirs[d][-1] else n_chunk - 1 - step
            p23, q2, q3 = maps[d, ci, hh]
            res = _dotp(p23, st[d, hh], NN, RWKV_PASSES_STATE)
            ys[d, ci, hh] = res[:cs] + q2
            st[d, hh] = res[cs:] + q3
    for d, hh in seqs:
        st_ref[2 * d + hh] = st[d, hh]
    for d in range(2):
        y_ref = dirs[d][6]
        for ci in range(n_chunk):
            y_ref[ci * cs:(ci + 1) * cs, :] = jnp.concatenate([ys[d, ci, 0], ys[d, ci, 1]], axis=1)

    @pl.when(sup == n_super - 1)
    def _():
        sff[...] = st_ref[0:2]
        sfb[...] = st_ref[2:4]


def _rwkv_scan(feats, s0f_t, s0b_t, *, t, n_seq, row0):
    r, v, kkn, lwf, lwb, kf, kb, bf, bb = feats
    rows = RWKV_SUPER
    n_super = t // rows
    blk0 = row0 // rows
    fwd = pl.BlockSpec((rows, 128), lambda n, p, s: (blk0 + n * n_super + s, p))
    bwd = pl.BlockSpec((rows, 128), lambda n, p, s: (blk0 + n * n_super + n_super - 1 - s, p))
    st_spec = pl.BlockSpec((None, 2, RWKV_HEAD_DIM, RWKV_HEAD_DIM), lambda n, p, s: (n, p, 0, 0))
    y_f = pl.BlockSpec((rows, 128), lambda n, p, s: (n * n_super + s, p))
    y_b = pl.BlockSpec((rows, 128), lambda n, p, s: (n * n_super + n_super - 1 - s, p))
    st_shape = jax.ShapeDtypeStruct((n_seq, RWKV_HEADS, RWKV_HEAD_DIM, RWKV_HEAD_DIM), F32)
    y_shape = jax.ShapeDtypeStruct((n_seq * t, RWKV_WIDTH), F32)
    return pl.pallas_call(
        functools.partial(_rwkv_scan_kernel, n_super=n_super, n_chunk=rows // RWKV_CHUNK),
        grid=(n_seq, RWKV_HEADS // 2, n_super),
        in_specs=[fwd] * 6 + [bwd] * 6 + [st_spec, st_spec],
        out_specs=[y_f, y_b, st_spec, st_spec],
        out_shape=[y_shape, y_shape, st_shape, st_shape],
        scratch_shapes=[pltpu.VMEM((4, RWKV_HEAD_DIM, RWKV_HEAD_DIM), F32)],
        compiler_params=_cparams(("parallel", "parallel", "arbitrary")),
        name="rwkv_scan",
    )(r, v, kkn, lwf, kf, bf, r, v, kkn, lwb, kb, bb, s0f_t, s0b_t)


def _rwkv_mixer(p_rw, s_lat_f, s_lat_b, lp):
    r, v, kkn, lwf, lwb, kf, kb, bf, bb, bonus, g = _rwkv_prep(p_rw, lp)
    feats = (r, v, kkn, lwf, lwb, kf, kb, bf, bb)
    s_zero = jnp.zeros((BATCH, RWKV_HEADS, RWKV_HEAD_DIM, RWKV_HEAD_DIM), F32)
    yf_c, yb_c, sf_c, sb_c = _rwkv_scan(feats, s_zero, s_zero, t=SEQ, n_seq=BATCH, row0=0)
    yf_l, yb_l, _, _ = _rwkv_scan(feats, jnp.swapaxes(s_lat_f, -1, -2), jnp.swapaxes(s_lat_b, -1, -2),
                                  t=DEC_SEQ, n_seq=DEC_BATCH, row0=N_CTX)
    oc = _rwkv_out(jnp.concatenate([yf_c, yf_l], axis=0), jnp.concatenate([yb_c, yb_l], axis=0),
                   bonus, g, lp['rwkv_lnx_g'], lp['rwkv_lnx_b'])
    return oc, jnp.swapaxes(sf_c, -1, -2), jnp.swapaxes(sb_c, -1, -2)


def _rwkv_out_kernel(yf_ref, yb_ref, bonus_ref, g_ref, lg_ref, lb_ref, ones_ref, o_ref):
    ones_bd = ones_ref[...]
    y = yf_ref[...] + yb_ref[...]
    mu = _head_sum(y, ones_bd) * (1.0 / RWKV_HEAD_DIM)
    yc = y - mu
    var = _head_sum(yc * yc, ones_bd) * (1.0 / RWKV_HEAD_DIM)
    yn = yc * lax.rsqrt(var + RWKV_GN_EPS) * lg_ref[...] + lb_ref[...]
    o_ref[...] = ((yn + bonus_ref[...]) * g_ref[...]).astype(o_ref.dtype)


def _rwkv_out(y_f, y_b, bonus, g, lnx_g, lnx_b):
    tm = 512
    w = RWKV_WIDTH
    row = pl.BlockSpec((tm, w), lambda i: (i, 0))
    vec = pl.BlockSpec((1, w), lambda i: (0, 0))
    ones_bd = jnp.asarray(np.kron(np.eye(RWKV_HEADS), np.ones((RWKV_HEAD_DIM, RWKV_HEAD_DIM))), BF16)
    return pl.pallas_call(
        _rwkv_out_kernel,
        grid=(N_TOK // tm,),
        in_specs=[row, row, row, row, vec, vec, pl.BlockSpec((w, w), lambda i: (0, 0))],
        out_specs=row,
        out_shape=jax.ShapeDtypeStruct((N_TOK, w), BF16),
        compiler_params=_cparams(("parallel",)),
        name="rwkv_out",
    )(y_f, y_b, bonus, g, lnx_g.reshape(1, w), lnx_b.reshape(1, w), ones_bd)


def kernel(x_prompt, x_sample, cache_na_k, cache_na_v, cache_diff_k, cache_diff_v, state_rwkv_fwd, state_rwkv_bwd, c, c_ctx, w_ada, b_ada, w_in, na_rpb, diff_lambda, diff_subln, rwkv_mix, rwkv_w0, rwkv_w2, rwkv_a0, rwkv_a2, rwkv_g2, rwkv_kk, rwkv_ka, rwkv_rk, rwkv_lnx_g, rwkv_lnx_b, w_branch, w_out, ln1_g, ln1_b, w_ffn_in, w_ffn_out, ln2_g, ln2_b):
    x = jnp.concatenate([x_prompt.reshape(N_CTX, D_MODEL), x_sample.reshape(N_LAT, D_MODEL)], axis=0)
    cond = jnp.concatenate([c_ctx[None, :], c, jnp.zeros((N_COND - 1 - DEC_BATCH, D_MODEL), F32)], axis=0)
    mods = _adaln_all(cond, w_ada, b_ada)
    mods = mods.reshape(DEPTH, N_COND, 6, D_MODEL).transpose(0, 2, 1, 3).reshape(DEPTH * 6 * N_COND, 1, D_MODEL)

    cache_na_k = cache_na_k.reshape(DEC_BATCH, DEPTH, PAST_LEN, BRANCH_WIDTH)
    cache_na_v = cache_na_v.reshape(DEC_BATCH, DEPTH, PAST_LEN, BRANCH_WIDTH)
    cache_diff_k = cache_diff_k.reshape(DEC_BATCH, DEPTH, PAST_LEN, BRANCH_WIDTH)
    cache_diff_v = cache_diff_v.reshape(DEC_BATCH, DEPTH, PAST_LEN, BRANCH_WIDTH)
    rope_tables = _rope_tables()

    h = _modulate(x, mods, 0)
    new_na_k, new_na_v, new_diff_k, new_diff_v, new_f, new_b = [], [], [], [], [], []
    for l in range(DEPTH):
        lam_init = 0.8 - 0.6 * math.exp(-0.3 * l)
        lp = {'rwkv_mix': rwkv_mix[l], 'rwkv_w0': rwkv_w0[l], 'rwkv_w2': rwkv_w2[l], 'rwkv_a0': rwkv_a0[l],
              'rwkv_a2': rwkv_a2[l], 'rwkv_g2': rwkv_g2[l], 'rwkv_kk': rwkv_kk[l], 'rwkv_ka': rwkv_ka[l],
              'rwkv_rk': rwkv_rk[l], 'rwkv_lnx_g': rwkv_lnx_g[l], 'rwkv_lnx_b': rwkv_lnx_b[l]}
        w_in_l = w_in[l]
        c1 = ATT_WIDTH + RWKV_FEAT
        p_att = _matmul(h, w_in_l[:, :ATT_WIDTH].astype(BF16), F32, name="in_att")
        p_rw = _matmul(h, w_in_l[:, ATT_WIDTH:c1].astype(BF16), F32, tn=RWKV_FEAT // 3, name="in_rwkv")
        gates = _matmul(h, w_in_l[:, c1:].astype(BF16), BF16, name="in_gate", sigmoid=True)

        oa_ctx = _na_ctx(p_att)
        oa_lat = _na_lat(p_att, cache_na_k, cache_na_v, l, _na_bias_table(na_rpb[l]))
        ob_ctx = _diff_attn(p_att, diff_lambda[l], diff_subln[l], lam_init, t=SEQ, n_seq=BATCH, blk0=0)
        ob_lat = _diff_attn(p_att, diff_lambda[l], diff_subln[l], lam_init, t=DEC_SEQ, n_seq=DEC_BATCH,
                            blk0=N_CTX // DEC_SEQ, cache=(cache_diff_k, cache_diff_v, l), tables=rope_tables)
        oc, sf_c, sb_c = _rwkv_mixer(p_rw, state_rwkv_fwd[:, l], state_rwkv_bwd[:, l], lp)

        oa = jnp.concatenate([oa_ctx, oa_lat], axis=0)
        ob = jnp.concatenate([ob_ctx, ob_lat], axis=0)
        merged = _merge(oa, ob, oc, gates, w_branch[l].astype(BF16))
        mixed = _matmul(merged, w_out[l].astype(BF16), F32, name="w_out")
        x, h2 = _res_ln(x, mixed, mods, l, 2, ln1_g[l], ln1_b[l], l, 3)
        hid = _ffn_in(h2, w_ffn_in[l].astype(BF16))
        ff = _matmul(hid, w_ffn_out[l].astype(BF16), F32, tn=512, name="ffn_out", vmem_mb=56)
        nxt = l + 1 if l + 1 < DEPTH else None
        x, h = _res_ln(x, ff, mods, l, 5, ln2_g[l], ln2_b[l], nxt, 0)

        ctx = p_att[:N_CTX]
        w = BRANCH_WIDTH
        new_na_k.append(ctx[:, w:2 * w].reshape(BATCH, SEQ, NA_HEADS, NA_HEAD_DIM))
        new_na_v.append(ctx[:, 2 * w:3 * w].reshape(BATCH, SEQ, NA_HEADS, NA_HEAD_DIM))
        new_diff_k.append(ctx[:, 4 * w:5 * w].reshape(BATCH, SEQ, DIFF_HEADS, 2, DIFF_QK_DIM))
        new_diff_v.append(ctx[:, 5 * w:6 * w].reshape(BATCH, SEQ, DIFF_HEADS, DIFF_V_DIM))
        new_f.append(sf_c)
        new_b.append(sb_c)

    y_prompt = x[:N_CTX].reshape(BATCH, SEQ, D_MODEL)
    y_sample = x[N_CTX:].reshape(DEC_BATCH, DEC_SEQ, D_MODEL)
    return (y_prompt, y_sample, jnp.stack(new_na_k, axis=1), jnp.stack(new_na_v, axis=1),
            jnp.stack(new_diff_k, axis=1), jnp.stack(new_diff_v, axis=1),
            jnp.stack(new_f, axis=1), jnp.stack(new_b, axis=1))
```

```python
import functools
import math

import numpy as np
import jax
import jax.numpy as jnp
from jax import lax
from jax.experimental import pallas as pl
from jax.experimental.pallas import tpu as pltpu

F32 = jnp.float32
BF16 = jnp.bfloat16

D_MODEL = 2048
BATCH = 16
SEQ = 256
DEPTH = 4
DEC_BATCH = 4
DEC_SEQ = 1024
PAST_LEN = 256
GRID_W = 64
GRID_ROWS = DEC_SEQ // GRID_W
BRANCH_WIDTH = 512
N_BRANCH = 3
NA_HEADS = 4
NA_HEAD_DIM = 128
NA_WIN_ROWS = 8
NA_WIN_COLS = 16
NA_REL_ROWS = 2 * NA_WIN_ROWS - 1
NA_REL_COLS = 2 * NA_WIN_COLS - 1
DIFF_HEADS = 4
DIFF_QK_DIM = 64
DIFF_V_DIM = 128
RWKV_HEADS = 8
RWKV_HEAD_DIM = 64
RWKV_WIDTH = RWKV_HEADS * RWKV_HEAD_DIM
RWKV_DECAY_RANK = 64
RWKV_ICL_RANK = 64
RWKV_GATE_RANK = 128
RWKV_FEAT = 3 * RWKV_WIDTH + 2 * RWKV_DECAY_RANK + 2 * RWKV_ICL_RANK + RWKV_GATE_RANK
RWKV_GN_EPS = 64e-5
ATT_WIDTH = 6 * BRANCH_WIDTH
GATE_WIDTH = N_BRANCH * D_MODEL
FFN_HIDDEN = -(-8 * D_MODEL // (3 * 256)) * 256
ROPE_THETA = 10000.0
LN_EPS = 1e-5
ALPHA = (2.0 * DEPTH) ** 0.25

N_CTX = BATCH * SEQ
N_LAT = DEC_BATCH * DEC_SEQ
N_TOK = N_CTX + N_LAT
N_COND = 8
RWKV_CHUNK = 64
RWKV_SUPER = 256
RWKV_PASSES_PAIR = 1
RWKV_PASSES_SOLVE = 1
RWKV_PASSES_APPLY = 3
RWKV_PASSES_OUT = 1
RWKV_PASSES_STATE = 3
NEG_BIG = -1e30

NN = ((1,), (0,))
NT = ((1,), (1,))
TN = ((0,), (0,))


def _cparams(sem, vmem_mb=48):
    return pltpu.CompilerParams(dimension_semantics=sem, vmem_limit_bytes=vmem_mb * 1024 * 1024)


def _dot(a, b, dims=NN):
    return lax.dot_general(a, b, (dims, ((), ())), preferred_element_type=F32)


def _split2(x):
    hi = x.astype(BF16)
    lo = (x - hi.astype(F32)).astype(BF16)
    return hi, lo


def _dot3(a, b, dims=NN):
    ah, al = _split2(a)
    bh, bl = _split2(b)
    return _dot(ah, bh, dims) + (_dot(ah, bl, dims) + _dot(al, bh, dims))


def _dotp(a, b, dims, passes):
    if passes == 1:
        return _dot(a.astype(BF16), b.astype(BF16), dims)
    assert passes == 3
    return _dot3(a, b, dims)


def _dot_exact_lhs(a_bf16, b, dims=NN):
    b1 = b.astype(BF16)
    r1 = b - b1.astype(F32)
    b2 = r1.astype(BF16)
    b3 = (r1 - b2.astype(F32)).astype(BF16)
    return _dot(a_bf16, b1, dims) + (_dot(a_bf16, b2, dims) + _dot(a_bf16, b3, dims))


def _dot_exact_rhs(a, b_bf16, dims=NN):
    a1 = a.astype(BF16)
    r1 = a - a1.astype(F32)
    a2 = r1.astype(BF16)
    a3 = (r1 - a2.astype(F32)).astype(BF16)
    return _dot(a1, b_bf16, dims) + (_dot(a2, b_bf16, dims) + _dot(a3, b_bf16, dims))


def _normalize(x):
    mu = jnp.mean(x, axis=-1, keepdims=True)
    xc = x - mu
    var = jnp.mean(xc * xc, axis=-1, keepdims=True)
    return xc * lax.rsqrt(var + LN_EPS)


def _cond_of_row(row):
    return jnp.where(row < N_CTX, 0, 1 + (row - N_CTX) // DEC_SEQ)


def _adaln_kernel(c_ref, w_ref, b_ref, o_ref):
    c = c_ref[...]
    s = (c * jax.nn.sigmoid(c)).astype(BF16)
    o_ref[...] = _dot(s, w_ref[...].astype(BF16)) + b_ref[...]


def _adaln_all(cond, w_ada, b_ada):
    tn = 1024
    n = 6 * D_MODEL
    return pl.pallas_call(
        _adaln_kernel,
        grid=(DEPTH, n // tn),
        in_specs=[
            pl.BlockSpec((N_COND, D_MODEL), lambda l, j: (0, 0)),
            pl.BlockSpec((None, D_MODEL, tn), lambda l, j: (l, 0, j)),
            pl.BlockSpec((None, 1, tn), lambda l, j: (l, 0, j)),
        ],
        out_specs=pl.BlockSpec((None, N_COND, tn), lambda l, j: (l, 0, j)),
        out_shape=jax.ShapeDtypeStruct((DEPTH, N_COND, n), F32),
        compiler_params=_cparams(("parallel", "parallel")),
        name="adaln",
    )(cond, w_ada, b_ada.reshape(DEPTH, 1, n))


def _mod_spec(layer, which, tm):
    base = layer * 6 * N_COND + which * N_COND
    return pl.BlockSpec((None, 1, D_MODEL), lambda i: (base + _cond_of_row(i * tm), 0, 0))


def _modulate_kernel(x_ref, sh_ref, sc_ref, h_ref):
    h_ref[...] = (_normalize(x_ref[...]) * (1.0 + sc_ref[...]) + sh_ref[...]).astype(h_ref.dtype)


def _modulate(x, mods, layer):
    tm = 256
    return pl.pallas_call(
        _modulate_kernel,
        grid=(N_TOK // tm,),
        in_specs=[
            pl.BlockSpec((tm, D_MODEL), lambda i: (i, 0)),
            _mod_spec(layer, 0, tm),
            _mod_spec(layer, 1, tm),
        ],
        out_specs=pl.BlockSpec((tm, D_MODEL), lambda i: (i, 0)),
        out_shape=jax.ShapeDtypeStruct((N_TOK, D_MODEL), BF16),
        compiler_params=_cparams(("parallel",)),
        name="modulate",
    )(x, mods, mods)


def _res_ln_kernel(x_ref, z_ref, gate_ref, g_ref, b_ref, *rest, with_mod):
    y = ALPHA * x_ref[...] + gate_ref[...] * z_ref[...]
    xn = _normalize(y) * g_ref[...] + b_ref[...]
    if with_mod:
        sh_ref, sc_ref, xo_ref, h_ref = rest
        xo_ref[...] = xn
        h_ref[...] = (_normalize(xn) * (1.0 + sc_ref[...]) + sh_ref[...]).astype(h_ref.dtype)
    else:
        (xo_ref,) = rest
        xo_ref[...] = xn


def _res_ln(x, z, mods, layer, gate_idx, ln_g, ln_b, mod_layer, mod_idx):
    tm = 256
    with_mod = mod_layer is not None
    row = pl.BlockSpec((tm, D_MODEL), lambda i: (i, 0))
    vec = pl.BlockSpec((1, D_MODEL), lambda i: (0, 0))
    in_specs = [row, row, _mod_spec(layer, gate_idx, tm), vec, vec]
    args = [x, z, mods, ln_g.reshape(1, D_MODEL), ln_b.reshape(1, D_MODEL)]
    out_specs = [row]
    out_shape = [jax.ShapeDtypeStruct((N_TOK, D_MODEL), F32)]
    if with_mod:
        in_specs += [_mod_spec(mod_layer, mod_idx, tm), _mod_spec(mod_layer, mod_idx + 1, tm)]
        args += [mods, mods]
        out_specs.append(row)
        out_shape.append(jax.ShapeDtypeStruct((N_TOK, D_MODEL), BF16))
    out = pl.pallas_call(
        functools.partial(_res_ln_kernel, with_mod=with_mod),
        grid=(N_TOK // tm,),
        in_specs=in_specs,
        out_specs=out_specs,
        out_shape=out_shape,
        compiler_params=_cparams(("parallel",)),
        name="res_ln",
    )(*args)
    return (out[0], out[1]) if with_mod else (out[0], None)


def _mm_kernel(a_ref, b_ref, o_ref, *, sigmoid):
    acc = _dot(a_ref[...], b_ref[...])
    if sigmoid:
        acc = jax.nn.sigmoid(acc)
    o_ref[...] = acc.astype(o_ref.dtype)


def _matmul(a, w, layer, out_dtype, col0=0, n=None, tm=1024, tn=1024, name="matmul", sigmoid=False, vmem_mb=48):
    m, k = a.shape
    n = w.shape[2] if n is None else n
    assert m % tm == 0 and n % tn == 0 and col0 % tn == 0
    j0 = col0 // tn
    return pl.pallas_call(
        functools.partial(_mm_kernel, sigmoid=sigmoid),
        grid=(m // tm, n // tn),
        in_specs=[
            pl.BlockSpec((tm, k), lambda i, j: (i, 0)),
            pl.BlockSpec((None, k, tn), lambda i, j: (layer, 0, j0 + j)),
        ],
        out_specs=pl.BlockSpec((tm, tn), lambda i, j: (i, j)),
        out_shape=jax.ShapeDtypeStruct((m, n), out_dtype),
        compiler_params=_cparams(("parallel", "parallel"), vmem_mb),
        name=name,
    )(a, w)


def _swiglu_kernel(a_ref, wg_ref, wu_ref, o_ref, wg16_ref, wu16_ref):
    @pl.when(pl.program_id(1) == 0)
    def _():
        wg16_ref[...] = wg_ref[...].astype(BF16)
        wu16_ref[...] = wu_ref[...].astype(BF16)

    a = a_ref[...]
    g = _dot(a, wg16_ref[...])
    u = _dot(a, wu16_ref[...])
    o_ref[...] = (g * jax.nn.sigmoid(g) * u).astype(o_ref.dtype)


def _ffn_in(h, w_ffn_in, layer):
    tm, tn = 1024, 512
    nb = FFN_HIDDEN // tn
    return pl.pallas_call(
        _swiglu_kernel,
        grid=(nb, N_TOK // tm),
        in_specs=[
            pl.BlockSpec((tm, D_MODEL), lambda j, i: (i, 0)),
            pl.BlockSpec((None, D_MODEL, tn), lambda j, i: (layer, 0, j)),
            pl.BlockSpec((None, D_MODEL, tn), lambda j, i: (layer, 0, j + nb)),
        ],
        out_specs=pl.BlockSpec((tm, tn), lambda j, i: (i, j)),
        out_shape=jax.ShapeDtypeStruct((N_TOK, FFN_HIDDEN), BF16),
        scratch_shapes=[pltpu.VMEM((D_MODEL, tn), BF16), pltpu.VMEM((D_MODEL, tn), BF16)],
        compiler_params=_cparams(("parallel", "arbitrary")),
        name="ffn_in",
    )(h, w_ffn_in, w_ffn_in)


def _merge_kernel(oa_ref, ob_ref, oc_ref, ga_ref, gb_ref, gc_ref, wb_ref, o_ref):
    acc = ga_ref[...].astype(F32) * _dot(oa_ref[...], wb_ref[0])
    acc += gb_ref[...].astype(F32) * _dot(ob_ref[...], wb_ref[1])
    acc += gc_ref[...].astype(F32) * _dot(oc_ref[...], wb_ref[2])
    o_ref[...] = acc.astype(o_ref.dtype)


def _merge(oa, ob, oc, gate_pre, w_branch, layer):
    tm, tn = 1024, 512
    nb = D_MODEL // tn
    o_spec = pl.BlockSpec((tm, BRANCH_WIDTH), lambda i, j: (i, 0))
    return pl.pallas_call(
        _merge_kernel,
        grid=(N_TOK // tm, nb),
        in_specs=[
            o_spec, o_spec, o_spec,
            pl.BlockSpec((tm, tn), lambda i, j: (i, j)),
            pl.BlockSpec((tm, tn), lambda i, j: (i, j + nb)),
            pl.BlockSpec((tm, tn), lambda i, j: (i, j + 2 * nb)),
            pl.BlockSpec((None, N_BRANCH, BRANCH_WIDTH, tn), lambda i, j: (layer, 0, 0, j)),
        ],
        out_specs=pl.BlockSpec((tm, tn), lambda i, j: (i, j)),
        out_shape=jax.ShapeDtypeStruct((N_TOK, D_MODEL), BF16),
        compiler_params=_cparams(("parallel", "parallel")),
        name="merge",
    )(oa, ob, oc, gate_pre, gate_pre, gate_pre, w_branch)


def _softmax_rows(s):
    m = jnp.max(s, axis=-1, keepdims=True)
    e = jnp.exp(s - m)
    return e * (1.0 / jnp.sum(e, axis=-1, keepdims=True))


def _na_ctx_kernel(q_ref, k_ref, v_ref, *rest):
    o_ref, new_k_ref, new_v_ref = rest[-3:]
    scale = NA_HEAD_DIM ** -0.5
    new_k_ref[...] = k_ref[...]
    new_v_ref[...] = v_ref[...]
    for h in range(NA_HEADS):
        sl = slice(h * NA_HEAD_DIM, (h + 1) * NA_HEAD_DIM)
        q = q_ref[:, sl].astype(BF16)
        k = k_ref[:, sl].astype(BF16)
        v = v_ref[:, sl].astype(BF16)
        p = _softmax_rows(_dot(q, k, NT) * scale)
        o_ref[:, sl] = _dot(p.astype(BF16), v).astype(o_ref.dtype)


_ANY_SPEC = pl.BlockSpec(memory_space=pl.ANY)


def _cache_stack_shape():
    return jax.ShapeDtypeStruct((BATCH, DEPTH, SEQ, BRANCH_WIDTH), F32)


def _na_ctx(p_att, layer, stacks):
    w = BRANCH_WIDTH
    stack_spec = pl.BlockSpec((None, None, SEQ, w), lambda n: (n, layer, 0, 0))
    extra = [] if stacks is None else list(stacks)
    return pl.pallas_call(
        _na_ctx_kernel,
        grid=(BATCH,),
        in_specs=[
            pl.BlockSpec((SEQ, w), lambda n: (n, 0)),
            pl.BlockSpec((SEQ, w), lambda n: (n, 1)),
            pl.BlockSpec((SEQ, w), lambda n: (n, 2)),
        ] + [_ANY_SPEC] * len(extra),
        out_specs=[pl.BlockSpec((SEQ, w), lambda n: (n, 0)), stack_spec, stack_spec],
        out_shape=[jax.ShapeDtypeStruct((N_TOK, w), BF16), _cache_stack_shape(), _cache_stack_shape()],
        input_output_aliases={3: 1, 4: 2} if extra else {},
        compiler_params=_cparams(("parallel",)),
        name="na_ctx",
    )(p_att, p_att, p_att, *extra)


def _na_row_start(r):
    return jnp.clip(r - NA_WIN_ROWS // 2, 0, GRID_ROWS - NA_WIN_ROWS)


def _na_lat_kernel(q_ref, k_ref, v_ref, kc_ref, vc_ref, bias_ref, _, o_ref):
    scale = NA_HEAD_DIM ** -0.5
    n_loc = NA_WIN_ROWS * GRID_W
    start = pl.multiple_of(_na_row_start(pl.program_id(1)) * GRID_W, GRID_W)
    k_loc = k_ref[pl.ds(start, n_loc), :]
    v_loc = v_ref[pl.ds(start, n_loc), :]
    for h in range(NA_HEADS):
        sl = slice(h * NA_HEAD_DIM, (h + 1) * NA_HEAD_DIM)
        q = q_ref[:, sl].astype(BF16)
        s_loc = _dot(q, k_loc[:, sl].astype(BF16), NT) * scale + bias_ref[h]
        s_ctx = _dot(q, kc_ref[:, sl].astype(BF16), NT) * scale
        m = jnp.maximum(jnp.max(s_loc, axis=-1, keepdims=True), jnp.max(s_ctx, axis=-1, keepdims=True))
        e_loc = jnp.exp(s_loc - m)
        e_ctx = jnp.exp(s_ctx - m)
        inv = 1.0 / (jnp.sum(e_loc, axis=-1, keepdims=True) + jnp.sum(e_ctx, axis=-1, keepdims=True))
        o = _dot((e_loc * inv).astype(BF16), v_loc[:, sl].astype(BF16))
        o += _dot((e_ctx * inv).astype(BF16), vc_ref[:, sl].astype(BF16))
        o_ref[:, sl] = o.astype(o_ref.dtype)


def _na_bias_kernel(rows_ref, onehot_ref, mask_ref, o_ref):
    o_ref[...] = _dot_exact_rhs(rows_ref[...], onehot_ref[...]) + mask_ref[...]


def _na_bias_table(rpb):
    nr, nd = NA_WIN_ROWS, NA_REL_COLS + 1
    rows = jnp.stack([rpb[:, nr - 1 - p:2 * nr - 1 - p, :] for p in range(nr)], axis=1)
    rows = jnp.pad(rows.astype(F32), ((0, 0), (0, 0), (0, 0), (0, nd - NA_REL_COLS)))
    qc = np.arange(GRID_W)[:, None]
    kc = np.arange(GRID_W)[None, :]
    rel_c = np.clip(kc - qc, -(NA_WIN_COLS - 1), NA_WIN_COLS - 1) + NA_WIN_COLS - 1
    onehot = (rel_c[None] == np.arange(nd)[:, None, None]).reshape(nd, GRID_W * GRID_W)
    win_c0 = np.clip(qc - NA_WIN_COLS // 2, 0, GRID_W - NA_WIN_COLS)
    valid = ((kc >= win_c0) & (kc < win_c0 + NA_WIN_COLS)).reshape(1, GRID_W * GRID_W)
    n_rows = NA_HEADS * nr * nr
    full = lambda shape: pl.BlockSpec(shape, lambda: (0,) * len(shape))
    bias = pl.pallas_call(
        _na_bias_kernel,
        in_specs=[full((n_rows, nd)), full((nd, GRID_W * GRID_W)), full((1, GRID_W * GRID_W))],
        out_specs=full((n_rows, GRID_W * GRID_W)),
        out_shape=jax.ShapeDtypeStruct((n_rows, GRID_W * GRID_W), F32),
        name="na_bias",
    )(rows.reshape(n_rows, nd), jnp.asarray(onehot, BF16), jnp.asarray(np.where(valid, 0.0, NEG_BIG), F32))
    bias = bias.reshape(NA_HEADS, nr, nr, GRID_W, GRID_W).transpose(0, 1, 3, 2, 4)
    return bias.reshape(NA_HEADS, nr, GRID_W, nr * GRID_W)


def _na_lat(p_att, cache_k, cache_v, layer, bias, oa):
    w = BRANCH_WIDTH
    blk0 = N_CTX // DEC_SEQ
    q0 = N_CTX // GRID_W
    rows_per = DEC_SEQ // GRID_W
    kv_cache = pl.BlockSpec((None, None, PAST_LEN, w), lambda n, r: (n, layer, 0, 0))
    return pl.pallas_call(
        _na_lat_kernel,
        grid=(DEC_BATCH, GRID_ROWS),
        in_specs=[
            pl.BlockSpec((GRID_W, w), lambda n, r: (q0 + n * rows_per + r, 0)),
            pl.BlockSpec((DEC_SEQ, w), lambda n, r: (blk0 + n, 1)),
            pl.BlockSpec((DEC_SEQ, w), lambda n, r: (blk0 + n, 2)),
            kv_cache, kv_cache,
            pl.BlockSpec((NA_HEADS, None, GRID_W, NA_WIN_ROWS * GRID_W),
                         lambda n, r: (0, r - _na_row_start(r), 0, 0)),
            _ANY_SPEC,
        ],
        out_specs=pl.BlockSpec((GRID_W, w), lambda n, r: (q0 + n * rows_per + r, 0)),
        out_shape=jax.ShapeDtypeStruct((N_TOK, w), BF16),
        input_output_aliases={6: 0},
        compiler_params=_cparams(("parallel", "arbitrary")),
        name="na_lat",
    )(p_att, p_att, p_att, cache_k, cache_v, bias, oa)


def _rope(x, cos, sin):
    lane = lax.broadcasted_iota(jnp.int32, x.shape, 1)
    first = (lane % 32) < 16
    rot = jnp.where(first, -pltpu.roll(x, 128 - 16, 1), pltpu.roll(x, 16, 1))
    return x * cos + rot * sin


def _diff_kernel(*refs, rope, cache, lam_init, tq):
    if cache:
        q_ref, k_ref, v_ref, kc_ref, vc_ref, cos_ref, sin_ref, lam_ref, g_ref, _, o_ref = refs
    else:
        q_ref, k_ref, v_ref, lam_ref, g_ref = refs[:5]
        o_ref, new_k_ref, new_v_ref = refs[-3:]
        new_k_ref[...] = k_ref[...]
        new_v_ref[...] = v_ref[...]
    scale = DIFF_QK_DIM ** -0.5
    lp = lam_ref[...]
    lam = (jnp.exp(jnp.sum(lp[0:1] * lp[1:2], axis=-1, keepdims=True))
           - jnp.exp(jnp.sum(lp[2:3] * lp[3:4], axis=-1, keepdims=True)) + lam_init)
    k = k_ref[...]
    v = v_ref[...]
    if rope:
        k = _rope(k, cos_ref[...], sin_ref[...])
    if cache:
        k = jnp.concatenate([k, kc_ref[...]], axis=0)
        v = jnp.concatenate([v, vc_ref[...]], axis=0)
    k = k.astype(BF16)
    v = v.astype(BF16)
    t = q_ref.shape[0]
    for i in range(t // tq):
        rows = slice(i * tq, (i + 1) * tq)
        q = q_ref[rows, :]
        if rope:
            q = _rope(q, cos_ref[rows, :], sin_ref[rows, :])
        lane = lax.broadcasted_iota(jnp.int32, q.shape, 1)
        q1 = jnp.where(lane < DIFF_QK_DIM, q, 0.0).astype(BF16)
        q2 = jnp.where(lane >= DIFF_QK_DIM, q, 0.0).astype(BF16)
        p1 = _softmax_rows(_dot(q1, k, NT) * scale)
        p2 = _softmax_rows(_dot(q2, k, NT) * scale)
        o = _dot((p1 - lam * p2).astype(BF16), v)
        o = o * lax.rsqrt(jnp.mean(o * o, axis=-1, keepdims=True) + LN_EPS)
        o_ref[rows, :] = (o * g_ref[...] * (1.0 - lam_init)).astype(o_ref.dtype)


def _diff_attn(p_att, lam_p, subln, lam_init, layer, *, t, n_seq, blk0, cache=None, tables=None, stacks=None, ob=None):
    hd = DIFF_V_DIM
    q_col, k_col, v_col = 3 * NA_HEADS, 4 * NA_HEADS, 5 * NA_HEADS
    in_specs = [
        pl.BlockSpec((t, hd), lambda n, h: (blk0 + n, q_col + h)),
        pl.BlockSpec((t, hd), lambda n, h: (blk0 + n, k_col + h)),
        pl.BlockSpec((t, hd), lambda n, h: (blk0 + n, v_col + h)),
    ]
    args = [p_att, p_att, p_att]
    stack_spec = pl.BlockSpec((None, None, t, hd), lambda n, h: (n, layer, 0, h))
    if cache is not None:
        cache_k, cache_v = cache
        tab = pl.BlockSpec((t, hd), lambda n, h: (0, 0))
        cache_spec = pl.BlockSpec((None, None, PAST_LEN, hd), lambda n, h: (n, layer, 0, h))
        in_specs += [cache_spec, cache_spec, tab, tab]
        args += [cache_k, cache_v, tables[0], tables[1]]
    in_specs += [
        pl.BlockSpec((4, DIFF_QK_DIM), lambda n, h: (0, 0)),
        pl.BlockSpec((1, hd), lambda n, h: (0, h)),
    ]
    args += [lam_p, subln.reshape(1, DIFF_HEADS * hd)]
    out_specs = [pl.BlockSpec((t, hd), lambda n, h: (blk0 + n, h))]
    out_shape = [jax.ShapeDtypeStruct((N_TOK, DIFF_HEADS * hd), BF16)]
    aliases = {}
    if cache is not None:
        aliases = {len(args): 0}
        in_specs.append(_ANY_SPEC)
        args.append(ob)
    else:
        out_specs += [stack_spec, stack_spec]
        out_shape += [_cache_stack_shape(), _cache_stack_shape()]
        if stacks is not None:
            aliases = {len(args): 1, len(args) + 1: 2}
            in_specs += [_ANY_SPEC, _ANY_SPEC]
            args += list(stacks)
    return pl.pallas_call(
        functools.partial(_diff_kernel, rope=cache is not None, cache=cache is not None,
                          lam_init=lam_init, tq=256),
        grid=(n_seq, DIFF_HEADS),
        in_specs=in_specs,
        out_specs=out_specs,
        out_shape=out_shape,
        input_output_aliases=aliases,
        compiler_params=_cparams(("parallel", "parallel")),
        name="diff_lat" if cache is not None else "diff_ctx",
    )(*args)


def _rope_tables():
    t = np.arange(DEC_SEQ)
    rows = (t // GRID_W).astype(np.float32)
    cols = (t % GRID_W).astype(np.float32)
    half = DIFF_QK_DIM // 2
    inv = jnp.asarray(ROPE_THETA, F32) ** (-jnp.arange(0, half, 2, dtype=F32) / half)
    ang_r = jnp.asarray(rows)[:, None] * inv
    ang_c = jnp.asarray(cols)[:, None] * inv
    ang = jnp.concatenate([ang_r, ang_r, ang_c, ang_c] * 2, axis=-1)
    return jnp.cos(ang), jnp.sin(ang)


def _head_sum(x, ones_bd):
    return _dot_exact_rhs(x, ones_bd)


def _rwkv_prep_kernel(cur_ref, prev_ref, next_ref, mix_ref, w0_ref, w2_ref, a0_ref, a2_ref, g2_ref,
                      kk_ref, ka_ref, rk_ref, ones_ref,
                      r_ref, v_ref, kkn_ref, lwf_ref, lwb_ref, kf_ref, kb_ref, bf_ref, bb_ref, bonus_ref, g_ref,
                      *, tm):
    i = pl.program_id(0)
    blocks_per_seq = DEC_SEQ // tm
    j = i - N_CTX // tm
    is_ctx = i < N_CTX // tm
    at_start = jnp.logical_or(is_ctx, j % blocks_per_seq == 0)
    at_end = jnp.logical_or(is_ctx, j % blocks_per_seq == blocks_per_seq - 1)
    cur = cur_ref[...]
    row = lax.broadcasted_iota(jnp.int32, cur.shape, 0)
    prev_row = jnp.where(at_start, 0.0, prev_ref[7:8, :])
    next_row = jnp.where(at_end, 0.0, next_ref[0:1, :])
    prev = jnp.where(row == 0, prev_row, pltpu.roll(cur, 1, 0))
    nxt = jnp.where(row == tm - 1, next_row, pltpu.roll(cur, tm - 1, 0))
    f = cur + mix_ref[0:1, :] * (prev - cur) + mix_ref[1:2, :] * (nxt - cur)

    w = RWKV_WIDTH
    r = f[:, 0:w]
    k = f[:, w:2 * w]
    v = f[:, 2 * w:3 * w]
    c0 = 3 * w
    wd = f[:, c0:c0 + 2 * RWKV_DECAY_RANK]
    ad = f[:, c0 + 2 * RWKV_DECAY_RANK:c0 + 2 * RWKV_DECAY_RANK + 2 * RWKV_ICL_RANK]
    gd = f[:, c0 + 2 * RWKV_DECAY_RANK + 2 * RWKV_ICL_RANK:]
    ones_bd = ones_ref[...]

    kkv = k * kk_ref[...]
    norm = jnp.sqrt(_head_sum(kkv * kkv, ones_bd))
    kkn = kkv / jnp.maximum(norm, 1e-12)
    wlin = _dot(jnp.tanh(wd).astype(BF16), w2_ref[...]) + w0_ref[...]
    alin = _dot(ad.astype(BF16), a2_ref[...]) + a0_ref[...]
    g = _dot(jax.nn.sigmoid(gd).astype(BF16), g2_ref[...])

    r_ref[...] = r
    v_ref[...] = v
    kkn_ref[...] = kkn
    g_ref[...] = g
    bonus = jnp.zeros_like(v)
    for d, (lw_ref, kd_ref, bd_ref) in enumerate(((lwf_ref, kf_ref, bf_ref), (lwb_ref, kb_ref, bb_ref))):
        z = -wlin[:, d * w:(d + 1) * w]
        softplus = jnp.maximum(z, 0.0) + jnp.log(1.0 + jnp.exp(-jnp.abs(z)))
        lw_ref[...] = -jnp.exp(-softplus - 0.5)
        a = jax.nn.sigmoid(alin[:, d * w:(d + 1) * w])
        k_d = k * (1.0 + (a - 1.0) * ka_ref[...])
        kd_ref[...] = k_d
        bd_ref[...] = kkn * a
        bonus += _head_sum(r * k_d * rk_ref[...], ones_bd) * v
    bonus_ref[...] = bonus


def _block_diag2(a, b):
    za = jnp.zeros((a.shape[0], b.shape[1]), a.dtype)
    zb = jnp.zeros((b.shape[0], a.shape[1]), b.dtype)
    return jnp.concatenate([jnp.concatenate([a, za], axis=1), jnp.concatenate([zb, b], axis=1)], axis=0)


def _rwkv_prep(p_rw, lp):
    tm = 256
    w = RWKV_WIDTH
    sub = tm // 8
    full = lambda shape: pl.BlockSpec(shape, lambda i: (0,) * len(shape))
    out_spec = pl.BlockSpec((tm, w), lambda i: (i, 0))
    ones_bd = jnp.asarray(np.kron(np.eye(RWKV_HEADS), np.ones((RWKV_HEAD_DIM, RWKV_HEAD_DIM))), BF16)
    w2_bd = _block_diag2(lp['rwkv_w2'][0], lp['rwkv_w2'][1]).astype(BF16)
    a2_bd = _block_diag2(lp['rwkv_a2'][0], lp['rwkv_a2'][1]).astype(BF16)
    n_out = 11
    return pl.pallas_call(
        functools.partial(_rwkv_prep_kernel, tm=tm),
        grid=(N_TOK // tm,),
        in_specs=[
            pl.BlockSpec((tm, RWKV_FEAT), lambda i: (i, 0)),
            pl.BlockSpec((8, RWKV_FEAT), lambda i: (jnp.maximum(i * sub - 1, 0), 0)),
            pl.BlockSpec((8, RWKV_FEAT), lambda i: (jnp.minimum((i + 1) * sub, N_TOK // 8 - 1), 0)),
            full((2, RWKV_FEAT)),
            full((1, 2 * w)), full((2 * RWKV_DECAY_RANK, 2 * w)),
            full((1, 2 * w)), full((2 * RWKV_ICL_RANK, 2 * w)),
            full((RWKV_GATE_RANK, w)),
            full((1, w)), full((1, w)), full((1, w)),
            full((w, w)),
        ],
        out_specs=[out_spec] * n_out,
        out_shape=[jax.ShapeDtypeStruct((N_TOK, w), F32)] * n_out,
        compiler_params=_cparams(("parallel",)),
        name="rwkv_prep",
    )(p_rw, p_rw, p_rw, lp['rwkv_mix'],
      lp['rwkv_w0'].reshape(1, 2 * w), w2_bd, lp['rwkv_a0'].reshape(1, 2 * w), a2_bd,
      lp['rwkv_g2'].astype(BF16),
      lp['rwkv_kk'].reshape(1, w), lp['rwkv_ka'].reshape(1, w), lp['rwkv_rk'].reshape(1, w),
      ones_bd)


def _rwkv_chunk_maps(chains):
    n = len(chains)
    rs, ks, vs, kkns, bs, lws, fwds = (list(z) for z in zip(*chains))
    c, nk = rs[0].shape
    row = lax.broadcasted_iota(jnp.int32, (c, c), 0)
    col = lax.broadcasted_iota(jnp.int32, (c, c), 1)
    eye = row == col
    incl_d = {True: row >= col, False: row <= col}
    strict_d = {True: row > col, False: row < col}
    tri_d = {f: jnp.where(m, 1.0, 0.0).astype(BF16) for f, m in incl_d.items()}
    idx = range(n)
    cum = [_dot_exact_lhs(tri_d[fwds[i]], lws[i]) for i in idx]
    tot = [cum[i][c - 1:c, :] if fwds[i] else cum[i][0:1, :] for i in idx]
    e_neg = [jnp.exp(-cum[i]) for i in idx]
    a_t = [-kkns[i] * jnp.exp(cum[i] - lws[i]) for i in idx]
    r_t = [rs[i] * jnp.exp(cum[i]) for i in idx]
    ar = [jnp.concatenate([a_t[i], r_t[i]], axis=0) for i in idx]
    with_b = [_dotp(ar[i], bs[i] * e_neg[i], NT, RWKV_PASSES_PAIR) for i in idx]
    with_k = [_dotp(ar[i], ks[i] * e_neg[i], NT, RWKV_PASSES_PAIR) for i in idx]
    l_ab = [jnp.where(strict_d[fwds[i]], with_b[i][:c], 0.0) for i in idx]
    l_ak = [jnp.where(strict_d[fwds[i]], with_k[i][:c], 0.0) for i in idx]
    m_rb = [jnp.where(incl_d[fwds[i]], with_b[i][c:], 0.0) for i in idx]
    m_rk = [jnp.where(incl_d[fwds[i]], with_k[i][c:], 0.0) for i in idx]
    lakv = [_dotp(l_ak[i], vs[i], NN, RWKV_PASSES_APPLY) for i in idx]
    mrkv = [_dotp(m_rk[i], vs[i], NN, RWKV_PASSES_OUT) for i in idx]
    e_tail = [jnp.exp(tot[i] - cum[i]) for i in idx]
    kwv = [_dotp(ks[i] * e_tail[i], vs[i], TN, RWKV_PASSES_APPLY) for i in idx]
    same = lambda s: (row // s) == (col // s)
    inv = [jnp.where(eye, 1.0, 0.0) + jnp.where(same(2), l_ab[i], 0.0) for i in idx]
    size = 4
    while size <= c:
        part = jnp.logical_and(same(size), jnp.logical_not(same(size // 2)))
        half = [_dotp(inv[i], jnp.where(part, l_ab[i], 0.0), NN, RWKV_PASSES_SOLVE) for i in idx]
        inv = [inv[i] + _dotp(half[i], inv[i], NN, RWKV_PASSES_SOLVE) for i in idx]
        size *= 2
    pq1 = [_dotp(inv[i], jnp.concatenate([a_t[i], lakv[i]], axis=1), NN, RWKV_PASSES_APPLY) for i in idx]
    rb = [_dotp(m_rb[i], pq1[i], NN, RWKV_PASSES_OUT) for i in idx]
    sb = [_dotp(bs[i] * e_tail[i], pq1[i], TN, RWKV_PASSES_APPLY) for i in idx]
    out = []
    for i in idx:
        p2 = r_t[i] + rb[i][:, :nk]
        q2 = rb[i][:, nk:] + mrkv[i]
        p3 = jnp.where(eye, jnp.exp(tot[i]), 0.0) + sb[i][:, :nk]
        q3 = sb[i][:, nk:] + kwv[i]
        out.append((jnp.concatenate([p2, p3], axis=0), q2, q3))
    return out


def _rwkv_scan_kernel(rf, vf, af, wf, kf, bf, rb, vb, ab, wb, kb, bb, s0f, s0b, *rest, n_super, n_chunk):
    yf, yb, sff, sfb, st_ref = rest[-5:]
    sup = pl.program_id(2)
    hd = RWKV_HEAD_DIM
    cs = RWKV_CHUNK

    @pl.when(sup == 0)
    def _():
        st_ref[0:2] = s0f[...]
        st_ref[2:4] = s0b[...]

    dirs = ((rf, vf, af, wf, kf, bf, yf, True), (rb, vb, ab, wb, kb, bb, yb, False))
    keys, chains = [], []
    for d, (r_ref, v_ref, a_ref, w_ref, k_ref, b_ref, _, forward) in enumerate(dirs):
        for ci in range(n_chunk):
            rows = slice(ci * cs, (ci + 1) * cs)
            for hh in range(2):
                sl = slice(hh * hd, (hh + 1) * hd)
                keys.append((d, ci, hh))
                chains.append((r_ref[rows, sl], k_ref[rows, sl], v_ref[rows, sl],
                               a_ref[rows, sl], b_ref[rows, sl], w_ref[rows, sl], forward))
    maps = dict(zip(keys, _rwkv_chunk_maps(chains)))

    seqs = [(d, hh) for d in range(2) for hh in range(2)]
    st = {s: st_ref[2 * s[0] + s[1]] for s in seqs}
    ys = {}
    for step in range(n_chunk):
        for d, hh in seqs:
            ci = step if dirs[d][-1] else n_chunk - 1 - step
            p23, q2, q3 = maps[d, ci, hh]
            res = _dotp(p23, st[d, hh], NN, RWKV_PASSES_STATE)
            ys[d, ci, hh] = res[:cs] + q2
            st[d, hh] = res[cs:] + q3
    for d, hh in seqs:
        st_ref[2 * d + hh] = st[d, hh]
    for d in range(2):
        y_ref = dirs[d][6]
        for ci in range(n_chunk):
            y_ref[ci * cs:(ci + 1) * cs, :] = jnp.concatenate([ys[d, ci, 0], ys[d, ci, 1]], axis=1)

    @pl.when(sup == n_super - 1)
    def _():
        sff[...] = st_ref[0:2]
        sfb[...] = st_ref[2:4]


def _rwkv_scan(feats, s0f_t, s0b_t, *, t, n_seq, row0, y_prev=None):
    r, v, kkn, lwf, lwb, kf, kb, bf, bb = feats
    rows = RWKV_SUPER
    n_super = t // rows
    blk0 = row0 // rows
    fwd = pl.BlockSpec((rows, 128), lambda n, p, s: (blk0 + n * n_super + s, p))
    bwd = pl.BlockSpec((rows, 128), lambda n, p, s: (blk0 + n * n_super + n_super - 1 - s, p))
    st_spec = pl.BlockSpec((None, 2, RWKV_HEAD_DIM, RWKV_HEAD_DIM), lambda n, p, s: (n, p, 0, 0))
    st_shape = jax.ShapeDtypeStruct((n_seq, RWKV_HEADS, RWKV_HEAD_DIM, RWKV_HEAD_DIM), F32)
    y_shape = jax.ShapeDtypeStruct((N_TOK, RWKV_WIDTH), F32)
    extra = [] if y_prev is None else list(y_prev)
    n_in = 14
    return pl.pallas_call(
        functools.partial(_rwkv_scan_kernel, n_super=n_super, n_chunk=rows // RWKV_CHUNK),
        grid=(n_seq, RWKV_HEADS // 2, n_super),
        in_specs=[fwd] * 6 + [bwd] * 6 + [st_spec, st_spec] + [_ANY_SPEC] * len(extra),
        out_specs=[fwd, bwd, st_spec, st_spec],
        out_shape=[y_shape, y_shape, st_shape, st_shape],
        input_output_aliases={n_in: 0, n_in + 1: 1} if extra else {},
        scratch_shapes=[pltpu.VMEM((4, RWKV_HEAD_DIM, RWKV_HEAD_DIM), F32)],
        compiler_params=_cparams(("parallel", "parallel", "arbitrary")),
        name="rwkv_scan",
    )(r, v, kkn, lwf, kf, bf, r, v, kkn, lwb, kb, bb, s0f_t, s0b_t, *extra)


def _rwkv_mixer(p_rw, s_lat_f, s_lat_b, lp):
    r, v, kkn, lwf, lwb, kf, kb, bf, bb, bonus, g = _rwkv_prep(p_rw, lp)
    feats = (r, v, kkn, lwf, lwb, kf, kb, bf, bb)
    s_zero = jnp.zeros((BATCH, RWKV_HEADS, RWKV_HEAD_DIM, RWKV_HEAD_DIM), F32)
    y_f, y_b, sf_c, sb_c = _rwkv_scan(feats, s_zero, s_zero, t=SEQ, n_seq=BATCH, row0=0)
    y_f, y_b, _, _ = _rwkv_scan(feats, jnp.swapaxes(s_lat_f, -1, -2), jnp.swapaxes(s_lat_b, -1, -2),
                                t=DEC_SEQ, n_seq=DEC_BATCH, row0=N_CTX, y_prev=(y_f, y_b))
    oc = _rwkv_out(y_f, y_b, bonus, g, lp['rwkv_lnx_g'], lp['rwkv_lnx_b'])
    return oc, jnp.swapaxes(sf_c, -1, -2), jnp.swapaxes(sb_c, -1, -2)


def _rwkv_out_kernel(yf_ref, yb_ref, bonus_ref, g_ref, lg_ref, lb_ref, ones_ref, o_ref):
    ones_bd = ones_ref[...]
    y = yf_ref[...] + yb_ref[...]
    mu = _head_sum(y, ones_bd) * (1.0 / RWKV_HEAD_DIM)
    yc = y - mu
    var = _head_sum(yc * yc, ones_bd) * (1.0 / RWKV_HEAD_DIM)
    yn = yc * lax.rsqrt(var + RWKV_GN_EPS) * lg_ref[...] + lb_ref[...]
    o_ref[...] = ((yn + bonus_ref[...]) * g_ref[...]).astype(o_ref.dtype)


def _rwkv_out(y_f, y_b, bonus, g, lnx_g, lnx_b):
    tm = 512
    w = RWKV_WIDTH
    row = pl.BlockSpec((tm, w), lambda i: (i, 0))
    vec = pl.BlockSpec((1, w), lambda i: (0, 0))
    ones_bd = jnp.asarray(np.kron(np.eye(RWKV_HEADS), np.ones((RWKV_HEAD_DIM, RWKV_HEAD_DIM))), BF16)
    return pl.pallas_call(
        _rwkv_out_kernel,
        grid=(N_TOK // tm,),
        in_specs=[row, row, row, row, vec, vec, pl.BlockSpec((w, w), lambda i: (0, 0))],
        out_specs=row,
        out_shape=jax.ShapeDtypeStruct((N_TOK, w), BF16),
        compiler_params=_cparams(("parallel",)),
        name="rwkv_out",
    )(y_f, y_b, bonus, g, lnx_g.reshape(1, w), lnx_b.reshape(1, w), ones_bd)


def kernel(x_prompt, x_sample, cache_na_k, cache_na_v, cache_diff_k, cache_diff_v, state_rwkv_fwd, state_rwkv_bwd, c, c_ctx, w_ada, b_ada, w_in, na_rpb, diff_lambda, diff_subln, rwkv_mix, rwkv_w0, rwkv_w2, rwkv_a0, rwkv_a2, rwkv_g2, rwkv_kk, rwkv_ka, rwkv_rk, rwkv_lnx_g, rwkv_lnx_b, w_branch, w_out, ln1_g, ln1_b, w_ffn_in, w_ffn_out, ln2_g, ln2_b):
    x = jnp.concatenate([x_prompt.reshape(N_CTX, D_MODEL), x_sample.reshape(N_LAT, D_MODEL)], axis=0)
    cond = jnp.concatenate([c_ctx[None, :], c, jnp.zeros((N_COND - 1 - DEC_BATCH, D_MODEL), F32)], axis=0)
    mods = _adaln_all(cond, w_ada, b_ada)
    mods = mods.reshape(DEPTH, N_COND, 6, D_MODEL).transpose(0, 2, 1, 3).reshape(DEPTH * 6 * N_COND, 1, D_MODEL)

    cache_na_k = cache_na_k.reshape(DEC_BATCH, DEPTH, PAST_LEN, BRANCH_WIDTH)
    cache_na_v = cache_na_v.reshape(DEC_BATCH, DEPTH, PAST_LEN, BRANCH_WIDTH)
    cache_diff_k = cache_diff_k.reshape(DEC_BATCH, DEPTH, PAST_LEN, BRANCH_WIDTH)
    cache_diff_v = cache_diff_v.reshape(DEC_BATCH, DEPTH, PAST_LEN, BRANCH_WIDTH)
    rope_tables = _rope_tables()

    c1 = ATT_WIDTH + RWKV_FEAT
    rw_pad = 128
    w_in16 = jnp.concatenate([w_in[:, :, :c1], jnp.zeros((DEPTH, D_MODEL, rw_pad), F32), w_in[:, :, c1:]],
                             axis=-1).astype(BF16)
    w_branch16 = w_branch.astype(BF16)
    w_out16 = w_out.astype(BF16)
    w_ffn_out16 = w_ffn_out.astype(BF16)

    h = _modulate(x, mods, 0)
    na_stacks, diff_stacks, new_f, new_b = None, None, [], []
    for l in range(DEPTH):
        lam_init = 0.8 - 0.6 * math.exp(-0.3 * l)
        lp = {'rwkv_mix': rwkv_mix[l], 'rwkv_w0': rwkv_w0[l], 'rwkv_w2': rwkv_w2[l], 'rwkv_a0': rwkv_a0[l],
              'rwkv_a2': rwkv_a2[l], 'rwkv_g2': rwkv_g2[l], 'rwkv_kk': rwkv_kk[l], 'rwkv_ka': rwkv_ka[l],
              'rwkv_rk': rwkv_rk[l], 'rwkv_lnx_g': rwkv_lnx_g[l], 'rwkv_lnx_b': rwkv_lnx_b[l]}
        p_att = _matmul(h, w_in16, l, F32, col0=0, n=ATT_WIDTH, name="in_att")
        p_rw = _matmul(h, w_in16, l, F32, col0=ATT_WIDTH, n=RWKV_FEAT + rw_pad, name="in_rwkv")
        gates = _matmul(h, w_in16, l, BF16, col0=c1 + rw_pad, n=GATE_WIDTH, name="in_gate", sigmoid=True)

        oa, *na_stacks = _na_ctx(p_att, l, na_stacks)
        oa = _na_lat(p_att, cache_na_k, cache_na_v, l, _na_bias_table(na_rpb[l]), oa)
        ob, *diff_stacks = _diff_attn(p_att, diff_lambda[l], diff_subln[l], lam_init, l, t=SEQ, n_seq=BATCH, blk0=0,
                                      stacks=diff_stacks)
        ob, = _diff_attn(p_att, diff_lambda[l], diff_subln[l], lam_init, l, t=DEC_SEQ, n_seq=DEC_BATCH,
                         blk0=N_CTX // DEC_SEQ, cache=(cache_diff_k, cache_diff_v), tables=rope_tables, ob=ob)
        oc, sf_c, sb_c = _rwkv_mixer(p_rw, state_rwkv_fwd[:, l], state_rwkv_bwd[:, l], lp)

        merged = _merge(oa, ob, oc, gates, w_branch16, l)
        mixed = _matmul(merged, w_out16, l, F32, name="w_out")
        x, h2 = _res_ln(x, mixed, mods, l, 2, ln1_g[l], ln1_b[l], l, 3)
        hid = _ffn_in(h2, w_ffn_in, l)
        ff = _matmul(hid, w_ffn_out16, l, F32, tn=512, name="ffn_out", vmem_mb=56)
        nxt = l + 1 if l + 1 < DEPTH else None
        x, h = _res_ln(x, ff, mods, l, 5, ln2_g[l], ln2_b[l], nxt, 0)
        new_f.append(sf_c)
        new_b.append(sb_c)

    y_prompt = x[:N_CTX].reshape(BATCH, SEQ, D_MODEL)
    y_sample = x[N_CTX:].reshape(DEC_BATCH, DEC_SEQ, D_MODEL)
    new_na_k, new_na_v = (z.reshape(BATCH, DEPTH, SEQ, NA_HEADS, NA_HEAD_DIM) for z in na_stacks)
    new_diff_k = diff_stacks[0].reshape(BATCH, DEPTH, SEQ, DIFF_HEADS, 2, DIFF_QK_DIM)
    new_diff_v = diff_stacks[1].reshape(BATCH, DEPTH, SEQ, DIFF_HEADS, DIFF_V_DIM)
    return (y_prompt, y_sample, new_na_k, new_na_v, new_diff_k, new_diff_v,
            jnp.stack(new_f, axis=1), jnp.stack(new_b, axis=1))
```

```python
import functools
import math

import numpy as np
import jax
import jax.numpy as jnp
from jax import lax
from jax.experimental import pallas as pl
from jax.experimental.pallas import tpu as pltpu

F32 = jnp.float32
BF16 = jnp.bfloat16

D_MODEL = 2048
BATCH = 16
SEQ = 256
DEPTH = 4
DEC_BATCH = 4
DEC_SEQ = 1024
PAST_LEN = 256
GRID_W = 64
GRID_ROWS = DEC_SEQ // GRID_W
BRANCH_WIDTH = 512
N_BRANCH = 3
NA_HEADS = 4
NA_HEAD_DIM = 128
NA_WIN_ROWS = 8
NA_WIN_COLS = 16
NA_REL_ROWS = 2 * NA_WIN_ROWS - 1
NA_REL_COLS = 2 * NA_WIN_COLS - 1
DIFF_HEADS = 4
DIFF_QK_DIM = 64
DIFF_V_DIM = 128
RWKV_HEADS = 8
RWKV_HEAD_DIM = 64
RWKV_WIDTH = RWKV_HEADS * RWKV_HEAD_DIM
RWKV_DECAY_RANK = 64
RWKV_ICL_RANK = 64
RWKV_GATE_RANK = 128
RWKV_FEAT = 3 * RWKV_WIDTH + 2 * RWKV_DECAY_RANK + 2 * RWKV_ICL_RANK + RWKV_GATE_RANK
RWKV_GN_EPS = 64e-5
ATT_WIDTH = 6 * BRANCH_WIDTH
GATE_WIDTH = N_BRANCH * D_MODEL
FFN_HIDDEN = -(-8 * D_MODEL // (3 * 256)) * 256
ROPE_THETA = 10000.0
LN_EPS = 1e-5
ALPHA = (2.0 * DEPTH) ** 0.25

N_CTX = BATCH * SEQ
N_LAT = DEC_BATCH * DEC_SEQ
N_TOK = N_CTX + N_LAT
N_COND = 8
RWKV_CHUNK = 64
RWKV_SUPER = 256
RWKV_PASSES_PAIR = 1
RWKV_PASSES_SOLVE = 1
RWKV_PASSES_APPLY = 3
RWKV_PASSES_OUT = 1
RWKV_PASSES_STATE = 3
NEG_BIG = -1e30

NN = ((1,), (0,))
NT = ((1,), (1,))
TN = ((0,), (0,))


def _cparams(sem, vmem_mb=48):
    return pltpu.CompilerParams(dimension_semantics=sem, vmem_limit_bytes=vmem_mb * 1024 * 1024)


def _dot(a, b, dims=NN):
    return lax.dot_general(a, b, (dims, ((), ())), preferred_element_type=F32)


def _split2(x):
    hi = x.astype(BF16)
    lo = (x - hi.astype(F32)).astype(BF16)
    return hi, lo


def _dot3(a, b, dims=NN):
    ah, al = _split2(a)
    bh, bl = _split2(b)
    return _dot(ah, bh, dims) + (_dot(ah, bl, dims) + _dot(al, bh, dims))


def _dotp(a, b, dims, passes):
    if passes == 1:
        return _dot(a.astype(BF16), b.astype(BF16), dims)
    assert passes == 3
    return _dot3(a, b, dims)


def _dot_exact_lhs(a_bf16, b, dims=NN):
    b1 = b.astype(BF16)
    r1 = b - b1.astype(F32)
    b2 = r1.astype(BF16)
    b3 = (r1 - b2.astype(F32)).astype(BF16)
    return _dot(a_bf16, b1, dims) + (_dot(a_bf16, b2, dims) + _dot(a_bf16, b3, dims))


def _dot_exact_rhs(a, b_bf16, dims=NN):
    a1 = a.astype(BF16)
    r1 = a - a1.astype(F32)
    a2 = r1.astype(BF16)
    a3 = (r1 - a2.astype(F32)).astype(BF16)
    return _dot(a1, b_bf16, dims) + (_dot(a2, b_bf16, dims) + _dot(a3, b_bf16, dims))


def _normalize(x):
    mu = jnp.mean(x, axis=-1, keepdims=True)
    xc = x - mu
    var = jnp.mean(xc * xc, axis=-1, keepdims=True)
    return xc * lax.rsqrt(var + LN_EPS)


def _cond_of_row(row):
    return jnp.where(row < N_CTX, 0, 1 + (row - N_CTX) // DEC_SEQ)


def _adaln_kernel(c_ref, w_ref, b_ref, o_ref):
    c = c_ref[...]
    s = (c * jax.nn.sigmoid(c)).astype(BF16)
    o_ref[...] = _dot(s, w_ref[...].astype(BF16)) + b_ref[...]


def _adaln_all(cond, w_ada, b_ada):
    tn = 1024
    n = 6 * D_MODEL
    return pl.pallas_call(
        _adaln_kernel,
        grid=(DEPTH, n // tn),
        in_specs=[
            pl.BlockSpec((N_COND, D_MODEL), lambda l, j: (0, 0)),
            pl.BlockSpec((None, D_MODEL, tn), lambda l, j: (l, 0, j)),
            pl.BlockSpec((None, 1, tn), lambda l, j: (l, 0, j)),
        ],
        out_specs=pl.BlockSpec((None, N_COND, tn), lambda l, j: (l, 0, j)),
        out_shape=jax.ShapeDtypeStruct((DEPTH, N_COND, n), F32),
        compiler_params=_cparams(("parallel", "parallel")),
        name="adaln",
    )(cond, w_ada, b_ada.reshape(DEPTH, 1, n))


def _mod_spec(layer, which, tm, blk0=0):
    base = layer * 6 * N_COND + which * N_COND
    return pl.BlockSpec((None, 1, D_MODEL), lambda i: (base + _cond_of_row((blk0 + i) * tm), 0, 0))


def _modulate_kernel(x_ref, sh_ref, sc_ref, h_ref):
    h_ref[...] = (_normalize(x_ref[...]) * (1.0 + sc_ref[...]) + sh_ref[...]).astype(h_ref.dtype)


def _modulate(x, mods, layer):
    tm = 256
    return pl.pallas_call(
        _modulate_kernel,
        grid=(N_TOK // tm,),
        in_specs=[
            pl.BlockSpec((tm, D_MODEL), lambda i: (i, 0)),
            _mod_spec(layer, 0, tm),
            _mod_spec(layer, 1, tm),
        ],
        out_specs=pl.BlockSpec((tm, D_MODEL), lambda i: (i, 0)),
        out_shape=jax.ShapeDtypeStruct((N_TOK, D_MODEL), BF16),
        compiler_params=_cparams(("parallel",)),
        name="modulate",
    )(x, mods, mods)


def _res_ln_kernel(x_ref, z_ref, gate_ref, g_ref, b_ref, *rest, with_mod):
    y = ALPHA * x_ref[...] + gate_ref[...] * z_ref[...]
    xn = _normalize(y) * g_ref[...] + b_ref[...]
    if with_mod:
        sh_ref, sc_ref, xo_ref, h_ref = rest
        xo_ref[...] = xn
        h_ref[...] = (_normalize(xn) * (1.0 + sc_ref[...]) + sh_ref[...]).astype(h_ref.dtype)
    else:
        (xo_ref,) = rest
        xo_ref[...] = xn


def _res_ln(x, z, mods, layer, gate_idx, ln_g, ln_b, mod_layer, mod_idx, rows=(0, N_TOK)):
    tm = 256
    with_mod = mod_layer is not None
    blk0, n_rows = rows[0] // tm, rows[1]
    row_in = pl.BlockSpec((tm, D_MODEL), lambda i: (blk0 + i, 0))
    row = pl.BlockSpec((tm, D_MODEL), lambda i: (i, 0))
    vec = pl.BlockSpec((1, D_MODEL), lambda i: (0, 0))
    in_specs = [row_in, row_in, _mod_spec(layer, gate_idx, tm, blk0), vec, vec]
    args = [x, z, mods, ln_g.reshape(1, D_MODEL), ln_b.reshape(1, D_MODEL)]
    out_specs = [row]
    out_shape = [jax.ShapeDtypeStruct((n_rows, D_MODEL), F32)]
    if with_mod:
        in_specs += [_mod_spec(mod_layer, mod_idx, tm, blk0), _mod_spec(mod_layer, mod_idx + 1, tm, blk0)]
        args += [mods, mods]
        out_specs.append(row)
        out_shape.append(jax.ShapeDtypeStruct((n_rows, D_MODEL), BF16))
    out = pl.pallas_call(
        functools.partial(_res_ln_kernel, with_mod=with_mod),
        grid=(n_rows // tm,),
        in_specs=in_specs,
        out_specs=out_specs,
        out_shape=out_shape,
        compiler_params=_cparams(("parallel",)),
        name="res_ln",
    )(*args)
    return (out[0], out[1]) if with_mod else (out[0], None)


def _mm_kernel(a_ref, b_ref, o_ref, *, sigmoid):
    acc = _dot(a_ref[...], b_ref[...])
    if sigmoid:
        acc = jax.nn.sigmoid(acc)
    o_ref[...] = acc.astype(o_ref.dtype)


def _matmul(a, w, layer, out_dtype, col0=0, n=None, tm=1024, tn=1024, name="matmul", sigmoid=False, vmem_mb=48):
    m, k = a.shape
    n = w.shape[2] if n is None else n
    assert m % tm == 0 and n % tn == 0 and col0 % tn == 0
    j0 = col0 // tn
    return pl.pallas_call(
        functools.partial(_mm_kernel, sigmoid=sigmoid),
        grid=(m // tm, n // tn),
        in_specs=[
            pl.BlockSpec((tm, k), lambda i, j: (i, 0)),
            pl.BlockSpec((None, k, tn), lambda i, j: (layer, 0, j0 + j)),
        ],
        out_specs=pl.BlockSpec((tm, tn), lambda i, j: (i, j)),
        out_shape=jax.ShapeDtypeStruct((m, n), out_dtype),
        compiler_params=_cparams(("parallel", "parallel"), vmem_mb),
        name=name,
    )(a, w)


def _swiglu_kernel(a_ref, wg_ref, wu_ref, o_ref, wg16_ref, wu16_ref):
    @pl.when(pl.program_id(1) == 0)
    def _():
        wg16_ref[...] = wg_ref[...].astype(BF16)
        wu16_ref[...] = wu_ref[...].astype(BF16)

    a = a_ref[...]
    g = _dot(a, wg16_ref[...])
    u = _dot(a, wu16_ref[...])
    o_ref[...] = (g * jax.nn.sigmoid(g) * u).astype(o_ref.dtype)


def _ffn_in(h, w_ffn_in, layer):
    tm, tn = 1024, 512
    nb = FFN_HIDDEN // tn
    return pl.pallas_call(
        _swiglu_kernel,
        grid=(nb, N_TOK // tm),
        in_specs=[
            pl.BlockSpec((tm, D_MODEL), lambda j, i: (i, 0)),
            pl.BlockSpec((None, D_MODEL, tn), lambda j, i: (layer, 0, j)),
            pl.BlockSpec((None, D_MODEL, tn), lambda j, i: (layer, 0, j + nb)),
        ],
        out_specs=pl.BlockSpec((tm, tn), lambda j, i: (i, j)),
        out_shape=jax.ShapeDtypeStruct((N_TOK, FFN_HIDDEN), BF16),
        scratch_shapes=[pltpu.VMEM((D_MODEL, tn), BF16), pltpu.VMEM((D_MODEL, tn), BF16)],
        compiler_params=_cparams(("parallel", "arbitrary")),
        name="ffn_in",
    )(h, w_ffn_in, w_ffn_in)


def _merge_kernel(oa_ref, ob_ref, oc_ref, ga_ref, gb_ref, gc_ref, wb_ref, o_ref):
    acc = ga_ref[...].astype(F32) * _dot(oa_ref[...], wb_ref[0])
    acc += gb_ref[...].astype(F32) * _dot(ob_ref[...], wb_ref[1])
    acc += gc_ref[...].astype(F32) * _dot(oc_ref[...], wb_ref[2])
    o_ref[...] = acc.astype(o_ref.dtype)


def _merge(oa, ob, oc, gate_pre, w_branch, layer):
    tm, tn = 1024, 512
    nb = D_MODEL // tn
    o_spec = pl.BlockSpec((tm, BRANCH_WIDTH), lambda i, j: (i, 0))
    return pl.pallas_call(
        _merge_kernel,
        grid=(N_TOK // tm, nb),
        in_specs=[
            o_spec, o_spec, o_spec,
            pl.BlockSpec((tm, tn), lambda i, j: (i, j)),
            pl.BlockSpec((tm, tn), lambda i, j: (i, j + nb)),
            pl.BlockSpec((tm, tn), lambda i, j: (i, j + 2 * nb)),
            pl.BlockSpec((None, N_BRANCH, BRANCH_WIDTH, tn), lambda i, j: (layer, 0, 0, j)),
        ],
        out_specs=pl.BlockSpec((tm, tn), lambda i, j: (i, j)),
        out_shape=jax.ShapeDtypeStruct((N_TOK, D_MODEL), BF16),
        compiler_params=_cparams(("parallel", "parallel")),
        name="merge",
    )(oa, ob, oc, gate_pre, gate_pre, gate_pre, w_branch)


def _softmax_rows(s):
    m = jnp.max(s, axis=-1, keepdims=True)
    e = jnp.exp(s - m)
    return e * (1.0 / jnp.sum(e, axis=-1, keepdims=True))


def _na_ctx_kernel(q_ref, k_ref, v_ref, *rest):
    o_ref, new_k_ref, new_v_ref = rest[-3:]
    scale = NA_HEAD_DIM ** -0.5
    new_k_ref[...] = k_ref[...]
    new_v_ref[...] = v_ref[...]
    heads = range(NA_HEADS)
    sls = [slice(h * NA_HEAD_DIM, (h + 1) * NA_HEAD_DIM) for h in heads]
    s = [_dot(q_ref[:, sl].astype(BF16), k_ref[:, sl].astype(BF16), NT) * scale for sl in sls]
    p = [_softmax_rows(s[h]).astype(BF16) for h in heads]
    o = [_dot(p[h], v_ref[:, sls[h]].astype(BF16)) for h in heads]
    for h in heads:
        o_ref[:, sls[h]] = o[h].astype(o_ref.dtype)


_ANY_SPEC = pl.BlockSpec(memory_space=pl.ANY)


def _cache_stack_shape():
    return jax.ShapeDtypeStruct((BATCH, DEPTH, SEQ, BRANCH_WIDTH), F32)


def _na_ctx(p_att, layer, stacks):
    w = BRANCH_WIDTH
    stack_spec = pl.BlockSpec((None, None, SEQ, w), lambda n: (n, layer, 0, 0))
    extra = [] if stacks is None else list(stacks)
    return pl.pallas_call(
        _na_ctx_kernel,
        grid=(BATCH,),
        in_specs=[
            pl.BlockSpec((SEQ, w), lambda n: (n, 0)),
            pl.BlockSpec((SEQ, w), lambda n: (n, 1)),
            pl.BlockSpec((SEQ, w), lambda n: (n, 2)),
        ] + [_ANY_SPEC] * len(extra),
        out_specs=[pl.BlockSpec((SEQ, w), lambda n: (n, 0)), stack_spec, stack_spec],
        out_shape=[jax.ShapeDtypeStruct((N_TOK, w), BF16), _cache_stack_shape(), _cache_stack_shape()],
        input_output_aliases={3: 1, 4: 2} if extra else {},
        compiler_params=_cparams(("parallel",)),
        name="na_ctx",
    )(p_att, p_att, p_att, *extra)


def _na_row_start(r):
    return jnp.clip(r - NA_WIN_ROWS // 2, 0, GRID_ROWS - NA_WIN_ROWS)


def _na_lat_kernel(q_ref, k_ref, v_ref, kc_ref, vc_ref, bias_ref, _, o_ref):
    scale = NA_HEAD_DIM ** -0.5
    n_loc = NA_WIN_ROWS * GRID_W
    start = pl.multiple_of(_na_row_start(pl.program_id(1)) * GRID_W, GRID_W)
    k_loc = k_ref[pl.ds(start, n_loc), :].astype(BF16)
    v_loc = v_ref[pl.ds(start, n_loc), :].astype(BF16)
    heads = range(NA_HEADS)
    sls = [slice(h * NA_HEAD_DIM, (h + 1) * NA_HEAD_DIM) for h in heads]
    q = [q_ref[:, sl].astype(BF16) for sl in sls]
    s_loc = [_dot(q[h], k_loc[:, sls[h]], NT) * scale + bias_ref[h] for h in heads]
    s_ctx = [_dot(q[h], kc_ref[:, sls[h]].astype(BF16), NT) * scale for h in heads]
    m = [jnp.maximum(jnp.max(s_loc[h], axis=-1, keepdims=True), jnp.max(s_ctx[h], axis=-1, keepdims=True))
         for h in heads]
    e_loc = [jnp.exp(s_loc[h] - m[h]) for h in heads]
    e_ctx = [jnp.exp(s_ctx[h] - m[h]) for h in heads]
    inv = [1.0 / (jnp.sum(e_loc[h], axis=-1, keepdims=True) + jnp.sum(e_ctx[h], axis=-1, keepdims=True))
           for h in heads]
    o_loc = [_dot((e_loc[h] * inv[h]).astype(BF16), v_loc[:, sls[h]]) for h in heads]
    o_ctx = [_dot((e_ctx[h] * inv[h]).astype(BF16), vc_ref[:, sls[h]].astype(BF16)) for h in heads]
    for h in heads:
        o_ref[:, sls[h]] = (o_loc[h] + o_ctx[h]).astype(o_ref.dtype)


def _na_bias_kernel(rows_ref, onehot_ref, mask_ref, o_ref):
    o_ref[...] = _dot_exact_rhs(rows_ref[...], onehot_ref[...]) + mask_ref[...]


def _na_bias_table(rpb):
    nr, nd = NA_WIN_ROWS, NA_REL_COLS + 1
    rows = jnp.stack([rpb[:, nr - 1 - p:2 * nr - 1 - p, :] for p in range(nr)], axis=1)
    rows = jnp.pad(rows.astype(F32), ((0, 0), (0, 0), (0, 0), (0, nd - NA_REL_COLS)))
    qc = np.arange(GRID_W)[:, None]
    kc = np.arange(GRID_W)[None, :]
    rel_c = np.clip(kc - qc, -(NA_WIN_COLS - 1), NA_WIN_COLS - 1) + NA_WIN_COLS - 1
    onehot = (rel_c[None] == np.arange(nd)[:, None, None]).reshape(nd, GRID_W * GRID_W)
    win_c0 = np.clip(qc - NA_WIN_COLS // 2, 0, GRID_W - NA_WIN_COLS)
    valid = ((kc >= win_c0) & (kc < win_c0 + NA_WIN_COLS)).reshape(1, GRID_W * GRID_W)
    n_rows = NA_HEADS * nr * nr
    full = lambda shape: pl.BlockSpec(shape, lambda: (0,) * len(shape))
    bias = pl.pallas_call(
        _na_bias_kernel,
        in_specs=[full((n_rows, nd)), full((nd, GRID_W * GRID_W)), full((1, GRID_W * GRID_W))],
        out_specs=full((n_rows, GRID_W * GRID_W)),
        out_shape=jax.ShapeDtypeStruct((n_rows, GRID_W * GRID_W), F32),
        name="na_bias",
    )(rows.reshape(n_rows, nd), jnp.asarray(onehot, BF16), jnp.asarray(np.where(valid, 0.0, NEG_BIG), F32))
    bias = bias.reshape(NA_HEADS, nr, nr, GRID_W, GRID_W).transpose(0, 1, 3, 2, 4)
    return bias.reshape(NA_HEADS, nr, GRID_W, nr * GRID_W)


def _na_lat(p_att, cache_k, cache_v, layer, bias, oa):
    w = BRANCH_WIDTH
    blk0 = N_CTX // DEC_SEQ
    q0 = N_CTX // GRID_W
    rows_per = DEC_SEQ // GRID_W
    kv_cache = pl.BlockSpec((None, None, PAST_LEN, w), lambda n, r: (n, layer, 0, 0))
    return pl.pallas_call(
        _na_lat_kernel,
        grid=(DEC_BATCH, GRID_ROWS),
        in_specs=[
            pl.BlockSpec((GRID_W, w), lambda n, r: (q0 + n * rows_per + r, 0)),
            pl.BlockSpec((DEC_SEQ, w), lambda n, r: (blk0 + n, 1)),
            pl.BlockSpec((DEC_SEQ, w), lambda n, r: (blk0 + n, 2)),
            kv_cache, kv_cache,
            pl.BlockSpec((NA_HEADS, None, GRID_W, NA_WIN_ROWS * GRID_W),
                         lambda n, r: (0, r - _na_row_start(r), 0, 0)),
            _ANY_SPEC,
        ],
        out_specs=pl.BlockSpec((GRID_W, w), lambda n, r: (q0 + n * rows_per + r, 0)),
        out_shape=jax.ShapeDtypeStruct((N_TOK, w), BF16),
        input_output_aliases={6: 0},
        compiler_params=_cparams(("parallel", "arbitrary")),
        name="na_lat",
    )(p_att, p_att, p_att, cache_k, cache_v, bias, oa)


def _rope(x, cos, sin):
    lane = lax.broadcasted_iota(jnp.int32, x.shape, 1)
    first = (lane % 32) < 16
    rot = jnp.where(first, -pltpu.roll(x, 128 - 16, 1), pltpu.roll(x, 16, 1))
    return x * cos + rot * sin


def _diff_kernel(*refs, rope, cache, lam_init, tq):
    if cache:
        q_ref, k_ref, v_ref, kc_ref, vc_ref, cos_ref, sin_ref, lam_ref, g_ref, _, o_ref = refs
    else:
        q_ref, k_ref, v_ref, lam_ref, g_ref = refs[:5]
        o_ref, new_k_ref, new_v_ref = refs[-3:]
        new_k_ref[...] = k_ref[...]
        new_v_ref[...] = v_ref[...]
    scale = DIFF_QK_DIM ** -0.5
    lp = lam_ref[...]
    lam = (jnp.exp(jnp.sum(lp[0:1] * lp[1:2], axis=-1, keepdims=True))
           - jnp.exp(jnp.sum(lp[2:3] * lp[3:4], axis=-1, keepdims=True)) + lam_init)
    hd = DIFF_V_DIM
    t = q_ref.shape[0]
    n_heads = q_ref.shape[1] // hd
    ks, vs = [], []
    for hh in range(n_heads):
        sl = slice(hh * hd, (hh + 1) * hd)
        k = k_ref[:, sl]
        v = v_ref[:, sl]
        if rope:
            k = _rope(k, cos_ref[...], sin_ref[...])
        if cache:
            k = jnp.concatenate([k, kc_ref[:, sl]], axis=0)
            v = jnp.concatenate([v, vc_ref[:, sl]], axis=0)
        ks.append(k.astype(BF16))
        vs.append(v.astype(BF16))
    units = [(hh, i) for hh in range(n_heads) for i in range(t // tq)]
    rows = {u: slice(u[1] * tq, (u[1] + 1) * tq) for u in units}
    cols = {u: slice(u[0] * hd, (u[0] + 1) * hd) for u in units}
    q = {u: q_ref[rows[u], cols[u]] for u in units}
    if rope:
        q = {u: _rope(q[u], cos_ref[rows[u], :], sin_ref[rows[u], :]) for u in units}
    lane = lax.broadcasted_iota(jnp.int32, (tq, hd), 1)
    s1 = {u: _dot(jnp.where(lane < DIFF_QK_DIM, q[u], 0.0).astype(BF16), ks[u[0]], NT) * scale for u in units}
    s2 = {u: _dot(jnp.where(lane >= DIFF_QK_DIM, q[u], 0.0).astype(BF16), ks[u[0]], NT) * scale for u in units}
    w = {u: (_softmax_rows(s1[u]) - lam * _softmax_rows(s2[u])).astype(BF16) for u in units}
    o = {u: _dot(w[u], vs[u[0]]) for u in units}
    for u in units:
        on = o[u] * lax.rsqrt(jnp.mean(o[u] * o[u], axis=-1, keepdims=True) + LN_EPS)
        o_ref[rows[u], cols[u]] = (on * g_ref[:, cols[u]] * (1.0 - lam_init)).astype(o_ref.dtype)


def _diff_attn(p_att, lam_p, subln, lam_init, layer, *, t, n_seq, blk0, cache=None, tables=None, stacks=None, ob=None):
    hd = DIFF_V_DIM * (1 if cache is not None else DIFF_HEADS)
    q_col, k_col, v_col = (z * BRANCH_WIDTH // hd for z in (3, 4, 5))
    in_specs = [
        pl.BlockSpec((t, hd), lambda n, h: (blk0 + n, q_col + h)),
        pl.BlockSpec((t, hd), lambda n, h: (blk0 + n, k_col + h)),
        pl.BlockSpec((t, hd), lambda n, h: (blk0 + n, v_col + h)),
    ]
    args = [p_att, p_att, p_att]
    stack_spec = pl.BlockSpec((None, None, t, hd), lambda n, h: (n, layer, 0, h))
    if cache is not None:
        cache_k, cache_v = cache
        tab = pl.BlockSpec((t, hd), lambda n, h: (0, 0))
        cache_spec = pl.BlockSpec((None, None, PAST_LEN, hd), lambda n, h: (n, layer, 0, h))
        in_specs += [cache_spec, cache_spec, tab, tab]
        args += [cache_k, cache_v, tables[0], tables[1]]
    in_specs += [
        pl.BlockSpec((4, DIFF_QK_DIM), lambda n, h: (0, 0)),
        pl.BlockSpec((1, hd), lambda n, h: (0, h)),
    ]
    args += [lam_p, subln.reshape(1, BRANCH_WIDTH)]
    out_specs = [pl.BlockSpec((t, hd), lambda n, h: (blk0 + n, h))]
    out_shape = [jax.ShapeDtypeStruct((N_TOK, BRANCH_WIDTH), BF16)]
    aliases = {}
    if cache is not None:
        aliases = {len(args): 0}
        in_specs.append(_ANY_SPEC)
        args.append(ob)
    else:
        out_specs += [stack_spec, stack_spec]
        out_shape += [_cache_stack_shape(), _cache_stack_shape()]
        if stacks is not None:
            aliases = {len(args): 1, len(args) + 1: 2}
            in_specs += [_ANY_SPEC, _ANY_SPEC]
            args += list(stacks)
    return pl.pallas_call(
        functools.partial(_diff_kernel, rope=cache is not None, cache=cache is not None,
                          lam_init=lam_init, tq=256),
        grid=(n_seq, BRANCH_WIDTH // hd),
        in_specs=in_specs,
        out_specs=out_specs,
        out_shape=out_shape,
        input_output_aliases=aliases,
        compiler_params=_cparams(("parallel", "parallel")),
        name="diff_lat" if cache is not None else "diff_ctx",
    )(*args)


def _rope_tables():
    t = np.arange(DEC_SEQ)
    rows = (t // GRID_W).astype(np.float32)
    cols = (t % GRID_W).astype(np.float32)
    half = DIFF_QK_DIM // 2
    inv = jnp.asarray(ROPE_THETA, F32) ** (-jnp.arange(0, half, 2, dtype=F32) / half)
    ang_r = jnp.asarray(rows)[:, None] * inv
    ang_c = jnp.asarray(cols)[:, None] * inv
    ang = jnp.concatenate([ang_r, ang_r, ang_c, ang_c] * 2, axis=-1)
    return jnp.cos(ang), jnp.sin(ang)


def _head_sum(x, ones_bd):
    return _dot_exact_rhs(x, ones_bd)


def _rwkv_prep_kernel(cur_ref, prev_ref, next_ref, mix_ref, w0_ref, w2_ref, a0_ref, a2_ref, g2_ref,
                      kk_ref, ka_ref, rk_ref, ones_ref,
                      r_ref, v_ref, kkn_ref, lwf_ref, lwb_ref, kf_ref, kb_ref, bf_ref, bb_ref, bonus_ref, g_ref,
                      *, tm):
    i = pl.program_id(0)
    blocks_per_seq = DEC_SEQ // tm
    j = i - N_CTX // tm
    is_ctx = i < N_CTX // tm
    at_start = jnp.logical_or(is_ctx, j % blocks_per_seq == 0)
    at_end = jnp.logical_or(is_ctx, j % blocks_per_seq == blocks_per_seq - 1)
    cur = cur_ref[...]
    row = lax.broadcasted_iota(jnp.int32, cur.shape, 0)
    prev_row = jnp.where(at_start, 0.0, prev_ref[7:8, :])
    next_row = jnp.where(at_end, 0.0, next_ref[0:1, :])
    prev = jnp.where(row == 0, prev_row, pltpu.roll(cur, 1, 0))
    nxt = jnp.where(row == tm - 1, next_row, pltpu.roll(cur, tm - 1, 0))
    f = cur + mix_ref[0:1, :] * (prev - cur) + mix_ref[1:2, :] * (nxt - cur)

    w = RWKV_WIDTH
    r = f[:, 0:w]
    k = f[:, w:2 * w]
    v = f[:, 2 * w:3 * w]
    c0 = 3 * w
    wd = f[:, c0:c0 + 2 * RWKV_DECAY_RANK]
    ad = f[:, c0 + 2 * RWKV_DECAY_RANK:c0 + 2 * RWKV_DECAY_RANK + 2 * RWKV_ICL_RANK]
    gd = f[:, c0 + 2 * RWKV_DECAY_RANK + 2 * RWKV_ICL_RANK:]
    ones_bd = ones_ref[...]

    kkv = k * kk_ref[...]
    norm = jnp.sqrt(_head_sum(kkv * kkv, ones_bd))
    kkn = kkv / jnp.maximum(norm, 1e-12)
    wlin = _dot(jnp.tanh(wd).astype(BF16), w2_ref[...]) + w0_ref[...]
    alin = _dot(ad.astype(BF16), a2_ref[...]) + a0_ref[...]
    g = _dot(jax.nn.sigmoid(gd).astype(BF16), g2_ref[...])

    r_ref[...] = r
    v_ref[...] = v
    kkn_ref[...] = kkn
    g_ref[...] = g
    bonus = jnp.zeros_like(v)
    for d, (lw_ref, kd_ref, bd_ref) in enumerate(((lwf_ref, kf_ref, bf_ref), (lwb_ref, kb_ref, bb_ref))):
        z = -wlin[:, d * w:(d + 1) * w]
        softplus = jnp.maximum(z, 0.0) + jnp.log(1.0 + jnp.exp(-jnp.abs(z)))
        lw_ref[...] = -jnp.exp(-softplus - 0.5)
        a = jax.nn.sigmoid(alin[:, d * w:(d + 1) * w])
        k_d = k * (1.0 + (a - 1.0) * ka_ref[...])
        kd_ref[...] = k_d
        bd_ref[...] = kkn * a
        bonus += _head_sum(r * k_d * rk_ref[...], ones_bd) * v
    bonus_ref[...] = bonus


def _block_diag2(a, b):
    za = jnp.zeros((a.shape[0], b.shape[1]), a.dtype)
    zb = jnp.zeros((b.shape[0], a.shape[1]), b.dtype)
    return jnp.concatenate([jnp.concatenate([a, za], axis=1), jnp.concatenate([zb, b], axis=1)], axis=0)


def _rwkv_prep(p_rw, lp):
    tm = 256
    w = RWKV_WIDTH
    sub = tm // 8
    full = lambda shape: pl.BlockSpec(shape, lambda i: (0,) * len(shape))
    out_spec = pl.BlockSpec((tm, w), lambda i: (i, 0))
    ones_bd = jnp.asarray(np.kron(np.eye(RWKV_HEADS), np.ones((RWKV_HEAD_DIM, RWKV_HEAD_DIM))), BF16)
    w2_bd = _block_diag2(lp['rwkv_w2'][0], lp['rwkv_w2'][1]).astype(BF16)
    a2_bd = _block_diag2(lp['rwkv_a2'][0], lp['rwkv_a2'][1]).astype(BF16)
    n_out = 11
    return pl.pallas_call(
        functools.partial(_rwkv_prep_kernel, tm=tm),
        grid=(N_TOK // tm,),
        in_specs=[
            pl.BlockSpec((tm, RWKV_FEAT), lambda i: (i, 0)),
            pl.BlockSpec((8, RWKV_FEAT), lambda i: (jnp.maximum(i * sub - 1, 0), 0)),
            pl.BlockSpec((8, RWKV_FEAT), lambda i: (jnp.minimum((i + 1) * sub, N_TOK // 8 - 1), 0)),
            full((2, RWKV_FEAT)),
            full((1, 2 * w)), full((2 * RWKV_DECAY_RANK, 2 * w)),
            full((1, 2 * w)), full((2 * RWKV_ICL_RANK, 2 * w)),
            full((RWKV_GATE_RANK, w)),
            full((1, w)), full((1, w)), full((1, w)),
            full((w, w)),
        ],
        out_specs=[out_spec] * n_out,
        out_shape=[jax.ShapeDtypeStruct((N_TOK, w), F32)] * n_out,
        compiler_params=_cparams(("parallel",)),
        name="rwkv_prep",
    )(p_rw, p_rw, p_rw, lp['rwkv_mix'],
      lp['rwkv_w0'].reshape(1, 2 * w), w2_bd, lp['rwkv_a0'].reshape(1, 2 * w), a2_bd,
      lp['rwkv_g2'].astype(BF16),
      lp['rwkv_kk'].reshape(1, w), lp['rwkv_ka'].reshape(1, w), lp['rwkv_rk'].reshape(1, w),
      ones_bd)


def _rwkv_chunk_maps(chains):
    n = len(chains)
    rs, ks, vs, kkns, bs, lws, fwds = (list(z) for z in zip(*chains))
    c, nk = rs[0].shape
    row = lax.broadcasted_iota(jnp.int32, (c, c), 0)
    col = lax.broadcasted_iota(jnp.int32, (c, c), 1)
    eye = row == col
    incl_d = {True: row >= col, False: row <= col}
    strict_d = {True: row > col, False: row < col}
    tri_d = {f: jnp.where(m, 1.0, 0.0).astype(BF16) for f, m in incl_d.items()}
    idx = range(n)
    cum = [_dot_exact_lhs(tri_d[fwds[i]], lws[i]) for i in idx]
    tot = [cum[i][c - 1:c, :] if fwds[i] else cum[i][0:1, :] for i in idx]
    e_neg = [jnp.exp(-cum[i]) for i in idx]
    a_t = [-kkns[i] * jnp.exp(cum[i] - lws[i]) for i in idx]
    r_t = [rs[i] * jnp.exp(cum[i]) for i in idx]
    ar = [jnp.concatenate([a_t[i], r_t[i]], axis=0) for i in idx]
    with_b = [_dotp(ar[i], bs[i] * e_neg[i], NT, RWKV_PASSES_PAIR) for i in idx]
    with_k = [_dotp(ar[i], ks[i] * e_neg[i], NT, RWKV_PASSES_PAIR) for i in idx]
    l_ab = [jnp.where(strict_d[fwds[i]], with_b[i][:c], 0.0) for i in idx]
    l_ak = [jnp.where(strict_d[fwds[i]], with_k[i][:c], 0.0) for i in idx]
    m_rb = [jnp.where(incl_d[fwds[i]], with_b[i][c:], 0.0) for i in idx]
    m_rk = [jnp.where(incl_d[fwds[i]], with_k[i][c:], 0.0) for i in idx]
    lakv = [_dotp(l_ak[i], vs[i], NN, RWKV_PASSES_APPLY) for i in idx]
    mrkv = [_dotp(m_rk[i], vs[i], NN, RWKV_PASSES_OUT) for i in idx]
    e_tail = [jnp.exp(tot[i] - cum[i]) for i in idx]
    kwv = [_dotp(ks[i] * e_tail[i], vs[i], TN, RWKV_PASSES_APPLY) for i in idx]
    same = lambda s: (row // s) == (col // s)
    inv = [jnp.where(eye, 1.0, 0.0) + jnp.where(same(2), l_ab[i], 0.0) for i in idx]
    size = 4
    while size <= c:
        part = jnp.logical_and(same(size), jnp.logical_not(same(size // 2)))
        half = [_dotp(inv[i], jnp.where(part, l_ab[i], 0.0), NN, RWKV_PASSES_SOLVE) for i in idx]
        inv = [inv[i] + _dotp(half[i], inv[i], NN, RWKV_PASSES_SOLVE) for i in idx]
        size *= 2
    pq1 = [_dotp(inv[i], jnp.concatenate([a_t[i], lakv[i]], axis=1), NN, RWKV_PASSES_APPLY) for i in idx]
    rb = [_dotp(m_rb[i], pq1[i], NN, RWKV_PASSES_OUT) for i in idx]
    sb = [_dotp(bs[i] * e_tail[i], pq1[i], TN, RWKV_PASSES_APPLY) for i in idx]
    out = []
    for i in idx:
        p2 = r_t[i] + rb[i][:, :nk]
        q2 = rb[i][:, nk:] + mrkv[i]
        p3 = jnp.where(eye, jnp.exp(tot[i]), 0.0) + sb[i][:, :nk]
        q3 = sb[i][:, nk:] + kwv[i]
        out.append((p2, p3, q2, q3))
    return out


def _rwkv_scan_kernel(rf, vf, af, wf, kf, bf, rb, vb, ab, wb, kb, bb, s0f, s0b, *rest, n_super, n_chunk):
    yf, yb, sff, sfb, st_ref = rest[-5:]
    sup = pl.program_id(2)
    hd = RWKV_HEAD_DIM
    cs = RWKV_CHUNK

    @pl.when(sup == 0)
    def _():
        st_ref[0:2] = s0f[...]
        st_ref[2:4] = s0b[...]

    dirs = ((rf, vf, af, wf, kf, bf, yf, True), (rb, vb, ab, wb, kb, bb, yb, False))
    keys, chains = [], []
    for d, (r_ref, v_ref, a_ref, w_ref, k_ref, b_ref, _, forward) in enumerate(dirs):
        for ci in range(n_chunk):
            rows = slice(ci * cs, (ci + 1) * cs)
            for hh in range(2):
                sl = slice(hh * hd, (hh + 1) * hd)
                keys.append((d, ci, hh))
                chains.append((r_ref[rows, sl], k_ref[rows, sl], v_ref[rows, sl],
                               a_ref[rows, sl], b_ref[rows, sl], w_ref[rows, sl], forward))
    maps = dict(zip(keys, _rwkv_chunk_maps(chains)))

    seqs = [(d, hh) for d in range(2) for hh in range(2)]
    st = {s: st_ref[2 * s[0] + s[1]] for s in seqs}
    ys = {}
    for step in range(n_chunk):
        for d, hh in seqs:
            ci = step if dirs[d][-1] else n_chunk - 1 - step
            p2, p3, q2, q3 = maps[d, ci, hh]
            ys[d, ci, hh] = _dotp(p2, st[d, hh], NN, RWKV_PASSES_OUT) + q2
            st[d, hh] = _dotp(p3, st[d, hh], NN, RWKV_PASSES_STATE) + q3
    for d, hh in seqs:
        st_ref[2 * d + hh] = st[d, hh]
    for d in range(2):
        y_ref = dirs[d][6]
        for ci in range(n_chunk):
            y_ref[ci * cs:(ci + 1) * cs, :] = jnp.concatenate([ys[d, ci, 0], ys[d, ci, 1]], axis=1)

    @pl.when(sup == n_super - 1)
    def _():
        sff[...] = st_ref[0:2]
        sfb[...] = st_ref[2:4]


def _rwkv_scan(feats, s0f_t, s0b_t, *, t, n_seq, row0, y_prev=None):
    r, v, kkn, lwf, lwb, kf, kb, bf, bb = feats
    rows = RWKV_SUPER
    n_super = t // rows
    blk0 = row0 // rows
    fwd = pl.BlockSpec((rows, 128), lambda n, p, s: (blk0 + n * n_super + s, p))
    bwd = pl.BlockSpec((rows, 128), lambda n, p, s: (blk0 + n * n_super + n_super - 1 - s, p))
    st_spec = pl.BlockSpec((None, 2, RWKV_HEAD_DIM, RWKV_HEAD_DIM), lambda n, p, s: (n, p, 0, 0))
    st_shape = jax.ShapeDtypeStruct((n_seq, RWKV_HEADS, RWKV_HEAD_DIM, RWKV_HEAD_DIM), F32)
    y_shape = jax.ShapeDtypeStruct((N_TOK, RWKV_WIDTH), F32)
    extra = [] if y_prev is None else list(y_prev)
    n_in = 14
    return pl.pallas_call(
        functools.partial(_rwkv_scan_kernel, n_super=n_super, n_chunk=rows // RWKV_CHUNK),
        grid=(n_seq, RWKV_HEADS // 2, n_super),
        in_specs=[fwd] * 6 + [bwd] * 6 + [st_spec, st_spec] + [_ANY_SPEC] * len(extra),
        out_specs=[fwd, bwd, st_spec, st_spec],
        out_shape=[y_shape, y_shape, st_shape, st_shape],
        input_output_aliases={n_in: 0, n_in + 1: 1} if extra else {},
        scratch_shapes=[pltpu.VMEM((4, RWKV_HEAD_DIM, RWKV_HEAD_DIM), F32)],
        compiler_params=_cparams(("parallel", "parallel", "arbitrary")),
        name="rwkv_scan",
    )(r, v, kkn, lwf, kf, bf, r, v, kkn, lwb, kb, bb, s0f_t, s0b_t, *extra)


def _rwkv_mixer(p_rw, s_lat_f, s_lat_b, lp):
    r, v, kkn, lwf, lwb, kf, kb, bf, bb, bonus, g = _rwkv_prep(p_rw, lp)
    feats = (r, v, kkn, lwf, lwb, kf, kb, bf, bb)
    s_zero = jnp.zeros((BATCH, RWKV_HEADS, RWKV_HEAD_DIM, RWKV_HEAD_DIM), F32)
    y_f, y_b, sf_c, sb_c = _rwkv_scan(feats, s_zero, s_zero, t=SEQ, n_seq=BATCH, row0=0)
    y_f, y_b, _, _ = _rwkv_scan(feats, jnp.swapaxes(s_lat_f, -1, -2), jnp.swapaxes(s_lat_b, -1, -2),
                                t=DEC_SEQ, n_seq=DEC_BATCH, row0=N_CTX, y_prev=(y_f, y_b))
    oc = _rwkv_out(y_f, y_b, bonus, g, lp['rwkv_lnx_g'], lp['rwkv_lnx_b'])
    return oc, jnp.swapaxes(sf_c, -1, -2), jnp.swapaxes(sb_c, -1, -2)


def _rwkv_out_kernel(yf_ref, yb_ref, bonus_ref, g_ref, lg_ref, lb_ref, ones_ref, o_ref):
    ones_bd = ones_ref[...]
    y = yf_ref[...] + yb_ref[...]
    mu = _head_sum(y, ones_bd) * (1.0 / RWKV_HEAD_DIM)
    yc = y - mu
    var = _head_sum(yc * yc, ones_bd) * (1.0 / RWKV_HEAD_DIM)
    yn = yc * lax.rsqrt(var + RWKV_GN_EPS) * lg_ref[...] + lb_ref[...]
    o_ref[...] = ((yn + bonus_ref[...]) * g_ref[...]).astype(o_ref.dtype)


def _rwkv_out(y_f, y_b, bonus, g, lnx_g, lnx_b):
    tm = 512
    w = RWKV_WIDTH
    row = pl.BlockSpec((tm, w), lambda i: (i, 0))
    vec = pl.BlockSpec((1, w), lambda i: (0, 0))
    ones_bd = jnp.asarray(np.kron(np.eye(RWKV_HEADS), np.ones((RWKV_HEAD_DIM, RWKV_HEAD_DIM))), BF16)
    return pl.pallas_call(
        _rwkv_out_kernel,
        grid=(N_TOK // tm,),
        in_specs=[row, row, row, row, vec, vec, pl.BlockSpec((w, w), lambda i: (0, 0))],
        out_specs=row,
        out_shape=jax.ShapeDtypeStruct((N_TOK, w), BF16),
        compiler_params=_cparams(("parallel",)),
        name="rwkv_out",
    )(y_f, y_b, bonus, g, lnx_g.reshape(1, w), lnx_b.reshape(1, w), ones_bd)


def kernel(x_prompt, x_sample, cache_na_k, cache_na_v, cache_diff_k, cache_diff_v, state_rwkv_fwd, state_rwkv_bwd, c, c_ctx, w_ada, b_ada, w_in, na_rpb, diff_lambda, diff_subln, rwkv_mix, rwkv_w0, rwkv_w2, rwkv_a0, rwkv_a2, rwkv_g2, rwkv_kk, rwkv_ka, rwkv_rk, rwkv_lnx_g, rwkv_lnx_b, w_branch, w_out, ln1_g, ln1_b, w_ffn_in, w_ffn_out, ln2_g, ln2_b):
    x = jnp.concatenate([x_prompt.reshape(N_CTX, D_MODEL), x_sample.reshape(N_LAT, D_MODEL)], axis=0)
    cond = jnp.concatenate([c_ctx[None, :], c, jnp.zeros((N_COND - 1 - DEC_BATCH, D_MODEL), F32)], axis=0)
    mods = _adaln_all(cond, w_ada, b_ada)
    mods = mods.reshape(DEPTH, N_COND, 6, D_MODEL).transpose(0, 2, 1, 3).reshape(DEPTH * 6 * N_COND, 1, D_MODEL)

    cache_na_k = cache_na_k.reshape(DEC_BATCH, DEPTH, PAST_LEN, BRANCH_WIDTH)
    cache_na_v = cache_na_v.reshape(DEC_BATCH, DEPTH, PAST_LEN, BRANCH_WIDTH)
    cache_diff_k = cache_diff_k.reshape(DEC_BATCH, DEPTH, PAST_LEN, BRANCH_WIDTH)
    cache_diff_v = cache_diff_v.reshape(DEC_BATCH, DEPTH, PAST_LEN, BRANCH_WIDTH)
    rope_tables = _rope_tables()

    c1 = ATT_WIDTH + RWKV_FEAT
    rw_pad = 128
    w_in16 = jnp.concatenate([w_in[:, :, :c1], jnp.zeros((DEPTH, D_MODEL, rw_pad), F32), w_in[:, :, c1:]],
                             axis=-1).astype(BF16)
    w_branch16 = w_branch.astype(BF16)
    w_out16 = w_out.astype(BF16)
    w_ffn_out16 = w_ffn_out.astype(BF16)

    h = _modulate(x, mods, 0)
    na_stacks, diff_stacks, new_f, new_b = None, None, [], []
    for l in range(DEPTH):
        lam_init = 0.8 - 0.6 * math.exp(-0.3 * l)
        lp = {'rwkv_mix': rwkv_mix[l], 'rwkv_w0': rwkv_w0[l], 'rwkv_w2': rwkv_w2[l], 'rwkv_a0': rwkv_a0[l],
              'rwkv_a2': rwkv_a2[l], 'rwkv_g2': rwkv_g2[l], 'rwkv_kk': rwkv_kk[l], 'rwkv_ka': rwkv_ka[l],
              'rwkv_rk': rwkv_rk[l], 'rwkv_lnx_g': rwkv_lnx_g[l], 'rwkv_lnx_b': rwkv_lnx_b[l]}
        p_att = _matmul(h, w_in16, l, F32, col0=0, n=ATT_WIDTH, name="in_att")
        p_rw = _matmul(h, w_in16, l, F32, col0=ATT_WIDTH, n=RWKV_FEAT + rw_pad, name="in_rwkv")
        gates = _matmul(h, w_in16, l, BF16, col0=c1 + rw_pad, n=GATE_WIDTH, name="in_gate", sigmoid=True)

        oa, *na_stacks = _na_ctx(p_att, l, na_stacks)
        oa = _na_lat(p_att, cache_na_k, cache_na_v, l, _na_bias_table(na_rpb[l]), oa)
        ob, *diff_stacks = _diff_attn(p_att, diff_lambda[l], diff_subln[l], lam_init, l, t=SEQ, n_seq=BATCH, blk0=0,
                                      stacks=diff_stacks)
        ob, = _diff_attn(p_att, diff_lambda[l], diff_subln[l], lam_init, l, t=DEC_SEQ, n_seq=DEC_BATCH,
                         blk0=N_CTX // DEC_SEQ, cache=(cache_diff_k, cache_diff_v), tables=rope_tables, ob=ob)
        oc, sf_c, sb_c = _rwkv_mixer(p_rw, state_rwkv_fwd[:, l], state_rwkv_bwd[:, l], lp)

        merged = _merge(oa, ob, oc, gates, w_branch16, l)
        mixed = _matmul(merged, w_out16, l, F32, name="w_out")
        x, h2 = _res_ln(x, mixed, mods, l, 2, ln1_g[l], ln1_b[l], l, 3)
        hid = _ffn_in(h2, w_ffn_in, l)
        ff = _matmul(hid, w_ffn_out16, l, F32, tn=512, name="ffn_out", vmem_mb=56)
        new_f.append(sf_c)
        new_b.append(sb_c)
        if l + 1 < DEPTH:
            x, h = _res_ln(x, ff, mods, l, 5, ln2_g[l], ln2_b[l], l + 1, 0)
        else:
            y_ctx, _ = _res_ln(x, ff, mods, l, 5, ln2_g[l], ln2_b[l], None, 0, rows=(0, N_CTX))
            y_lat, _ = _res_ln(x, ff, mods, l, 5, ln2_g[l], ln2_b[l], None, 0, rows=(N_CTX, N_LAT))

    y_prompt = y_ctx.reshape(BATCH, SEQ, D_MODEL)
    y_sample = y_lat.reshape(DEC_BATCH, DEC_SEQ, D_MODEL)
    new_na_k, new_na_v = (z.reshape(BATCH, DEPTH, SEQ, NA_HEADS, NA_HEAD_DIM) for z in na_stacks)
    new_diff_k = diff_stacks[0].reshape(BATCH, DEPTH, SEQ, DIFF_HEADS, 2, DIFF_QK_DIM)
    new_diff_v = diff_stacks[1].reshape(BATCH, DEPTH, SEQ, DIFF_HEADS, DIFF_V_DIM)
    return (y_prompt, y_sample, new_na_k, new_na_v, new_diff_k, new_diff_v,
            jnp.stack(new_f, axis=1), jnp.stack(new_b, axis=1))
```

```python
import functools
import math

import numpy as np
import jax
import jax.numpy as jnp
from jax import lax
from jax.experimental import pallas as pl
from jax.experimental.pallas import tpu as pltpu

F32 = jnp.float32
BF16 = jnp.bfloat16

D_MODEL = 2048
BATCH = 16
SEQ = 256
DEPTH = 4
DEC_BATCH = 4
DEC_SEQ = 1024
PAST_LEN = 256
GRID_W = 64
GRID_ROWS = DEC_SEQ // GRID_W
BRANCH_WIDTH = 512
N_BRANCH = 3
NA_HEADS = 4
NA_HEAD_DIM = 128
NA_WIN_ROWS = 8
NA_WIN_COLS = 16
NA_REL_ROWS = 2 * NA_WIN_ROWS - 1
NA_REL_COLS = 2 * NA_WIN_COLS - 1
DIFF_HEADS = 4
DIFF_QK_DIM = 64
DIFF_V_DIM = 128
RWKV_HEADS = 8
RWKV_HEAD_DIM = 64
RWKV_WIDTH = RWKV_HEADS * RWKV_HEAD_DIM
RWKV_DECAY_RANK = 64
RWKV_ICL_RANK = 64
RWKV_GATE_RANK = 128
RWKV_FEAT = 3 * RWKV_WIDTH + 2 * RWKV_DECAY_RANK + 2 * RWKV_ICL_RANK + RWKV_GATE_RANK
RWKV_GN_EPS = 64e-5
ATT_WIDTH = 6 * BRANCH_WIDTH
GATE_WIDTH = N_BRANCH * D_MODEL
FFN_HIDDEN = -(-8 * D_MODEL // (3 * 256)) * 256
ROPE_THETA = 10000.0
LN_EPS = 1e-5
ALPHA = (2.0 * DEPTH) ** 0.25

N_CTX = BATCH * SEQ
N_LAT = DEC_BATCH * DEC_SEQ
N_TOK = N_CTX + N_LAT
N_COND = 8
RWKV_CHUNK = 64
RWKV_SUPER = 256
RWKV_PASSES_PAIR = 1
RWKV_PASSES_SOLVE = 1
RWKV_PASSES_APPLY = 3
RWKV_PASSES_OUT = 1
RWKV_PASSES_STATE = 3
NEG_BIG = -1e30

NN = ((1,), (0,))
NT = ((1,), (1,))
TN = ((0,), (0,))


def _cparams(sem, vmem_mb=48):
    return pltpu.CompilerParams(dimension_semantics=sem, vmem_limit_bytes=vmem_mb * 1024 * 1024)


def _dot(a, b, dims=NN):
    return lax.dot_general(a, b, (dims, ((), ())), preferred_element_type=F32)


def _split2(x):
    hi = x.astype(BF16)
    lo = (x - hi.astype(F32)).astype(BF16)
    return hi, lo


def _dot3(a, b, dims=NN):
    ah, al = _split2(a)
    bh, bl = _split2(b)
    return _dot(ah, bh, dims) + (_dot(ah, bl, dims) + _dot(al, bh, dims))


def _dotp(a, b, dims, passes):
    if passes == 1:
        return _dot(a.astype(BF16), b.astype(BF16), dims)
    assert passes == 3
    return _dot3(a, b, dims)


def _dot_exact_lhs(a_bf16, b, dims=NN):
    b1 = b.astype(BF16)
    r1 = b - b1.astype(F32)
    b2 = r1.astype(BF16)
    b3 = (r1 - b2.astype(F32)).astype(BF16)
    return _dot(a_bf16, b1, dims) + (_dot(a_bf16, b2, dims) + _dot(a_bf16, b3, dims))


def _dot_exact_rhs(a, b_bf16, dims=NN):
    a1 = a.astype(BF16)
    r1 = a - a1.astype(F32)
    a2 = r1.astype(BF16)
    a3 = (r1 - a2.astype(F32)).astype(BF16)
    return _dot(a1, b_bf16, dims) + (_dot(a2, b_bf16, dims) + _dot(a3, b_bf16, dims))


def _normalize(x):
    mu = jnp.mean(x, axis=-1, keepdims=True)
    xc = x - mu
    var = jnp.mean(xc * xc, axis=-1, keepdims=True)
    return xc * lax.rsqrt(var + LN_EPS)


def _cond_of_row(row):
    return jnp.where(row < N_CTX, 0, 1 + (row - N_CTX) // DEC_SEQ)


def _adaln_kernel(c_ref, w_ref, b_ref, o_ref):
    c = c_ref[...]
    s = (c * jax.nn.sigmoid(c)).astype(BF16)
    o_ref[...] = _dot(s, w_ref[...].astype(BF16)) + b_ref[...]


def _adaln_all(cond, w_ada, b_ada):
    tn = 1024
    n = 6 * D_MODEL
    return pl.pallas_call(
        _adaln_kernel,
        grid=(DEPTH, n // tn),
        in_specs=[
            pl.BlockSpec((N_COND, D_MODEL), lambda l, j: (0, 0)),
            pl.BlockSpec((None, D_MODEL, tn), lambda l, j: (l, 0, j)),
            pl.BlockSpec((None, 1, tn), lambda l, j: (l, 0, j)),
        ],
        out_specs=pl.BlockSpec((None, N_COND, tn), lambda l, j: (l, 0, j)),
        out_shape=jax.ShapeDtypeStruct((DEPTH, N_COND, n), F32),
        compiler_params=_cparams(("parallel", "parallel")),
        name="adaln",
    )(cond, w_ada, b_ada.reshape(DEPTH, 1, n))


def _mod_spec(layer, which, tm, blk0=0):
    base = layer * 6 * N_COND + which * N_COND
    return pl.BlockSpec((None, 1, D_MODEL), lambda i: (base + _cond_of_row((blk0 + i) * tm), 0, 0))


def _modulate_kernel(x_ref, sh_ref, sc_ref, h_ref):
    h_ref[...] = (_normalize(x_ref[...]) * (1.0 + sc_ref[...]) + sh_ref[...]).astype(h_ref.dtype)


def _modulate(x, mods, layer):
    tm = 256
    return pl.pallas_call(
        _modulate_kernel,
        grid=(N_TOK // tm,),
        in_specs=[
            pl.BlockSpec((tm, D_MODEL), lambda i: (i, 0)),
            _mod_spec(layer, 0, tm),
            _mod_spec(layer, 1, tm),
        ],
        out_specs=pl.BlockSpec((tm, D_MODEL), lambda i: (i, 0)),
        out_shape=jax.ShapeDtypeStruct((N_TOK, D_MODEL), BF16),
        compiler_params=_cparams(("parallel",)),
        name="modulate",
    )(x, mods, mods)


def _res_ln_kernel(x_ref, z_ref, gate_ref, g_ref, b_ref, *rest, with_mod):
    y = ALPHA * x_ref[...] + gate_ref[...] * z_ref[...]
    xn = _normalize(y) * g_ref[...] + b_ref[...]
    if with_mod:
        sh_ref, sc_ref, xo_ref, h_ref = rest
        xo_ref[...] = xn
        h_ref[...] = (_normalize(xn) * (1.0 + sc_ref[...]) + sh_ref[...]).astype(h_ref.dtype)
    else:
        (xo_ref,) = rest
        xo_ref[...] = xn


def _res_ln(x, z, mods, layer, gate_idx, ln_g, ln_b, mod_layer, mod_idx, rows=(0, N_TOK)):
    tm = 256
    with_mod = mod_layer is not None
    blk0, n_rows = rows[0] // tm, rows[1]
    row_in = pl.BlockSpec((tm, D_MODEL), lambda i: (blk0 + i, 0))
    row = pl.BlockSpec((tm, D_MODEL), lambda i: (i, 0))
    vec = pl.BlockSpec((1, D_MODEL), lambda i: (0, 0))
    in_specs = [row_in, row_in, _mod_spec(layer, gate_idx, tm, blk0), vec, vec]
    args = [x, z, mods, ln_g.reshape(1, D_MODEL), ln_b.reshape(1, D_MODEL)]
    out_specs = [row]
    out_shape = [jax.ShapeDtypeStruct((n_rows, D_MODEL), F32)]
    if with_mod:
        in_specs += [_mod_spec(mod_layer, mod_idx, tm, blk0), _mod_spec(mod_layer, mod_idx + 1, tm, blk0)]
        args += [mods, mods]
        out_specs.append(row)
        out_shape.append(jax.ShapeDtypeStruct((n_rows, D_MODEL), BF16))
    out = pl.pallas_call(
        functools.partial(_res_ln_kernel, with_mod=with_mod),
        grid=(n_rows // tm,),
        in_specs=in_specs,
        out_specs=out_specs,
        out_shape=out_shape,
        compiler_params=_cparams(("parallel",)),
        name="res_ln",
    )(*args)
    return (out[0], out[1]) if with_mod else (out[0], None)


def _mm_kernel(a_ref, b_ref, o_ref, *, sigmoid):
    acc = _dot(a_ref[...], b_ref[...])
    if sigmoid:
        acc = jax.nn.sigmoid(acc)
    o_ref[...] = acc.astype(o_ref.dtype)


def _matmul(a, w, layer, out_dtype, col0=0, n=None, tm=1024, tn=1024, name="matmul", sigmoid=False, vmem_mb=48):
    m, k = a.shape
    n = w.shape[2] if n is None else n
    assert m % tm == 0 and n % tn == 0 and col0 % tn == 0
    j0 = col0 // tn
    return pl.pallas_call(
        functools.partial(_mm_kernel, sigmoid=sigmoid),
        grid=(m // tm, n // tn),
        in_specs=[
            pl.BlockSpec((tm, k), lambda i, j: (i, 0)),
            pl.BlockSpec((None, k, tn), lambda i, j: (layer, 0, j0 + j)),
        ],
        out_specs=pl.BlockSpec((tm, tn), lambda i, j: (i, j)),
        out_shape=jax.ShapeDtypeStruct((m, n), out_dtype),
        compiler_params=_cparams(("parallel", "parallel"), vmem_mb),
        name=name,
    )(a, w)


def _mm_w32_kernel(a_ref, w_ref, o_ref, w16_ref, *, sigmoid):
    @pl.when(pl.program_id(1) == 0)
    def _():
        w16_ref[...] = w_ref[0].astype(BF16)

    acc = _dot(a_ref[...], w16_ref[...])
    if sigmoid:
        acc = jax.nn.sigmoid(acc)
    o_ref[...] = acc.astype(o_ref.dtype)


def _matmul_w32(a, w, layer, out_dtype, col0=0, n=None, tm=1024, tn=1024, name="matmul", sigmoid=False):
    m, k = a.shape
    n = w.shape[2] if n is None else n
    assert m % tm == 0 and n % tn == 0
    return pl.pallas_call(
        functools.partial(_mm_w32_kernel, sigmoid=sigmoid),
        grid=(n // tn, m // tm),
        in_specs=[
            pl.BlockSpec((tm, k), lambda j, i: (i, 0)),
            pl.BlockSpec((pl.Element(1), pl.Element(k), pl.Element(tn)),
                         lambda j, i: (layer, 0, pl.multiple_of(col0 + j * tn, 128))),
        ],
        out_specs=pl.BlockSpec((tm, tn), lambda j, i: (i, j)),
        out_shape=jax.ShapeDtypeStruct((m, n), out_dtype),
        scratch_shapes=[pltpu.VMEM((k, tn), BF16)],
        compiler_params=_cparams(("parallel", "arbitrary")),
        name=name,
    )(a, w)


def _swiglu_kernel(a_ref, wg_ref, wu_ref, o_ref, wg16_ref, wu16_ref):
    @pl.when(pl.program_id(1) == 0)
    def _():
        wg16_ref[...] = wg_ref[...].astype(BF16)
        wu16_ref[...] = wu_ref[...].astype(BF16)

    a = a_ref[...]
    g = _dot(a, wg16_ref[...])
    u = _dot(a, wu16_ref[...])
    o_ref[...] = (g * jax.nn.sigmoid(g) * u).astype(o_ref.dtype)


def _ffn_in(h, w_ffn_in, layer):
    tm, tn = 1024, 512
    nb = FFN_HIDDEN // tn
    return pl.pallas_call(
        _swiglu_kernel,
        grid=(nb, N_TOK // tm),
        in_specs=[
            pl.BlockSpec((tm, D_MODEL), lambda j, i: (i, 0)),
            pl.BlockSpec((None, D_MODEL, tn), lambda j, i: (layer, 0, j)),
            pl.BlockSpec((None, D_MODEL, tn), lambda j, i: (layer, 0, j + nb)),
        ],
        out_specs=pl.BlockSpec((tm, tn), lambda j, i: (i, j)),
        out_shape=jax.ShapeDtypeStruct((N_TOK, FFN_HIDDEN), BF16),
        scratch_shapes=[pltpu.VMEM((D_MODEL, tn), BF16), pltpu.VMEM((D_MODEL, tn), BF16)],
        compiler_params=_cparams(("parallel", "arbitrary")),
        name="ffn_in",
    )(h, w_ffn_in, w_ffn_in)


def _merge_kernel(oa_ref, ob_ref, oc_ref, ga_ref, gb_ref, gc_ref, wb_ref, o_ref):
    acc = ga_ref[...].astype(F32) * _dot(oa_ref[...], wb_ref[0])
    acc += gb_ref[...].astype(F32) * _dot(ob_ref[...], wb_ref[1])
    acc += gc_ref[...].astype(F32) * _dot(oc_ref[...], wb_ref[2])
    o_ref[...] = acc.astype(o_ref.dtype)


def _merge(oa, ob, oc, gate_pre, w_branch, layer):
    tm, tn = 1024, 512
    nb = D_MODEL // tn
    o_spec = pl.BlockSpec((tm, BRANCH_WIDTH), lambda i, j: (i, 0))
    return pl.pallas_call(
        _merge_kernel,
        grid=(N_TOK // tm, nb),
        in_specs=[
            o_spec, o_spec, o_spec,
            pl.BlockSpec((tm, tn), lambda i, j: (i, j)),
            pl.BlockSpec((tm, tn), lambda i, j: (i, j + nb)),
            pl.BlockSpec((tm, tn), lambda i, j: (i, j + 2 * nb)),
            pl.BlockSpec((None, N_BRANCH, BRANCH_WIDTH, tn), lambda i, j: (layer, 0, 0, j)),
        ],
        out_specs=pl.BlockSpec((tm, tn), lambda i, j: (i, j)),
        out_shape=jax.ShapeDtypeStruct((N_TOK, D_MODEL), BF16),
        compiler_params=_cparams(("parallel", "parallel")),
        name="merge",
    )(oa, ob, oc, gate_pre, gate_pre, gate_pre, w_branch)


def _softmax_rows(s):
    m = jnp.max(s, axis=-1, keepdims=True)
    e = jnp.exp(s - m)
    return e * (1.0 / jnp.sum(e, axis=-1, keepdims=True))


def _na_ctx_kernel(q_ref, k_ref, v_ref, *rest):
    o_ref, new_k_ref, new_v_ref = rest[-3:]
    scale = NA_HEAD_DIM ** -0.5
    _store_heads(new_k_ref, k_ref)
    _store_heads(new_v_ref, v_ref)
    heads = range(NA_HEADS)
    sls = [slice(h * NA_HEAD_DIM, (h + 1) * NA_HEAD_DIM) for h in heads]
    s = [_dot(q_ref[:, sl].astype(BF16), k_ref[:, sl].astype(BF16), NT) * scale for sl in sls]
    p = [_softmax_rows(s[h]).astype(BF16) for h in heads]
    o = [_dot(p[h], v_ref[:, sls[h]].astype(BF16)) for h in heads]
    for h in heads:
        o_ref[:, sls[h]] = o[h].astype(o_ref.dtype)


_ANY_SPEC = pl.BlockSpec(memory_space=pl.ANY)


def _cache_stack_shape(split_heads):
    tail = (NA_HEADS, NA_HEAD_DIM) if split_heads else (BRANCH_WIDTH,)
    return jax.ShapeDtypeStruct((BATCH, DEPTH, SEQ) + tail, F32)


def _cache_stack_spec(split_heads, index_map):
    tail = (NA_HEADS, NA_HEAD_DIM) if split_heads else (BRANCH_WIDTH,)
    return pl.BlockSpec((None, None, SEQ) + tail, index_map)


def _store_heads(dst_ref, src_ref):
    for h in range(NA_HEADS):
        dst_ref[:, h, :] = src_ref[:, h * NA_HEAD_DIM:(h + 1) * NA_HEAD_DIM]


def _na_ctx(p_att, layer, stacks):
    w = BRANCH_WIDTH
    stack_spec = _cache_stack_spec(True, lambda n: (n, layer, 0, 0, 0))
    extra = [] if stacks is None else list(stacks)
    return pl.pallas_call(
        _na_ctx_kernel,
        grid=(BATCH,),
        in_specs=[
            pl.BlockSpec((SEQ, w), lambda n: (n, 0)),
            pl.BlockSpec((SEQ, w), lambda n: (n, 1)),
            pl.BlockSpec((SEQ, w), lambda n: (n, 2)),
        ] + [_ANY_SPEC] * len(extra),
        out_specs=[pl.BlockSpec((SEQ, w), lambda n: (n, 0)), stack_spec, stack_spec],
        out_shape=[jax.ShapeDtypeStruct((N_TOK, w), BF16), _cache_stack_shape(True), _cache_stack_shape(True)],
        input_output_aliases={3: 1, 4: 2} if extra else {},
        compiler_params=_cparams(("parallel",)),
        name="na_ctx",
    )(p_att, p_att, p_att, *extra)


def _na_row_start(r):
    return jnp.clip(r - NA_WIN_ROWS // 2, 0, GRID_ROWS - NA_WIN_ROWS)


def _na_lat_kernel(q_ref, k_ref, v_ref, kc_ref, vc_ref, bias_ref, _, o_ref):
    scale = NA_HEAD_DIM ** -0.5
    n_loc = NA_WIN_ROWS * GRID_W
    start = pl.multiple_of(_na_row_start(pl.program_id(1)) * GRID_W, GRID_W)
    k_loc = k_ref[pl.ds(start, n_loc), :].astype(BF16)
    v_loc = v_ref[pl.ds(start, n_loc), :].astype(BF16)
    heads = range(NA_HEADS)
    sls = [slice(h * NA_HEAD_DIM, (h + 1) * NA_HEAD_DIM) for h in heads]
    q = [q_ref[:, sl].astype(BF16) for sl in sls]
    s_loc = [_dot(q[h], k_loc[:, sls[h]], NT) * scale + bias_ref[h] for h in heads]
    s_ctx = [_dot(q[h], kc_ref[:, sls[h]].astype(BF16), NT) * scale for h in heads]
    m = [jnp.maximum(jnp.max(s_loc[h], axis=-1, keepdims=True), jnp.max(s_ctx[h], axis=-1, keepdims=True))
         for h in heads]
    e_loc = [jnp.exp(s_loc[h] - m[h]) for h in heads]
    e_ctx = [jnp.exp(s_ctx[h] - m[h]) for h in heads]
    inv = [1.0 / (jnp.sum(e_loc[h], axis=-1, keepdims=True) + jnp.sum(e_ctx[h], axis=-1, keepdims=True))
           for h in heads]
    o_loc = [_dot((e_loc[h] * inv[h]).astype(BF16), v_loc[:, sls[h]]) for h in heads]
    o_ctx = [_dot((e_ctx[h] * inv[h]).astype(BF16), vc_ref[:, sls[h]].astype(BF16)) for h in heads]
    for h in heads:
        o_ref[:, sls[h]] = (o_loc[h] + o_ctx[h]).astype(o_ref.dtype)


def _na_bias_kernel(rows_ref, onehot_ref, mask_ref, o_ref):
    o_ref[...] = _dot_exact_rhs(rows_ref[...], onehot_ref[...]) + mask_ref[...]


def _na_bias_table(rpb):
    nr, nd = NA_WIN_ROWS, NA_REL_COLS + 1
    rows = jnp.stack([rpb[:, nr - 1 - p:2 * nr - 1 - p, :] for p in range(nr)], axis=1)
    rows = jnp.pad(rows.astype(F32), ((0, 0), (0, 0), (0, 0), (0, nd - NA_REL_COLS)))
    qc = np.arange(GRID_W)[:, None]
    kc = np.arange(GRID_W)[None, :]
    rel_c = np.clip(kc - qc, -(NA_WIN_COLS - 1), NA_WIN_COLS - 1) + NA_WIN_COLS - 1
    onehot = (rel_c[None] == np.arange(nd)[:, None, None]).reshape(nd, GRID_W * GRID_W)
    win_c0 = np.clip(qc - NA_WIN_COLS // 2, 0, GRID_W - NA_WIN_COLS)
    valid = ((kc >= win_c0) & (kc < win_c0 + NA_WIN_COLS)).reshape(1, GRID_W * GRID_W)
    n_rows = NA_HEADS * nr * nr
    full = lambda shape: pl.BlockSpec(shape, lambda: (0,) * len(shape))
    bias = pl.pallas_call(
        _na_bias_kernel,
        in_specs=[full((n_rows, nd)), full((nd, GRID_W * GRID_W)), full((1, GRID_W * GRID_W))],
        out_specs=full((n_rows, GRID_W * GRID_W)),
        out_shape=jax.ShapeDtypeStruct((n_rows, GRID_W * GRID_W), F32),
        name="na_bias",
    )(rows.reshape(n_rows, nd), jnp.asarray(onehot, BF16), jnp.asarray(np.where(valid, 0.0, NEG_BIG), F32))
    bias = bias.reshape(NA_HEADS, nr, nr, GRID_W, GRID_W).transpose(0, 1, 3, 2, 4)
    return bias.reshape(NA_HEADS, nr, GRID_W, nr * GRID_W)


def _na_lat(p_att, cache_k, cache_v, layer, bias, oa):
    w = BRANCH_WIDTH
    blk0 = N_CTX // DEC_SEQ
    q0 = N_CTX // GRID_W
    rows_per = DEC_SEQ // GRID_W
    kv_cache = pl.BlockSpec((None, None, PAST_LEN, w), lambda n, r: (n, layer, 0, 0))
    return pl.pallas_call(
        _na_lat_kernel,
        grid=(DEC_BATCH, GRID_ROWS),
        in_specs=[
            pl.BlockSpec((GRID_W, w), lambda n, r: (q0 + n * rows_per + r, 0)),
            pl.BlockSpec((DEC_SEQ, w), lambda n, r: (blk0 + n, 1)),
            pl.BlockSpec((DEC_SEQ, w), lambda n, r: (blk0 + n, 2)),
            kv_cache, kv_cache,
            pl.BlockSpec((NA_HEADS, None, GRID_W, NA_WIN_ROWS * GRID_W),
                         lambda n, r: (0, r - _na_row_start(r), 0, 0)),
            _ANY_SPEC,
        ],
        out_specs=pl.BlockSpec((GRID_W, w), lambda n, r: (q0 + n * rows_per + r, 0)),
        out_shape=jax.ShapeDtypeStruct((N_TOK, w), BF16),
        input_output_aliases={6: 0},
        compiler_params=_cparams(("parallel", "arbitrary")),
        name="na_lat",
    )(p_att, p_att, p_att, cache_k, cache_v, bias, oa)


def _rope(x, cos, sin):
    lane = lax.broadcasted_iota(jnp.int32, x.shape, 1)
    first = (lane % 32) < 16
    rot = jnp.where(first, -pltpu.roll(x, 128 - 16, 1), pltpu.roll(x, 16, 1))
    return x * cos + rot * sin


def _diff_kernel(*refs, rope, cache, lam_init, tq):
    if cache:
        q_ref, k_ref, v_ref, kc_ref, vc_ref, cos_ref, sin_ref, lam_ref, g_ref, _, o_ref = refs
    else:
        q_ref, k_ref, v_ref, lam_ref, g_ref = refs[:5]
        o_ref, new_k_ref, new_v_ref = refs[-3:]
        new_k_ref[...] = k_ref[...]
        _store_heads(new_v_ref, v_ref)
    scale = DIFF_QK_DIM ** -0.5
    lp = lam_ref[...]
    lam = (jnp.exp(jnp.sum(lp[0:1] * lp[1:2], axis=-1, keepdims=True))
           - jnp.exp(jnp.sum(lp[2:3] * lp[3:4], axis=-1, keepdims=True)) + lam_init)
    hd = DIFF_V_DIM
    t = q_ref.shape[0]
    n_heads = q_ref.shape[1] // hd
    ks, vs = [], []
    for hh in range(n_heads):
        sl = slice(hh * hd, (hh + 1) * hd)
        k = k_ref[:, sl]
        v = v_ref[:, sl]
        if rope:
            k = _rope(k, cos_ref[...], sin_ref[...])
        if cache:
            k = jnp.concatenate([k, kc_ref[:, sl]], axis=0)
            v = jnp.concatenate([v, vc_ref[:, sl]], axis=0)
        ks.append(k.astype(BF16))
        vs.append(v.astype(BF16))
    units = [(hh, i) for hh in range(n_heads) for i in range(t // tq)]
    rows = {u: slice(u[1] * tq, (u[1] + 1) * tq) for u in units}
    cols = {u: slice(u[0] * hd, (u[0] + 1) * hd) for u in units}
    q = {u: q_ref[rows[u], cols[u]] for u in units}
    if rope:
        q = {u: _rope(q[u], cos_ref[rows[u], :], sin_ref[rows[u], :]) for u in units}
    lane = lax.broadcasted_iota(jnp.int32, (tq, hd), 1)
    s1 = {u: _dot(jnp.where(lane < DIFF_QK_DIM, q[u], 0.0).astype(BF16), ks[u[0]], NT) * scale for u in units}
    s2 = {u: _dot(jnp.where(lane >= DIFF_QK_DIM, q[u], 0.0).astype(BF16), ks[u[0]], NT) * scale for u in units}
    w = {u: (_softmax_rows(s1[u]) - lam * _softmax_rows(s2[u])).astype(BF16) for u in units}
    o = {u: _dot(w[u], vs[u[0]]) for u in units}
    for u in units:
        on = o[u] * lax.rsqrt(jnp.mean(o[u] * o[u], axis=-1, keepdims=True) + LN_EPS)
        o_ref[rows[u], cols[u]] = (on * g_ref[:, cols[u]] * (1.0 - lam_init)).astype(o_ref.dtype)


def _diff_attn(p_att, lam_p, subln, lam_init, layer, *, t, n_seq, blk0, cache=None, tables=None, stacks=None, ob=None):
    hd = DIFF_V_DIM * (1 if cache is not None else DIFF_HEADS)
    q_col, k_col, v_col = (z * BRANCH_WIDTH // hd for z in (3, 4, 5))
    in_specs = [
        pl.BlockSpec((t, hd), lambda n, h: (blk0 + n, q_col + h)),
        pl.BlockSpec((t, hd), lambda n, h: (blk0 + n, k_col + h)),
        pl.BlockSpec((t, hd), lambda n, h: (blk0 + n, v_col + h)),
    ]
    args = [p_att, p_att, p_att]
    if cache is not None:
        cache_k, cache_v = cache
        tab = pl.BlockSpec((t, hd), lambda n, h: (0, 0))
        cache_spec = pl.BlockSpec((None, None, PAST_LEN, hd), lambda n, h: (n, layer, 0, h))
        in_specs += [cache_spec, cache_spec, tab, tab]
        args += [cache_k, cache_v, tables[0], tables[1]]
    in_specs += [
        pl.BlockSpec((4, DIFF_QK_DIM), lambda n, h: (0, 0)),
        pl.BlockSpec((1, hd), lambda n, h: (0, h)),
    ]
    args += [lam_p, subln.reshape(1, BRANCH_WIDTH)]
    out_specs = [pl.BlockSpec((t, hd), lambda n, h: (blk0 + n, h))]
    out_shape = [jax.ShapeDtypeStruct((N_TOK, BRANCH_WIDTH), BF16)]
    aliases = {}
    if cache is not None:
        aliases = {len(args): 0}
        in_specs.append(_ANY_SPEC)
        args.append(ob)
    else:
        out_specs += [_cache_stack_spec(False, lambda n, h: (n, layer, 0, 0)),
                      _cache_stack_spec(True, lambda n, h: (n, layer, 0, 0, 0))]
        out_shape += [_cache_stack_shape(False), _cache_stack_shape(True)]
        if stacks is not None:
            aliases = {len(args): 1, len(args) + 1: 2}
            in_specs += [_ANY_SPEC, _ANY_SPEC]
            args += list(stacks)
    return pl.pallas_call(
        functools.partial(_diff_kernel, rope=cache is not None, cache=cache is not None,
                          lam_init=lam_init, tq=256),
        grid=(n_seq, BRANCH_WIDTH // hd),
        in_specs=in_specs,
        out_specs=out_specs,
        out_shape=out_shape,
        input_output_aliases=aliases,
        compiler_params=_cparams(("parallel", "parallel")),
        name="diff_lat" if cache is not None else "diff_ctx",
    )(*args)


def _rope_tables():
    t = np.arange(DEC_SEQ)
    rows = (t // GRID_W).astype(np.float32)
    cols = (t % GRID_W).astype(np.float32)
    half = DIFF_QK_DIM // 2
    inv = jnp.asarray(ROPE_THETA, F32) ** (-jnp.arange(0, half, 2, dtype=F32) / half)
    ang_r = jnp.asarray(rows)[:, None] * inv
    ang_c = jnp.asarray(cols)[:, None] * inv
    ang = jnp.concatenate([ang_r, ang_r, ang_c, ang_c] * 2, axis=-1)
    return jnp.cos(ang), jnp.sin(ang)


def _head_sum(x, ones_bd):
    return _dot_exact_rhs(x, ones_bd)


def _rwkv_prep_kernel(cur_ref, prev_ref, next_ref, mix_ref, w0_ref, w2_ref, a0_ref, a2_ref, g2_ref,
                      kk_ref, ka_ref, rk_ref, ones_ref,
                      r_ref, v_ref, kkn_ref, lwf_ref, lwb_ref, kf_ref, kb_ref, bf_ref, bb_ref, bonus_ref, g_ref,
                      *, tm):
    i = pl.program_id(0)
    blocks_per_seq = DEC_SEQ // tm
    j = i - N_CTX // tm
    is_ctx = i < N_CTX // tm
    at_start = jnp.logical_or(is_ctx, j % blocks_per_seq == 0)
    at_end = jnp.logical_or(is_ctx, j % blocks_per_seq == blocks_per_seq - 1)
    cur = cur_ref[...]
    row = lax.broadcasted_iota(jnp.int32, cur.shape, 0)
    prev_row = jnp.where(at_start, 0.0, prev_ref[7:8, :])
    next_row = jnp.where(at_end, 0.0, next_ref[0:1, :])
    prev = jnp.where(row == 0, prev_row, pltpu.roll(cur, 1, 0))
    nxt = jnp.where(row == tm - 1, next_row, pltpu.roll(cur, tm - 1, 0))
    f = cur + mix_ref[0:1, :] * (prev - cur) + mix_ref[1:2, :] * (nxt - cur)

    w = RWKV_WIDTH
    r = f[:, 0:w]
    k = f[:, w:2 * w]
    v = f[:, 2 * w:3 * w]
    c0 = 3 * w
    wd = f[:, c0:c0 + 2 * RWKV_DECAY_RANK]
    ad = f[:, c0 + 2 * RWKV_DECAY_RANK:c0 + 2 * RWKV_DECAY_RANK + 2 * RWKV_ICL_RANK]
    gd = f[:, c0 + 2 * RWKV_DECAY_RANK + 2 * RWKV_ICL_RANK:]
    ones_bd = ones_ref[...]

    kkv = k * kk_ref[...]
    norm = jnp.sqrt(_head_sum(kkv * kkv, ones_bd))
    kkn = kkv / jnp.maximum(norm, 1e-12)
    wlin = _dot(jnp.tanh(wd).astype(BF16), w2_ref[...]) + w0_ref[...]
    alin = _dot(ad.astype(BF16), a2_ref[...]) + a0_ref[...]
    g = _dot(jax.nn.sigmoid(gd).astype(BF16), g2_ref[...])

    r_ref[...] = r
    v_ref[...] = v
    kkn_ref[...] = kkn
    g_ref[...] = g
    bonus = jnp.zeros_like(v)
    for d, (lw_ref, kd_ref, bd_ref) in enumerate(((lwf_ref, kf_ref, bf_ref), (lwb_ref, kb_ref, bb_ref))):
        z = -wlin[:, d * w:(d + 1) * w]
        softplus = jnp.maximum(z, 0.0) + jnp.log(1.0 + jnp.exp(-jnp.abs(z)))
        lw_ref[...] = -jnp.exp(-softplus - 0.5)
        a = jax.nn.sigmoid(alin[:, d * w:(d + 1) * w])
        k_d = k * (1.0 + (a - 1.0) * ka_ref[...])
        kd_ref[...] = k_d
        bd_ref[...] = kkn * a
        bonus += _head_sum(r * k_d * rk_ref[...], ones_bd) * v
    bonus_ref[...] = bonus


def _block_diag2(a, b):
    za = jnp.zeros((a.shape[0], b.shape[1]), a.dtype)
    zb = jnp.zeros((b.shape[0], a.shape[1]), b.dtype)
    return jnp.concatenate([jnp.concatenate([a, za], axis=1), jnp.concatenate([zb, b], axis=1)], axis=0)


def _rwkv_prep(p_rw, lp):
    tm = 256
    w = RWKV_WIDTH
    sub = tm // 8
    full = lambda shape: pl.BlockSpec(shape, lambda i: (0,) * len(shape))
    out_spec = pl.BlockSpec((tm, w), lambda i: (i, 0))
    ones_bd = jnp.asarray(np.kron(np.eye(RWKV_HEADS), np.ones((RWKV_HEAD_DIM, RWKV_HEAD_DIM))), BF16)
    w2_bd = _block_diag2(lp['rwkv_w2'][0], lp['rwkv_w2'][1]).astype(BF16)
    a2_bd = _block_diag2(lp['rwkv_a2'][0], lp['rwkv_a2'][1]).astype(BF16)
    n_out = 11
    return pl.pallas_call(
        functools.partial(_rwkv_prep_kernel, tm=tm),
        grid=(N_TOK // tm,),
        in_specs=[
            pl.BlockSpec((tm, RWKV_FEAT), lambda i: (i, 0)),
            pl.BlockSpec((8, RWKV_FEAT), lambda i: (jnp.maximum(i * sub - 1, 0), 0)),
            pl.BlockSpec((8, RWKV_FEAT), lambda i: (jnp.minimum((i + 1) * sub, N_TOK // 8 - 1), 0)),
            full((2, RWKV_FEAT)),
            full((1, 2 * w)), full((2 * RWKV_DECAY_RANK, 2 * w)),
            full((1, 2 * w)), full((2 * RWKV_ICL_RANK, 2 * w)),
            full((RWKV_GATE_RANK, w)),
            full((1, w)), full((1, w)), full((1, w)),
            full((w, w)),
        ],
        out_specs=[out_spec] * n_out,
        out_shape=[jax.ShapeDtypeStruct((N_TOK, w), F32)] * n_out,
        compiler_params=_cparams(("parallel",)),
        name="rwkv_prep",
    )(p_rw, p_rw, p_rw, lp['rwkv_mix'],
      lp['rwkv_w0'].reshape(1, 2 * w), w2_bd, lp['rwkv_a0'].reshape(1, 2 * w), a2_bd,
      lp['rwkv_g2'].astype(BF16),
      lp['rwkv_kk'].reshape(1, w), lp['rwkv_ka'].reshape(1, w), lp['rwkv_rk'].reshape(1, w),
      ones_bd)


def _rwkv_chunk_maps(chains):
    n = len(chains)
    rs, ks, vs, kkns, bs, lws, fwds = (list(z) for z in zip(*chains))
    c, nk = rs[0].shape
    row = lax.broadcasted_iota(jnp.int32, (c, c), 0)
    col = lax.broadcasted_iota(jnp.int32, (c, c), 1)
    eye = row == col
    incl_d = {True: row >= col, False: row <= col}
    strict_d = {True: row > col, False: row < col}
    tri_d = {f: jnp.where(m, 1.0, 0.0).astype(BF16) for f, m in incl_d.items()}
    idx = range(n)
    cum = [_dot_exact_lhs(tri_d[fwds[i]], lws[i]) for i in idx]
    tot = [cum[i][c - 1:c, :] if fwds[i] else cum[i][0:1, :] for i in idx]
    e_neg = [jnp.exp(-cum[i]) for i in idx]
    a_t = [-kkns[i] * jnp.exp(cum[i] - lws[i]) for i in idx]
    r_t = [rs[i] * jnp.exp(cum[i]) for i in idx]
    ar = [jnp.concatenate([a_t[i], r_t[i]], axis=0) for i in idx]
    with_b = [_dotp(ar[i], bs[i] * e_neg[i], NT, RWKV_PASSES_PAIR) for i in idx]
    with_k = [_dotp(ar[i], ks[i] * e_neg[i], NT, RWKV_PASSES_PAIR) for i in idx]
    l_ab = [jnp.where(strict_d[fwds[i]], with_b[i][:c], 0.0) for i in idx]
    l_ak = [jnp.where(strict_d[fwds[i]], with_k[i][:c], 0.0) for i in idx]
    m_rb = [jnp.where(incl_d[fwds[i]], with_b[i][c:], 0.0) for i in idx]
    m_rk = [jnp.where(incl_d[fwds[i]], with_k[i][c:], 0.0) for i in idx]
    lakv = [_dotp(l_ak[i], vs[i], NN, RWKV_PASSES_APPLY) for i in idx]
    mrkv = [_dotp(m_rk[i], vs[i], NN, RWKV_PASSES_OUT) for i in idx]
    e_tail = [jnp.exp(tot[i] - cum[i]) for i in idx]
    kwv = [_dotp(ks[i] * e_tail[i], vs[i], TN, RWKV_PASSES_APPLY) for i in idx]
    same = lambda s: (row // s) == (col // s)
    inv = [jnp.where(eye, 1.0, 0.0) + jnp.where(same(2), l_ab[i], 0.0) for i in idx]
    size = 4
    while size <= c:
        part = jnp.logical_and(same(size), jnp.logical_not(same(size // 2)))
        half = [_dotp(inv[i], jnp.where(part, l_ab[i], 0.0), NN, RWKV_PASSES_SOLVE) for i in idx]
        inv = [inv[i] + _dotp(half[i], inv[i], NN, RWKV_PASSES_SOLVE) for i in idx]
        size *= 2
    pq1 = [_dotp(inv[i], jnp.concatenate([a_t[i], lakv[i]], axis=1), NN, RWKV_PASSES_APPLY) for i in idx]
    rb = [_dotp(m_rb[i], pq1[i], NN, RWKV_PASSES_OUT) for i in idx]
    sb = [_dotp(bs[i] * e_tail[i], pq1[i], TN, RWKV_PASSES_APPLY) for i in idx]
    out = []
    for i in idx:
        p2 = r_t[i] + rb[i][:, :nk]
        q2 = rb[i][:, nk:] + mrkv[i]
        p3 = jnp.where(eye, jnp.exp(tot[i]), 0.0) + sb[i][:, :nk]
        q3 = sb[i][:, nk:] + kwv[i]
        out.append((p2, p3, q2, q3))
    return out


def _rwkv_scan_kernel(rf, vf, af, wf, kf, bf, rb, vb, ab, wb, kb, bb, s0f, s0b, *rest, n_super, n_chunk):
    yf, yb, sff, sfb, st_ref = rest[-5:]
    sup = pl.program_id(2)
    hd = RWKV_HEAD_DIM
    cs = RWKV_CHUNK

    @pl.when(sup == 0)
    def _():
        st_ref[0:2] = s0f[...]
        st_ref[2:4] = s0b[...]

    dirs = ((rf, vf, af, wf, kf, bf, yf, True), (rb, vb, ab, wb, kb, bb, yb, False))
    keys, chains = [], []
    for d, (r_ref, v_ref, a_ref, w_ref, k_ref, b_ref, _, forward) in enumerate(dirs):
        for ci in range(n_chunk):
            rows = slice(ci * cs, (ci + 1) * cs)
            for hh in range(2):
                sl = slice(hh * hd, (hh + 1) * hd)
                keys.append((d, ci, hh))
                chains.append((r_ref[rows, sl], k_ref[rows, sl], v_ref[rows, sl],
                               a_ref[rows, sl], b_ref[rows, sl], w_ref[rows, sl], forward))
    maps = dict(zip(keys, _rwkv_chunk_maps(chains)))

    seqs = [(d, hh) for d in range(2) for hh in range(2)]
    st = {s: st_ref[2 * s[0] + s[1]] for s in seqs}
    ys = {}
    for step in range(n_chunk):
        for d, hh in seqs:
            ci = step if dirs[d][-1] else n_chunk - 1 - step
            p2, p3, q2, q3 = maps[d, ci, hh]
            ys[d, ci, hh] = _dotp(p2, st[d, hh], NN, RWKV_PASSES_OUT) + q2
            st[d, hh] = _dotp(p3, st[d, hh], NN, RWKV_PASSES_STATE) + q3
    for d, hh in seqs:
        st_ref[2 * d + hh] = st[d, hh]
    for d in range(2):
        y_ref = dirs[d][6]
        for ci in range(n_chunk):
            y_ref[ci * cs:(ci + 1) * cs, :] = jnp.concatenate([ys[d, ci, 0], ys[d, ci, 1]], axis=1)

    @pl.when(sup == n_super - 1)
    def _():
        sff[...] = st_ref[0:2]
        sfb[...] = st_ref[2:4]


def _rwkv_scan(feats, s0f_t, s0b_t, *, t, n_seq, row0, y_prev=None):
    r, v, kkn, lwf, lwb, kf, kb, bf, bb = feats
    rows = RWKV_SUPER
    n_super = t // rows
    blk0 = row0 // rows
    fwd = pl.BlockSpec((rows, 128), lambda n, p, s: (blk0 + n * n_super + s, p))
    bwd = pl.BlockSpec((rows, 128), lambda n, p, s: (blk0 + n * n_super + n_super - 1 - s, p))
    st_spec = pl.BlockSpec((None, 2, RWKV_HEAD_DIM, RWKV_HEAD_DIM), lambda n, p, s: (n, p, 0, 0))
    st_shape = jax.ShapeDtypeStruct((n_seq, RWKV_HEADS, RWKV_HEAD_DIM, RWKV_HEAD_DIM), F32)
    y_shape = jax.ShapeDtypeStruct((N_TOK, RWKV_WIDTH), F32)
    extra = [] if y_prev is None else list(y_prev)
    n_in = 14
    return pl.pallas_call(
        functools.partial(_rwkv_scan_kernel, n_super=n_super, n_chunk=rows // RWKV_CHUNK),
        grid=(n_seq, RWKV_HEADS // 2, n_super),
        in_specs=[fwd] * 6 + [bwd] * 6 + [st_spec, st_spec] + [_ANY_SPEC] * len(extra),
        out_specs=[fwd, bwd, st_spec, st_spec],
        out_shape=[y_shape, y_shape, st_shape, st_shape],
        input_output_aliases={n_in: 0, n_in + 1: 1} if extra else {},
        scratch_shapes=[pltpu.VMEM((4, RWKV_HEAD_DIM, RWKV_HEAD_DIM), F32)],
        compiler_params=_cparams(("parallel", "parallel", "arbitrary")),
        name="rwkv_scan",
    )(r, v, kkn, lwf, kf, bf, r, v, kkn, lwb, kb, bb, s0f_t, s0b_t, *extra)


def _rwkv_mixer(p_rw, s_lat_f, s_lat_b, lp):
    r, v, kkn, lwf, lwb, kf, kb, bf, bb, bonus, g = _rwkv_prep(p_rw, lp)
    feats = (r, v, kkn, lwf, lwb, kf, kb, bf, bb)
    s_zero = jnp.zeros((BATCH, RWKV_HEADS, RWKV_HEAD_DIM, RWKV_HEAD_DIM), F32)
    y_f, y_b, sf_c, sb_c = _rwkv_scan(feats, s_zero, s_zero, t=SEQ, n_seq=BATCH, row0=0)
    y_f, y_b, _, _ = _rwkv_scan(feats, jnp.swapaxes(s_lat_f, -1, -2), jnp.swapaxes(s_lat_b, -1, -2),
                                t=DEC_SEQ, n_seq=DEC_BATCH, row0=N_CTX, y_prev=(y_f, y_b))
    oc = _rwkv_out(y_f, y_b, bonus, g, lp['rwkv_lnx_g'], lp['rwkv_lnx_b'])
    return oc, jnp.swapaxes(sf_c, -1, -2), jnp.swapaxes(sb_c, -1, -2)


def _rwkv_out_kernel(yf_ref, yb_ref, bonus_ref, g_ref, lg_ref, lb_ref, ones_ref, o_ref):
    ones_bd = ones_ref[...]
    y = yf_ref[...] + yb_ref[...]
    mu = _head_sum(y, ones_bd) * (1.0 / RWKV_HEAD_DIM)
    yc = y - mu
    var = _head_sum(yc * yc, ones_bd) * (1.0 / RWKV_HEAD_DIM)
    yn = yc * lax.rsqrt(var + RWKV_GN_EPS) * lg_ref[...] + lb_ref[...]
    o_ref[...] = ((yn + bonus_ref[...]) * g_ref[...]).astype(o_ref.dtype)


def _rwkv_out(y_f, y_b, bonus, g, lnx_g, lnx_b):
    tm = 512
    w = RWKV_WIDTH
    row = pl.BlockSpec((tm, w), lambda i: (i, 0))
    vec = pl.BlockSpec((1, w), lambda i: (0, 0))
    ones_bd = jnp.asarray(np.kron(np.eye(RWKV_HEADS), np.ones((RWKV_HEAD_DIM, RWKV_HEAD_DIM))), BF16)
    return pl.pallas_call(
        _rwkv_out_kernel,
        grid=(N_TOK // tm,),
        in_specs=[row, row, row, row, vec, vec, pl.BlockSpec((w, w), lambda i: (0, 0))],
        out_specs=row,
        out_shape=jax.ShapeDtypeStruct((N_TOK, w), BF16),
        compiler_params=_cparams(("parallel",)),
        name="rwkv_out",
    )(y_f, y_b, bonus, g, lnx_g.reshape(1, w), lnx_b.reshape(1, w), ones_bd)


def kernel(x_prompt, x_sample, cache_na_k, cache_na_v, cache_diff_k, cache_diff_v, state_rwkv_fwd, state_rwkv_bwd, c, c_ctx, w_ada, b_ada, w_in, na_rpb, diff_lambda, diff_subln, rwkv_mix, rwkv_w0, rwkv_w2, rwkv_a0, rwkv_a2, rwkv_g2, rwkv_kk, rwkv_ka, rwkv_rk, rwkv_lnx_g, rwkv_lnx_b, w_branch, w_out, ln1_g, ln1_b, w_ffn_in, w_ffn_out, ln2_g, ln2_b):
    x = jnp.concatenate([x_prompt.reshape(N_CTX, D_MODEL), x_sample.reshape(N_LAT, D_MODEL)], axis=0)
    cond = jnp.concatenate([c_ctx[None, :], c, jnp.zeros((N_COND - 1 - DEC_BATCH, D_MODEL), F32)], axis=0)
    mods = _adaln_all(cond, w_ada, b_ada)
    mods = mods.reshape(DEPTH, N_COND, 6, D_MODEL).transpose(0, 2, 1, 3).reshape(DEPTH * 6 * N_COND, 1, D_MODEL)

    cache_na_k = cache_na_k.reshape(DEC_BATCH, DEPTH, PAST_LEN, BRANCH_WIDTH)
    cache_na_v = cache_na_v.reshape(DEC_BATCH, DEPTH, PAST_LEN, BRANCH_WIDTH)
    cache_diff_k = cache_diff_k.reshape(DEC_BATCH, DEPTH, PAST_LEN, BRANCH_WIDTH)
    cache_diff_v = cache_diff_v.reshape(DEC_BATCH, DEPTH, PAST_LEN, BRANCH_WIDTH)
    rope_tables = _rope_tables()

    c1 = ATT_WIDTH + RWKV_FEAT
    w_branch16 = w_branch.astype(BF16)
    w_ffn_out16 = w_ffn_out.astype(BF16)

    h = _modulate(x, mods, 0)
    na_stacks, diff_stacks, new_f, new_b = None, None, [], []
    for l in range(DEPTH):
        lam_init = 0.8 - 0.6 * math.exp(-0.3 * l)
        lp = {'rwkv_mix': rwkv_mix[l], 'rwkv_w0': rwkv_w0[l], 'rwkv_w2': rwkv_w2[l], 'rwkv_a0': rwkv_a0[l],
              'rwkv_a2': rwkv_a2[l], 'rwkv_g2': rwkv_g2[l], 'rwkv_kk': rwkv_kk[l], 'rwkv_ka': rwkv_ka[l],
              'rwkv_rk': rwkv_rk[l], 'rwkv_lnx_g': rwkv_lnx_g[l], 'rwkv_lnx_b': rwkv_lnx_b[l]}
        p_att = _matmul_w32(h, w_in, l, F32, col0=0, n=ATT_WIDTH, name="in_att")
        p_rw = _matmul_w32(h, w_in, l, F32, col0=ATT_WIDTH, n=RWKV_FEAT, tn=RWKV_FEAT // 3, name="in_rwkv")
        gates = _matmul_w32(h, w_in, l, BF16, col0=c1, n=GATE_WIDTH, name="in_gate", sigmoid=True)

        oa, *na_stacks = _na_ctx(p_att, l, na_stacks)
        oa = _na_lat(p_att, cache_na_k, cache_na_v, l, _na_bias_table(na_rpb[l]), oa)
        ob, *diff_stacks = _diff_attn(p_att, diff_lambda[l], diff_subln[l], lam_init, l, t=SEQ, n_seq=BATCH, blk0=0,
                                      stacks=diff_stacks)
        ob, = _diff_attn(p_att, diff_lambda[l], diff_subln[l], lam_init, l, t=DEC_SEQ, n_seq=DEC_BATCH,
                         blk0=N_CTX // DEC_SEQ, cache=(cache_diff_k, cache_diff_v), tables=rope_tables, ob=ob)
        oc, sf_c, sb_c = _rwkv_mixer(p_rw, state_rwkv_fwd[:, l], state_rwkv_bwd[:, l], lp)

        merged = _merge(oa, ob, oc, gates, w_branch16, l)
        mixed = _matmul_w32(merged, w_out, l, F32, name="w_out")
        x, h2 = _res_ln(x, mixed, mods, l, 2, ln1_g[l], ln1_b[l], l, 3)
        hid = _ffn_in(h2, w_ffn_in, l)
        ff = _matmul(hid, w_ffn_out16, l, F32, tn=512, name="ffn_out", vmem_mb=56)
        new_f.append(sf_c)
        new_b.append(sb_c)
        if l + 1 < DEPTH:
            x, h = _res_ln(x, ff, mods, l, 5, ln2_g[l], ln2_b[l], l + 1, 0)
        else:
            y_ctx, _ = _res_ln(x, ff, mods, l, 5, ln2_g[l], ln2_b[l], None, 0, rows=(0, N_CTX))
            y_lat, _ = _res_ln(x, ff, mods, l, 5, ln2_g[l], ln2_b[l], None, 0, rows=(N_CTX, N_LAT))

    y_prompt = y_ctx.reshape(BATCH, SEQ, D_MODEL)
    y_sample = y_lat.reshape(DEC_BATCH, DEC_SEQ, D_MODEL)
    new_na_k, new_na_v = na_stacks
    new_diff_k = diff_stacks[0].reshape(BATCH, DEPTH, SEQ, DIFF_HEADS, 2, DIFF_QK_DIM)
    new_diff_v = diff_stacks[1]
    return (y_prompt, y_sample, new_na_k, new_na_v, new_diff_k, new_diff_v,
            jnp.stack(new_f, axis=1), jnp.stack(new_b, axis=1))
```

```python
import functools
import math

import numpy as np
import jax
import jax.numpy as jnp
from jax import lax
from jax.experimental import pallas as pl
from jax.experimental.pallas import tpu as pltpu

F32 = jnp.float32
BF16 = jnp.bfloat16

D_MODEL = 2048
BATCH = 16
SEQ = 256
DEPTH = 4
DEC_BATCH = 4
DEC_SEQ = 1024
PAST_LEN = 256
GRID_W = 64
GRID_ROWS = DEC_SEQ // GRID_W
BRANCH_WIDTH = 512
N_BRANCH = 3
NA_HEADS = 4
NA_HEAD_DIM = 128
NA_WIN_ROWS = 8
NA_WIN_COLS = 16
NA_REL_ROWS = 2 * NA_WIN_ROWS - 1
NA_REL_COLS = 2 * NA_WIN_COLS - 1
DIFF_HEADS = 4
DIFF_QK_DIM = 64
DIFF_V_DIM = 128
RWKV_HEADS = 8
RWKV_HEAD_DIM = 64
RWKV_WIDTH = RWKV_HEADS * RWKV_HEAD_DIM
RWKV_DECAY_RANK = 64
RWKV_ICL_RANK = 64
RWKV_GATE_RANK = 128
RWKV_FEAT = 3 * RWKV_WIDTH + 2 * RWKV_DECAY_RANK + 2 * RWKV_ICL_RANK + RWKV_GATE_RANK
RWKV_GN_EPS = 64e-5
ATT_WIDTH = 6 * BRANCH_WIDTH
GATE_WIDTH = N_BRANCH * D_MODEL
FFN_HIDDEN = -(-8 * D_MODEL // (3 * 256)) * 256
ROPE_THETA = 10000.0
LN_EPS = 1e-5
ALPHA = (2.0 * DEPTH) ** 0.25

N_CTX = BATCH * SEQ
N_LAT = DEC_BATCH * DEC_SEQ
N_TOK = N_CTX + N_LAT
N_COND = 8
RWKV_CHUNK = 64
RWKV_SUPER = 256
RWKV_HEADS_PER_STEP = 4
RWKV_PASSES_PAIR = 1
RWKV_PASSES_SOLVE = 1
RWKV_PASSES_APPLY = 3
RWKV_PASSES_OUT = 1
RWKV_PASSES_STATE = 3
NEG_BIG = -1e30

NN = ((1,), (0,))
NT = ((1,), (1,))
TN = ((0,), (0,))


def _cparams(sem, vmem_mb=48):
    return pltpu.CompilerParams(dimension_semantics=sem, vmem_limit_bytes=vmem_mb * 1024 * 1024)


def _dot(a, b, dims=NN):
    return lax.dot_general(a, b, (dims, ((), ())), preferred_element_type=F32)


def _split2(x):
    hi = x.astype(BF16)
    lo = (x - hi.astype(F32)).astype(BF16)
    return hi, lo


def _dot3(a, b, dims=NN):
    ah, al = _split2(a)
    bh, bl = _split2(b)
    return _dot(ah, bh, dims) + (_dot(ah, bl, dims) + _dot(al, bh, dims))


def _dotp(a, b, dims, passes):
    if passes == 1:
        return _dot(a.astype(BF16), b.astype(BF16), dims)
    assert passes == 3
    return _dot3(a, b, dims)


def _dot_exact_lhs(a_bf16, b, dims=NN):
    b1 = b.astype(BF16)
    r1 = b - b1.astype(F32)
    b2 = r1.astype(BF16)
    b3 = (r1 - b2.astype(F32)).astype(BF16)
    return _dot(a_bf16, b1, dims) + (_dot(a_bf16, b2, dims) + _dot(a_bf16, b3, dims))


def _dot_exact_rhs(a, b_bf16, dims=NN):
    a1 = a.astype(BF16)
    r1 = a - a1.astype(F32)
    a2 = r1.astype(BF16)
    a3 = (r1 - a2.astype(F32)).astype(BF16)
    return _dot(a1, b_bf16, dims) + (_dot(a2, b_bf16, dims) + _dot(a3, b_bf16, dims))


def _normalize(x):
    mu = jnp.mean(x, axis=-1, keepdims=True)
    xc = x - mu
    var = jnp.mean(xc * xc, axis=-1, keepdims=True)
    return xc * lax.rsqrt(var + LN_EPS)


def _cond_of_row(row):
    return jnp.where(row < N_CTX, 0, 1 + (row - N_CTX) // DEC_SEQ)


def _adaln_kernel(c_ref, w_ref, b_ref, o_ref):
    c = c_ref[...]
    s = (c * jax.nn.sigmoid(c)).astype(BF16)
    o_ref[...] = _dot(s, w_ref[...].astype(BF16)) + b_ref[...]


def _adaln_all(cond, w_ada, b_ada):
    tn = 1024
    n = 6 * D_MODEL
    return pl.pallas_call(
        _adaln_kernel,
        grid=(DEPTH, n // tn),
        in_specs=[
            pl.BlockSpec((N_COND, D_MODEL), lambda l, j: (0, 0)),
            pl.BlockSpec((None, D_MODEL, tn), lambda l, j: (l, 0, j)),
            pl.BlockSpec((None, 1, tn), lambda l, j: (l, 0, j)),
        ],
        out_specs=pl.BlockSpec((None, N_COND, tn), lambda l, j: (l, 0, j)),
        out_shape=jax.ShapeDtypeStruct((DEPTH, N_COND, n), F32),
        compiler_params=_cparams(("parallel", "parallel")),
        name="adaln",
    )(cond, w_ada, b_ada.reshape(DEPTH, 1, n))


def _mod_spec(layer, which, tm, blk0=0):
    base = layer * 6 * N_COND + which * N_COND
    return pl.BlockSpec((None, 1, D_MODEL), lambda i: (base + _cond_of_row((blk0 + i) * tm), 0, 0))


def _modulate_kernel(x_ref, sh_ref, sc_ref, h_ref):
    h_ref[...] = (_normalize(x_ref[...]) * (1.0 + sc_ref[...]) + sh_ref[...]).astype(h_ref.dtype)


def _modulate(x, mods, layer):
    tm = 256
    return pl.pallas_call(
        _modulate_kernel,
        grid=(N_TOK // tm,),
        in_specs=[
            pl.BlockSpec((tm, D_MODEL), lambda i: (i, 0)),
            _mod_spec(layer, 0, tm),
            _mod_spec(layer, 1, tm),
        ],
        out_specs=pl.BlockSpec((tm, D_MODEL), lambda i: (i, 0)),
        out_shape=jax.ShapeDtypeStruct((N_TOK, D_MODEL), BF16),
        compiler_params=_cparams(("parallel",)),
        name="modulate",
    )(x, mods, mods)


def _res_ln_kernel(x_ref, z_ref, gate_ref, g_ref, b_ref, *rest, with_mod):
    y = ALPHA * x_ref[...] + gate_ref[...] * z_ref[...]
    xn = _normalize(y) * g_ref[...] + b_ref[...]
    if with_mod:
        sh_ref, sc_ref, xo_ref, h_ref = rest
        xo_ref[...] = xn
        h_ref[...] = (_normalize(xn) * (1.0 + sc_ref[...]) + sh_ref[...]).astype(h_ref.dtype)
    else:
        (xo_ref,) = rest
        xo_ref[...] = xn


def _proj_res_ln_kernel(m_ref, w_ref, x_ref, gate_ref, g_ref, b_ref, sh_ref, sc_ref, xo_ref, h_ref):
    n_sub = 4
    sub = m_ref.shape[0] // n_sub
    w = w_ref[...]
    def norm_rows(s, z):
        rows = slice(s * sub, (s + 1) * sub)
        y = ALPHA * x_ref[rows, :] + gate_ref[...] * z
        xn = _normalize(y) * g_ref[...] + b_ref[...]
        xo_ref[rows, :] = xn
        h_ref[rows, :] = (_normalize(xn) * (1.0 + sc_ref[...]) + sh_ref[...]).astype(h_ref.dtype)

    z_prev = _dot(m_ref[0:sub, :], w)
    for s in range(1, n_sub):
        z_next = _dot(m_ref[s * sub:(s + 1) * sub, :], w)
        norm_rows(s - 1, z_prev)
        z_prev = z_next
    norm_rows(n_sub - 1, z_prev)


def _proj_res_ln(m, w16, x, mods, layer, gate_idx, ln_g, ln_b, mod_idx):
    tm = 512
    vec = pl.BlockSpec((1, D_MODEL), lambda i: (0, 0))
    return pl.pallas_call(
        _proj_res_ln_kernel,
        grid=(N_TOK // tm,),
        in_specs=[
            pl.BlockSpec((tm, D_MODEL), lambda i: (i, 0)),
            pl.BlockSpec((None, D_MODEL, D_MODEL), lambda i: (layer, 0, 0)),
            pl.BlockSpec((tm, D_MODEL), lambda i: (i, 0)),
            _mod_spec(layer, gate_idx, tm), vec, vec,
            _mod_spec(layer, mod_idx, tm), _mod_spec(layer, mod_idx + 1, tm),
        ],
        out_specs=[pl.BlockSpec((tm, D_MODEL), lambda i: (i, 0))] * 2,
        out_shape=[jax.ShapeDtypeStruct((N_TOK, D_MODEL), F32), jax.ShapeDtypeStruct((N_TOK, D_MODEL), BF16)],
        compiler_params=_cparams(("parallel",), 60),
        name="w_out_res_ln",
    )(m, w16, x, mods, ln_g.reshape(1, D_MODEL), ln_b.reshape(1, D_MODEL), mods, mods)


def _res_ln(x, z, mods, layer, gate_idx, ln_g, ln_b, mod_layer, mod_idx, rows=(0, N_TOK)):
    tm = 256
    with_mod = mod_layer is not None
    blk0, n_rows = rows[0] // tm, rows[1]
    row_in = pl.BlockSpec((tm, D_MODEL), lambda i: (blk0 + i, 0))
    row = pl.BlockSpec((tm, D_MODEL), lambda i: (i, 0))
    vec = pl.BlockSpec((1, D_MODEL), lambda i: (0, 0))
    in_specs = [row_in, row_in, _mod_spec(layer, gate_idx, tm, blk0), vec, vec]
    args = [x, z, mods, ln_g.reshape(1, D_MODEL), ln_b.reshape(1, D_MODEL)]
    out_specs = [row]
    out_shape = [jax.ShapeDtypeStruct((n_rows, D_MODEL), F32)]
    if with_mod:
        in_specs += [_mod_spec(mod_layer, mod_idx, tm, blk0), _mod_spec(mod_layer, mod_idx + 1, tm, blk0)]
        args += [mods, mods]
        out_specs.append(row)
        out_shape.append(jax.ShapeDtypeStruct((n_rows, D_MODEL), BF16))
    out = pl.pallas_call(
        functools.partial(_res_ln_kernel, with_mod=with_mod),
        grid=(n_rows // tm,),
        in_specs=in_specs,
        out_specs=out_specs,
        out_shape=out_shape,
        compiler_params=_cparams(("parallel",)),
        name="res_ln",
    )(*args)
    return (out[0], out[1]) if with_mod else (out[0], None)


def _mm_kernel(a_ref, b_ref, o_ref, *, sigmoid):
    acc = _dot(a_ref[...], b_ref[...])
    if sigmoid:
        acc = jax.nn.sigmoid(acc)
    o_ref[...] = acc.astype(o_ref.dtype)


def _matmul(a, w, layer, out_dtype, col0=0, n=None, tm=1024, tn=1024, name="matmul", sigmoid=False, vmem_mb=48):
    m, k = a.shape
    n = w.shape[2] if n is None else n
    assert m % tm == 0 and n % tn == 0 and col0 % tn == 0
    j0 = col0 // tn
    return pl.pallas_call(
        functools.partial(_mm_kernel, sigmoid=sigmoid),
        grid=(m // tm, n // tn),
        in_specs=[
            pl.BlockSpec((tm, k), lambda i, j: (i, 0)),
            pl.BlockSpec((None, k, tn), lambda i, j: (layer, 0, j0 + j)),
        ],
        out_specs=pl.BlockSpec((tm, tn), lambda i, j: (i, j)),
        out_shape=jax.ShapeDtypeStruct((m, n), out_dtype),
        compiler_params=_cparams(("parallel", "parallel"), vmem_mb),
        name=name,
    )(a, w)


def _mm_w32_kernel(a_ref, w_ref, o_ref, w16_ref, *, sigmoid):
    @pl.when(pl.program_id(1) == 0)
    def _():
        w16_ref[...] = w_ref[0].astype(BF16)

    acc = _dot(a_ref[...], w16_ref[...])
    if sigmoid:
        acc = jax.nn.sigmoid(acc)
    o_ref[...] = acc.astype(o_ref.dtype)


def _matmul_w32(a, w, layer, out_dtype, col0=0, n=None, tm=1024, tn=1024, name="matmul", sigmoid=False):
    m, k = a.shape
    n = w.shape[2] if n is None else n
    assert m % tm == 0 and n % tn == 0
    return pl.pallas_call(
        functools.partial(_mm_w32_kernel, sigmoid=sigmoid),
        grid=(n // tn, m // tm),
        in_specs=[
            pl.BlockSpec((tm, k), lambda j, i: (i, 0)),
            pl.BlockSpec((pl.Element(1), pl.Element(k), pl.Element(tn)),
                         lambda j, i: (layer, 0, pl.multiple_of(col0 + j * tn, 128))),
        ],
        out_specs=pl.BlockSpec((tm, tn), lambda j, i: (i, j)),
        out_shape=jax.ShapeDtypeStruct((m, n), out_dtype),
        scratch_shapes=[pltpu.VMEM((k, tn), BF16)],
        compiler_params=_cparams(("parallel", "arbitrary")),
        name=name,
    )(a, w)


def _swiglu_kernel(a_ref, wg_ref, wu_ref, o_ref, wg16_ref, wu16_ref):
    @pl.when(pl.program_id(1) == 0)
    def _():
        wg16_ref[...] = wg_ref[...].astype(BF16)
        wu16_ref[...] = wu_ref[...].astype(BF16)

    a = a_ref[...]
    g = _dot(a, wg16_ref[...])
    u = _dot(a, wu16_ref[...])
    o_ref[...] = (g * jax.nn.sigmoid(g) * u).astype(o_ref.dtype)


def _ffn_in(h, w_ffn_in, layer):
    tm, tn = 1024, 512
    nb = FFN_HIDDEN // tn
    return pl.pallas_call(
        _swiglu_kernel,
        grid=(nb, N_TOK // tm),
        in_specs=[
            pl.BlockSpec((tm, D_MODEL), lambda j, i: (i, 0)),
            pl.BlockSpec((None, D_MODEL, tn), lambda j, i: (layer, 0, j)),
            pl.BlockSpec((None, D_MODEL, tn), lambda j, i: (layer, 0, j + nb)),
        ],
        out_specs=pl.BlockSpec((tm, tn), lambda j, i: (i, j)),
        out_shape=jax.ShapeDtypeStruct((N_TOK, FFN_HIDDEN), BF16),
        scratch_shapes=[pltpu.VMEM((D_MODEL, tn), BF16), pltpu.VMEM((D_MODEL, tn), BF16)],
        compiler_params=_cparams(("parallel", "arbitrary")),
        name="ffn_in",
    )(h, w_ffn_in, w_ffn_in)


def _merge_kernel(oa_ref, ob_ref, oc_ref, ga_ref, gb_ref, gc_ref, wb_ref, o_ref):
    acc = ga_ref[...].astype(F32) * _dot(oa_ref[...], wb_ref[0])
    acc += gb_ref[...].astype(F32) * _dot(ob_ref[...], wb_ref[1])
    acc += gc_ref[...].astype(F32) * _dot(oc_ref[...], wb_ref[2])
    o_ref[...] = acc.astype(o_ref.dtype)


def _merge(oa, ob, oc, gate_pre, w_branch, layer):
    tm, tn = 1024, 512
    nb = D_MODEL // tn
    o_spec = pl.BlockSpec((tm, BRANCH_WIDTH), lambda i, j: (i, 0))
    return pl.pallas_call(
        _merge_kernel,
        grid=(N_TOK // tm, nb),
        in_specs=[
            o_spec, o_spec, o_spec,
            pl.BlockSpec((tm, tn), lambda i, j: (i, j)),
            pl.BlockSpec((tm, tn), lambda i, j: (i, j + nb)),
            pl.BlockSpec((tm, tn), lambda i, j: (i, j + 2 * nb)),
            pl.BlockSpec((None, N_BRANCH, BRANCH_WIDTH, tn), lambda i, j: (layer, 0, 0, j)),
        ],
        out_specs=pl.BlockSpec((tm, tn), lambda i, j: (i, j)),
        out_shape=jax.ShapeDtypeStruct((N_TOK, D_MODEL), BF16),
        compiler_params=_cparams(("parallel", "parallel")),
        name="merge",
    )(oa, ob, oc, gate_pre, gate_pre, gate_pre, w_branch)


def _softmax_rows(s):
    m = jnp.max(s, axis=-1, keepdims=True)
    e = jnp.exp(s - m)
    return e * (1.0 / jnp.sum(e, axis=-1, keepdims=True))


def _na_ctx_kernel(q_ref, k_ref, v_ref, *rest):
    o_ref, new_k_ref, new_v_ref = rest[-3:]
    scale = NA_HEAD_DIM ** -0.5
    _store_heads(new_k_ref, k_ref)
    _store_heads(new_v_ref, v_ref)
    heads = range(NA_HEADS)
    sls = [slice(h * NA_HEAD_DIM, (h + 1) * NA_HEAD_DIM) for h in heads]
    s = [_dot(q_ref[:, sl].astype(BF16), k_ref[:, sl].astype(BF16), NT) * scale for sl in sls]
    p = [_softmax_rows(s[h]).astype(BF16) for h in heads]
    o = [_dot(p[h], v_ref[:, sls[h]].astype(BF16)) for h in heads]
    for h in heads:
        o_ref[:, sls[h]] = o[h].astype(o_ref.dtype)


_ANY_SPEC = pl.BlockSpec(memory_space=pl.ANY)


def _cache_stack_shape(split_heads):
    tail = (NA_HEADS, NA_HEAD_DIM) if split_heads else (BRANCH_WIDTH,)
    return jax.ShapeDtypeStruct((BATCH, DEPTH, SEQ) + tail, F32)


def _cache_stack_spec(split_heads, index_map):
    tail = (NA_HEADS, NA_HEAD_DIM) if split_heads else (BRANCH_WIDTH,)
    return pl.BlockSpec((None, None, SEQ) + tail, index_map)


def _store_heads(dst_ref, src_ref):
    for h in range(NA_HEADS):
        dst_ref[:, h, :] = src_ref[:, h * NA_HEAD_DIM:(h + 1) * NA_HEAD_DIM]


def _na_ctx(p_att, layer, stacks):
    w = BRANCH_WIDTH
    stack_spec = _cache_stack_spec(True, lambda n: (n, layer, 0, 0, 0))
    extra = [] if stacks is None else list(stacks)
    return pl.pallas_call(
        _na_ctx_kernel,
        grid=(BATCH,),
        in_specs=[
            pl.BlockSpec((SEQ, w), lambda n: (n, 0)),
            pl.BlockSpec((SEQ, w), lambda n: (n, 1)),
            pl.BlockSpec((SEQ, w), lambda n: (n, 2)),
        ] + [_ANY_SPEC] * len(extra),
        out_specs=[pl.BlockSpec((SEQ, w), lambda n: (n, 0)), stack_spec, stack_spec],
        out_shape=[jax.ShapeDtypeStruct((N_TOK, w), BF16), _cache_stack_shape(True), _cache_stack_shape(True)],
        input_output_aliases={3: 1, 4: 2} if extra else {},
        compiler_params=_cparams(("parallel",)),
        name="na_ctx",
    )(p_att, p_att, p_att, *extra)


def _na_row_start(r):
    return jnp.clip(r - NA_WIN_ROWS // 2, 0, GRID_ROWS - NA_WIN_ROWS)


def _na_lat_kernel(q_ref, k_ref, v_ref, kc_ref, vc_ref, bias_ref, _, o_ref):
    scale = NA_HEAD_DIM ** -0.5
    n_loc = NA_WIN_ROWS * GRID_W
    start = pl.multiple_of(_na_row_start(pl.program_id(1)) * GRID_W, GRID_W)
    k_loc = k_ref[pl.ds(start, n_loc), :].astype(BF16)
    v_loc = v_ref[pl.ds(start, n_loc), :].astype(BF16)
    heads = range(NA_HEADS)
    sls = [slice(h * NA_HEAD_DIM, (h + 1) * NA_HEAD_DIM) for h in heads]
    q = [q_ref[:, sl].astype(BF16) for sl in sls]
    s_loc = [_dot(q[h], k_loc[:, sls[h]], NT) * scale + bias_ref[h] for h in heads]
    s_ctx = [_dot(q[h], kc_ref[:, sls[h]].astype(BF16), NT) * scale for h in heads]
    m = [jnp.maximum(jnp.max(s_loc[h], axis=-1, keepdims=True), jnp.max(s_ctx[h], axis=-1, keepdims=True))
         for h in heads]
    e_loc = [jnp.exp(s_loc[h] - m[h]) for h in heads]
    e_ctx = [jnp.exp(s_ctx[h] - m[h]) for h in heads]
    inv = [1.0 / (jnp.sum(e_loc[h], axis=-1, keepdims=True) + jnp.sum(e_ctx[h], axis=-1, keepdims=True))
           for h in heads]
    o_loc = [_dot((e_loc[h] * inv[h]).astype(BF16), v_loc[:, sls[h]]) for h in heads]
    o_ctx = [_dot((e_ctx[h] * inv[h]).astype(BF16), vc_ref[:, sls[h]].astype(BF16)) for h in heads]
    for h in heads:
        o_ref[:, sls[h]] = (o_loc[h] + o_ctx[h]).astype(o_ref.dtype)


def _na_bias_kernel(rows_ref, onehot_ref, mask_ref, o_ref):
    o_ref[...] = _dot_exact_rhs(rows_ref[...], onehot_ref[...]) + mask_ref[...]


def _na_bias_table(rpb):
    nr, nd = NA_WIN_ROWS, NA_REL_COLS + 1
    rows = jnp.stack([rpb[:, nr - 1 - p:2 * nr - 1 - p, :] for p in range(nr)], axis=1)
    rows = jnp.pad(rows.astype(F32), ((0, 0), (0, 0), (0, 0), (0, nd - NA_REL_COLS)))
    qc = np.arange(GRID_W)[:, None]
    kc = np.arange(GRID_W)[None, :]
    rel_c = np.clip(kc - qc, -(NA_WIN_COLS - 1), NA_WIN_COLS - 1) + NA_WIN_COLS - 1
    onehot = (rel_c[None] == np.arange(nd)[:, None, None]).reshape(nd, GRID_W * GRID_W)
    win_c0 = np.clip(qc - NA_WIN_COLS // 2, 0, GRID_W - NA_WIN_COLS)
    valid = ((kc >= win_c0) & (kc < win_c0 + NA_WIN_COLS)).reshape(1, GRID_W * GRID_W)
    n_rows = NA_HEADS * nr * nr
    full = lambda shape: pl.BlockSpec(shape, lambda: (0,) * len(shape))
    bias = pl.pallas_call(
        _na_bias_kernel,
        in_specs=[full((n_rows, nd)), full((nd, GRID_W * GRID_W)), full((1, GRID_W * GRID_W))],
        out_specs=full((n_rows, GRID_W * GRID_W)),
        out_shape=jax.ShapeDtypeStruct((n_rows, GRID_W * GRID_W), F32),
        name="na_bias",
    )(rows.reshape(n_rows, nd), jnp.asarray(onehot, BF16), jnp.asarray(np.where(valid, 0.0, NEG_BIG), F32))
    bias = bias.reshape(NA_HEADS, nr, nr, GRID_W, GRID_W).transpose(0, 1, 3, 2, 4)
    return bias.reshape(NA_HEADS, nr, GRID_W, nr * GRID_W)


def _na_lat(p_att, cache_k, cache_v, layer, bias, oa):
    w = BRANCH_WIDTH
    blk0 = N_CTX // DEC_SEQ
    q0 = N_CTX // GRID_W
    rows_per = DEC_SEQ // GRID_W
    kv_cache = pl.BlockSpec((None, None, PAST_LEN, w), lambda n, r: (n, layer, 0, 0))
    return pl.pallas_call(
        _na_lat_kernel,
        grid=(DEC_BATCH, GRID_ROWS),
        in_specs=[
            pl.BlockSpec((GRID_W, w), lambda n, r: (q0 + n * rows_per + r, 0)),
            pl.BlockSpec((DEC_SEQ, w), lambda n, r: (blk0 + n, 1)),
            pl.BlockSpec((DEC_SEQ, w), lambda n, r: (blk0 + n, 2)),
            kv_cache, kv_cache,
            pl.BlockSpec((NA_HEADS, None, GRID_W, NA_WIN_ROWS * GRID_W),
                         lambda n, r: (0, r - _na_row_start(r), 0, 0)),
            _ANY_SPEC,
        ],
        out_specs=pl.BlockSpec((GRID_W, w), lambda n, r: (q0 + n * rows_per + r, 0)),
        out_shape=jax.ShapeDtypeStruct((N_TOK, w), BF16),
        input_output_aliases={6: 0},
        compiler_params=_cparams(("parallel", "arbitrary")),
        name="na_lat",
    )(p_att, p_att, p_att, cache_k, cache_v, bias, oa)


def _rope(x, cos, sin):
    lane = lax.broadcasted_iota(jnp.int32, x.shape, 1)
    first = (lane % 32) < 16
    rot = jnp.where(first, -pltpu.roll(x, 128 - 16, 1), pltpu.roll(x, 16, 1))
    return x * cos + rot * sin


def _diff_kernel(*refs, rope, cache, lam_init, tq):
    if cache:
        q_ref, k_ref, v_ref, kc_ref, vc_ref, cos_ref, sin_ref, lam_ref, g_ref, _, o_ref = refs
    else:
        q_ref, k_ref, v_ref, lam_ref, g_ref = refs[:5]
        o_ref, new_k_ref, new_v_ref = refs[-3:]
        new_k_ref[...] = k_ref[...]
        _store_heads(new_v_ref, v_ref)
    scale = DIFF_QK_DIM ** -0.5
    lp = lam_ref[...]
    lam = (jnp.exp(jnp.sum(lp[0:1] * lp[1:2], axis=-1, keepdims=True))
           - jnp.exp(jnp.sum(lp[2:3] * lp[3:4], axis=-1, keepdims=True)) + lam_init)
    hd = DIFF_V_DIM
    t = q_ref.shape[0]
    n_heads = q_ref.shape[1] // hd
    ks, vs = [], []
    for hh in range(n_heads):
        sl = slice(hh * hd, (hh + 1) * hd)
        k = k_ref[:, sl]
        v = v_ref[:, sl]
        if rope:
            k = _rope(k, cos_ref[...], sin_ref[...])
        if cache:
            k = jnp.concatenate([k, kc_ref[:, sl]], axis=0)
            v = jnp.concatenate([v, vc_ref[:, sl]], axis=0)
        ks.append(k.astype(BF16))
        vs.append(v.astype(BF16))
    units = [(hh, i) for hh in range(n_heads) for i in range(t // tq)]
    rows = {u: slice(u[1] * tq, (u[1] + 1) * tq) for u in units}
    cols = {u: slice(u[0] * hd, (u[0] + 1) * hd) for u in units}
    q = {u: q_ref[rows[u], cols[u]] for u in units}
    if rope:
        q = {u: _rope(q[u], cos_ref[rows[u], :], sin_ref[rows[u], :]) for u in units}
    lane = lax.broadcasted_iota(jnp.int32, (tq, hd), 1)
    s1 = {u: _dot(jnp.where(lane < DIFF_QK_DIM, q[u], 0.0).astype(BF16), ks[u[0]], NT) * scale for u in units}
    s2 = {u: _dot(jnp.where(lane >= DIFF_QK_DIM, q[u], 0.0).astype(BF16), ks[u[0]], NT) * scale for u in units}
    w = {u: (_softmax_rows(s1[u]) - lam * _softmax_rows(s2[u])).astype(BF16) for u in units}
    o = {u: _dot(w[u], vs[u[0]]) for u in units}
    for u in units:
        on = o[u] * lax.rsqrt(jnp.mean(o[u] * o[u], axis=-1, keepdims=True) + LN_EPS)
        o_ref[rows[u], cols[u]] = (on * g_ref[:, cols[u]] * (1.0 - lam_init)).astype(o_ref.dtype)


def _diff_attn(p_att, lam_p, subln, lam_init, layer, *, t, n_seq, blk0, cache=None, tables=None, stacks=None, ob=None):
    hd = DIFF_V_DIM * (1 if cache is not None else DIFF_HEADS)
    q_col, k_col, v_col = (z * BRANCH_WIDTH // hd for z in (3, 4, 5))
    in_specs = [
        pl.BlockSpec((t, hd), lambda n, h: (blk0 + n, q_col + h)),
        pl.BlockSpec((t, hd), lambda n, h: (blk0 + n, k_col + h)),
        pl.BlockSpec((t, hd), lambda n, h: (blk0 + n, v_col + h)),
    ]
    args = [p_att, p_att, p_att]
    if cache is not None:
        cache_k, cache_v = cache
        tab = pl.BlockSpec((t, hd), lambda n, h: (0, 0))
        cache_spec = pl.BlockSpec((None, None, PAST_LEN, hd), lambda n, h: (n, layer, 0, h))
        in_specs += [cache_spec, cache_spec, tab, tab]
        args += [cache_k, cache_v, tables[0], tables[1]]
    in_specs += [
        pl.BlockSpec((4, DIFF_QK_DIM), lambda n, h: (0, 0)),
        pl.BlockSpec((1, hd), lambda n, h: (0, h)),
    ]
    args += [lam_p, subln.reshape(1, BRANCH_WIDTH)]
    out_specs = [pl.BlockSpec((t, hd), lambda n, h: (blk0 + n, h))]
    out_shape = [jax.ShapeDtypeStruct((N_TOK, BRANCH_WIDTH), BF16)]
    aliases = {}
    if cache is not None:
        aliases = {len(args): 0}
        in_specs.append(_ANY_SPEC)
        args.append(ob)
    else:
        out_specs += [_cache_stack_spec(False, lambda n, h: (n, layer, 0, 0)),
                      _cache_stack_spec(True, lambda n, h: (n, layer, 0, 0, 0))]
        out_shape += [_cache_stack_shape(False), _cache_stack_shape(True)]
        if stacks is not None:
            aliases = {len(args): 1, len(args) + 1: 2}
            in_specs += [_ANY_SPEC, _ANY_SPEC]
            args += list(stacks)
    return pl.pallas_call(
        functools.partial(_diff_kernel, rope=cache is not None, cache=cache is not None,
                          lam_init=lam_init, tq=256),
        grid=(n_seq, BRANCH_WIDTH // hd),
        in_specs=in_specs,
        out_specs=out_specs,
        out_shape=out_shape,
        input_output_aliases=aliases,
        compiler_params=_cparams(("parallel", "parallel")),
        name="diff_lat" if cache is not None else "diff_ctx",
    )(*args)


def _rope_tables():
    t = np.arange(DEC_SEQ)
    rows = (t // GRID_W).astype(np.float32)
    cols = (t % GRID_W).astype(np.float32)
    half = DIFF_QK_DIM // 2
    inv = jnp.asarray(ROPE_THETA, F32) ** (-jnp.arange(0, half, 2, dtype=F32) / half)
    ang_r = jnp.asarray(rows)[:, None] * inv
    ang_c = jnp.asarray(cols)[:, None] * inv
    ang = jnp.concatenate([ang_r, ang_r, ang_c, ang_c] * 2, axis=-1)
    return jnp.cos(ang), jnp.sin(ang)


def _head_sum(x, ones_bd):
    return _dot_exact_rhs(x, ones_bd)


def _rwkv_prep_kernel(cur_ref, prev_ref, next_ref, mix_ref, w0_ref, w2_ref, a0_ref, a2_ref, g2_ref,
                      kk_ref, ka_ref, rk_ref, ones_ref,
                      r_ref, v_ref, kkn_ref, lwf_ref, lwb_ref, kf_ref, kb_ref, bf_ref, bb_ref, bonus_ref, g_ref,
                      *, tm):
    i = pl.program_id(0)
    blocks_per_seq = DEC_SEQ // tm
    j = i - N_CTX // tm
    is_ctx = i < N_CTX // tm
    at_start = jnp.logical_or(is_ctx, j % blocks_per_seq == 0)
    at_end = jnp.logical_or(is_ctx, j % blocks_per_seq == blocks_per_seq - 1)
    cur = cur_ref[...]
    row = lax.broadcasted_iota(jnp.int32, cur.shape, 0)
    prev_row = jnp.where(at_start, 0.0, prev_ref[7:8, :])
    next_row = jnp.where(at_end, 0.0, next_ref[0:1, :])
    prev = jnp.where(row == 0, prev_row, pltpu.roll(cur, 1, 0))
    nxt = jnp.where(row == tm - 1, next_row, pltpu.roll(cur, tm - 1, 0))
    f = cur + mix_ref[0:1, :] * (prev - cur) + mix_ref[1:2, :] * (nxt - cur)

    w = RWKV_WIDTH
    r = f[:, 0:w]
    k = f[:, w:2 * w]
    v = f[:, 2 * w:3 * w]
    c0 = 3 * w
    wd = f[:, c0:c0 + 2 * RWKV_DECAY_RANK]
    ad = f[:, c0 + 2 * RWKV_DECAY_RANK:c0 + 2 * RWKV_DECAY_RANK + 2 * RWKV_ICL_RANK]
    gd = f[:, c0 + 2 * RWKV_DECAY_RANK + 2 * RWKV_ICL_RANK:]
    ones_bd = ones_ref[...]

    kkv = k * kk_ref[...]
    norm = jnp.sqrt(_head_sum(kkv * kkv, ones_bd))
    kkn = kkv / jnp.maximum(norm, 1e-12)
    wlin = _dot(jnp.tanh(wd).astype(BF16), w2_ref[...]) + w0_ref[...]
    alin = _dot(ad.astype(BF16), a2_ref[...]) + a0_ref[...]
    g = _dot(jax.nn.sigmoid(gd).astype(BF16), g2_ref[...])

    r_ref[...] = r
    v_ref[...] = v
    kkn_ref[...] = kkn
    g_ref[...] = g
    bonus = jnp.zeros_like(v)
    for d, (lw_ref, kd_ref, bd_ref) in enumerate(((lwf_ref, kf_ref, bf_ref), (lwb_ref, kb_ref, bb_ref))):
        z = -wlin[:, d * w:(d + 1) * w]
        softplus = jnp.maximum(z, 0.0) + jnp.log(1.0 + jnp.exp(-jnp.abs(z)))
        lw_ref[...] = -jnp.exp(-softplus - 0.5)
        a = jax.nn.sigmoid(alin[:, d * w:(d + 1) * w])
        k_d = k * (1.0 + (a - 1.0) * ka_ref[...])
        kd_ref[...] = k_d
        bd_ref[...] = kkn * a
        bonus += _head_sum(r * k_d * rk_ref[...], ones_bd) * v
    bonus_ref[...] = bonus


def _block_diag2(a, b):
    za = jnp.zeros((a.shape[0], b.shape[1]), a.dtype)
    zb = jnp.zeros((b.shape[0], a.shape[1]), b.dtype)
    return jnp.concatenate([jnp.concatenate([a, za], axis=1), jnp.concatenate([zb, b], axis=1)], axis=0)


def _rwkv_prep(p_rw, lp):
    tm = 256
    w = RWKV_WIDTH
    sub = tm // 8
    full = lambda shape: pl.BlockSpec(shape, lambda i: (0,) * len(shape))
    out_spec = pl.BlockSpec((tm, w), lambda i: (i, 0))
    ones_bd = jnp.asarray(np.kron(np.eye(RWKV_HEADS), np.ones((RWKV_HEAD_DIM, RWKV_HEAD_DIM))), BF16)
    w2_bd = _block_diag2(lp['rwkv_w2'][0], lp['rwkv_w2'][1]).astype(BF16)
    a2_bd = _block_diag2(lp['rwkv_a2'][0], lp['rwkv_a2'][1]).astype(BF16)
    n_out = 11
    return pl.pallas_call(
        functools.partial(_rwkv_prep_kernel, tm=tm),
        grid=(N_TOK // tm,),
        in_specs=[
            pl.BlockSpec((tm, RWKV_FEAT), lambda i: (i, 0)),
            pl.BlockSpec((8, RWKV_FEAT), lambda i: (jnp.maximum(i * sub - 1, 0), 0)),
            pl.BlockSpec((8, RWKV_FEAT), lambda i: (jnp.minimum((i + 1) * sub, N_TOK // 8 - 1), 0)),
            full((2, RWKV_FEAT)),
            full((1, 2 * w)), full((2 * RWKV_DECAY_RANK, 2 * w)),
            full((1, 2 * w)), full((2 * RWKV_ICL_RANK, 2 * w)),
            full((RWKV_GATE_RANK, w)),
            full((1, w)), full((1, w)), full((1, w)),
            full((w, w)),
        ],
        out_specs=[out_spec] * n_out,
        out_shape=[jax.ShapeDtypeStruct((N_TOK, w), F32)] * n_out,
        compiler_params=_cparams(("parallel",)),
        name="rwkv_prep",
    )(p_rw, p_rw, p_rw, lp['rwkv_mix'],
      lp['rwkv_w0'].reshape(1, 2 * w), w2_bd, lp['rwkv_a0'].reshape(1, 2 * w), a2_bd,
      lp['rwkv_g2'].astype(BF16),
      lp['rwkv_kk'].reshape(1, w), lp['rwkv_ka'].reshape(1, w), lp['rwkv_rk'].reshape(1, w),
      ones_bd)


def _rwkv_chunk_maps(chains):
    n = len(chains)
    rs, ks, vs, kkns, bs, lws, fwds = (list(z) for z in zip(*chains))
    c, nk = rs[0].shape
    row = lax.broadcasted_iota(jnp.int32, (c, c), 0)
    col = lax.broadcasted_iota(jnp.int32, (c, c), 1)
    eye = row == col
    incl_d = {True: row >= col, False: row <= col}
    strict_d = {True: row > col, False: row < col}
    tri_d = {f: jnp.where(m, 1.0, 0.0).astype(BF16) for f, m in incl_d.items()}
    idx = range(n)
    cum = [_dot_exact_lhs(tri_d[fwds[i]], lws[i]) for i in idx]
    tot = [cum[i][c - 1:c, :] if fwds[i] else cum[i][0:1, :] for i in idx]
    e_neg = [jnp.exp(-cum[i]) for i in idx]
    a_t = [-kkns[i] * jnp.exp(cum[i] - lws[i]) for i in idx]
    r_t = [rs[i] * jnp.exp(cum[i]) for i in idx]
    ar = [jnp.concatenate([a_t[i], r_t[i]], axis=0) for i in idx]
    with_b = [_dotp(ar[i], bs[i] * e_neg[i], NT, RWKV_PASSES_PAIR) for i in idx]
    with_k = [_dotp(ar[i], ks[i] * e_neg[i], NT, RWKV_PASSES_PAIR) for i in idx]
    l_ab = [jnp.where(strict_d[fwds[i]], with_b[i][:c], 0.0) for i in idx]
    l_ak = [jnp.where(strict_d[fwds[i]], with_k[i][:c], 0.0) for i in idx]
    m_rb = [jnp.where(incl_d[fwds[i]], with_b[i][c:], 0.0) for i in idx]
    m_rk = [jnp.where(incl_d[fwds[i]], with_k[i][c:], 0.0) for i in idx]
    lakv = [_dotp(l_ak[i], vs[i], NN, RWKV_PASSES_APPLY) for i in idx]
    mrkv = [_dotp(m_rk[i], vs[i], NN, RWKV_PASSES_OUT) for i in idx]
    e_tail = [jnp.exp(tot[i] - cum[i]) for i in idx]
    kwv = [_dotp(ks[i] * e_tail[i], vs[i], TN, RWKV_PASSES_APPLY) for i in idx]
    same = lambda s: (row // s) == (col // s)
    inv = [jnp.where(eye, 1.0, 0.0) + jnp.where(same(2), l_ab[i], 0.0) for i in idx]
    size = 4
    while size <= c:
        part = jnp.logical_and(same(size), jnp.logical_not(same(size // 2)))
        half = [_dotp(inv[i], jnp.where(part, l_ab[i], 0.0), NN, RWKV_PASSES_SOLVE) for i in idx]
        inv = [inv[i] + _dotp(half[i], inv[i], NN, RWKV_PASSES_SOLVE) for i in idx]
        size *= 2
    pq1 = [_dotp(inv[i], jnp.concatenate([a_t[i], lakv[i]], axis=1), NN, RWKV_PASSES_APPLY) for i in idx]
    rb = [_dotp(m_rb[i], pq1[i], NN, RWKV_PASSES_OUT) for i in idx]
    sb = [_dotp(bs[i] * e_tail[i], pq1[i], TN, RWKV_PASSES_APPLY) for i in idx]
    out = []
    for i in idx:
        p2 = r_t[i] + rb[i][:, :nk]
        q2 = rb[i][:, nk:] + mrkv[i]
        p3 = jnp.where(eye, jnp.exp(tot[i]), 0.0) + sb[i][:, :nk]
        q3 = sb[i][:, nk:] + kwv[i]
        out.append((p2, p3, q2, q3))
    return out


def _rwkv_scan_kernel(rf, vf, af, wf, kf, bf, rb, vb, ab, wb, kb, bb, s0f, s0b, *rest, n_super, n_chunk):
    yf, yb, sff, sfb, st_ref = rest[-5:]
    sup = pl.program_id(2)
    hd = RWKV_HEAD_DIM
    cs = RWKV_CHUNK
    hp = RWKV_HEADS_PER_STEP

    @pl.when(sup == 0)
    def _():
        st_ref[0:hp] = s0f[...]
        st_ref[hp:2 * hp] = s0b[...]

    dirs = ((rf, vf, af, wf, kf, bf, yf, True), (rb, vb, ab, wb, kb, bb, yb, False))
    keys, chains = [], []
    for d, (r_ref, v_ref, a_ref, w_ref, k_ref, b_ref, _, forward) in enumerate(dirs):
        for ci in range(n_chunk):
            rows = slice(ci * cs, (ci + 1) * cs)
            for hh in range(hp):
                sl = slice(hh * hd, (hh + 1) * hd)
                keys.append((d, ci, hh))
                chains.append((r_ref[rows, sl], k_ref[rows, sl], v_ref[rows, sl],
                               a_ref[rows, sl], b_ref[rows, sl], w_ref[rows, sl], forward))
    maps = dict(zip(keys, _rwkv_chunk_maps(chains)))

    seqs = [(d, hh) for d in range(2) for hh in range(hp)]
    st = {s: st_ref[hp * s[0] + s[1]] for s in seqs}
    ys = {}
    for step in range(n_chunk):
        for d, hh in seqs:
            ci = step if dirs[d][-1] else n_chunk - 1 - step
            p2, p3, q2, q3 = maps[d, ci, hh]
            ys[d, ci, hh] = _dotp(p2, st[d, hh], NN, RWKV_PASSES_OUT) + q2
            st[d, hh] = _dotp(p3, st[d, hh], NN, RWKV_PASSES_STATE) + q3
    for d, hh in seqs:
        st_ref[hp * d + hh] = st[d, hh]
    for d in range(2):
        y_ref = dirs[d][6]
        for ci in range(n_chunk):
            y_ref[ci * cs:(ci + 1) * cs, :] = jnp.concatenate([ys[d, ci, hh] for hh in range(hp)], axis=1)

    @pl.when(sup == n_super - 1)
    def _():
        sff[...] = st_ref[0:hp]
        sfb[...] = st_ref[hp:2 * hp]


def _rwkv_scan(feats, s0f_t, s0b_t, *, t, n_seq, row0, y_prev=None):
    r, v, kkn, lwf, lwb, kf, kb, bf, bb = feats
    rows = RWKV_SUPER
    n_super = t // rows
    blk0 = row0 // rows
    hp = RWKV_HEADS_PER_STEP
    width = hp * RWKV_HEAD_DIM
    fwd = pl.BlockSpec((rows, width), lambda n, p, s: (blk0 + n * n_super + s, p))
    bwd = pl.BlockSpec((rows, width), lambda n, p, s: (blk0 + n * n_super + n_super - 1 - s, p))
    st_spec = pl.BlockSpec((None, hp, RWKV_HEAD_DIM, RWKV_HEAD_DIM), lambda n, p, s: (n, p, 0, 0))
    st_shape = jax.ShapeDtypeStruct((n_seq, RWKV_HEADS, RWKV_HEAD_DIM, RWKV_HEAD_DIM), F32)
    y_shape = jax.ShapeDtypeStruct((N_TOK, RWKV_WIDTH), F32)
    extra = [] if y_prev is None else list(y_prev)
    n_in = 14
    return pl.pallas_call(
        functools.partial(_rwkv_scan_kernel, n_super=n_super, n_chunk=rows // RWKV_CHUNK),
        grid=(n_seq, RWKV_HEADS // hp, n_super),
        in_specs=[fwd] * 6 + [bwd] * 6 + [st_spec, st_spec] + [_ANY_SPEC] * len(extra),
        out_specs=[fwd, bwd, st_spec, st_spec],
        out_shape=[y_shape, y_shape, st_shape, st_shape],
        input_output_aliases={n_in: 0, n_in + 1: 1} if extra else {},
        scratch_shapes=[pltpu.VMEM((2 * hp, RWKV_HEAD_DIM, RWKV_HEAD_DIM), F32)],
        compiler_params=_cparams(("parallel", "parallel", "arbitrary")),
        name="rwkv_scan",
    )(r, v, kkn, lwf, kf, bf, r, v, kkn, lwb, kb, bb, s0f_t, s0b_t, *extra)


def _rwkv_mixer(p_rw, s_lat_f, s_lat_b, lp):
    r, v, kkn, lwf, lwb, kf, kb, bf, bb, bonus, g = _rwkv_prep(p_rw, lp)
    feats = (r, v, kkn, lwf, lwb, kf, kb, bf, bb)
    s_zero = jnp.zeros((BATCH, RWKV_HEADS, RWKV_HEAD_DIM, RWKV_HEAD_DIM), F32)
    y_f, y_b, sf_c, sb_c = _rwkv_scan(feats, s_zero, s_zero, t=SEQ, n_seq=BATCH, row0=0)
    y_f, y_b, _, _ = _rwkv_scan(feats, jnp.swapaxes(s_lat_f, -1, -2), jnp.swapaxes(s_lat_b, -1, -2),
                                t=DEC_SEQ, n_seq=DEC_BATCH, row0=N_CTX, y_prev=(y_f, y_b))
    oc = _rwkv_out(y_f, y_b, bonus, g, lp['rwkv_lnx_g'], lp['rwkv_lnx_b'])
    return oc, jnp.swapaxes(sf_c, -1, -2), jnp.swapaxes(sb_c, -1, -2)


def _rwkv_out_kernel(yf_ref, yb_ref, bonus_ref, g_ref, lg_ref, lb_ref, ones_ref, o_ref):
    ones_bd = ones_ref[...]
    y = yf_ref[...] + yb_ref[...]
    mu = _head_sum(y, ones_bd) * (1.0 / RWKV_HEAD_DIM)
    yc = y - mu
    var = _head_sum(yc * yc, ones_bd) * (1.0 / RWKV_HEAD_DIM)
    yn = yc * lax.rsqrt(var + RWKV_GN_EPS) * lg_ref[...] + lb_ref[...]
    o_ref[...] = ((yn + bonus_ref[...]) * g_ref[...]).astype(o_ref.dtype)


def _rwkv_out(y_f, y_b, bonus, g, lnx_g, lnx_b):
    tm = 512
    w = RWKV_WIDTH
    row = pl.BlockSpec((tm, w), lambda i: (i, 0))
    vec = pl.BlockSpec((1, w), lambda i: (0, 0))
    ones_bd = jnp.asarray(np.kron(np.eye(RWKV_HEADS), np.ones((RWKV_HEAD_DIM, RWKV_HEAD_DIM))), BF16)
    return pl.pallas_call(
        _rwkv_out_kernel,
        grid=(N_TOK // tm,),
        in_specs=[row, row, row, row, vec, vec, pl.BlockSpec((w, w), lambda i: (0, 0))],
        out_specs=row,
        out_shape=jax.ShapeDtypeStruct((N_TOK, w), BF16),
        compiler_params=_cparams(("parallel",)),
        name="rwkv_out",
    )(y_f, y_b, bonus, g, lnx_g.reshape(1, w), lnx_b.reshape(1, w), ones_bd)


def kernel(x_prompt, x_sample, cache_na_k, cache_na_v, cache_diff_k, cache_diff_v, state_rwkv_fwd, state_rwkv_bwd, c, c_ctx, w_ada, b_ada, w_in, na_rpb, diff_lambda, diff_subln, rwkv_mix, rwkv_w0, rwkv_w2, rwkv_a0, rwkv_a2, rwkv_g2, rwkv_kk, rwkv_ka, rwkv_rk, rwkv_lnx_g, rwkv_lnx_b, w_branch, w_out, ln1_g, ln1_b, w_ffn_in, w_ffn_out, ln2_g, ln2_b):
    x = jnp.concatenate([x_prompt.reshape(N_CTX, D_MODEL), x_sample.reshape(N_LAT, D_MODEL)], axis=0)
    cond = jnp.concatenate([c_ctx[None, :], c, jnp.zeros((N_COND - 1 - DEC_BATCH, D_MODEL), F32)], axis=0)
    mods = _adaln_all(cond, w_ada, b_ada)
    mods = mods.reshape(DEPTH, N_COND, 6, D_MODEL).transpose(0, 2, 1, 3).reshape(DEPTH * 6 * N_COND, 1, D_MODEL)

    cache_na_k = cache_na_k.reshape(DEC_BATCH, DEPTH, PAST_LEN, BRANCH_WIDTH)
    cache_na_v = cache_na_v.reshape(DEC_BATCH, DEPTH, PAST_LEN, BRANCH_WIDTH)
    cache_diff_k = cache_diff_k.reshape(DEC_BATCH, DEPTH, PAST_LEN, BRANCH_WIDTH)
    cache_diff_v = cache_diff_v.reshape(DEC_BATCH, DEPTH, PAST_LEN, BRANCH_WIDTH)
    rope_tables = _rope_tables()

    c1 = ATT_WIDTH + RWKV_FEAT
    w_branch16 = w_branch.astype(BF16)
    w_out16 = w_out.astype(BF16)
    w_ffn_out16 = w_ffn_out.astype(BF16)

    h = _modulate(x, mods, 0)
    na_stacks, diff_stacks, new_f, new_b = None, None, [], []
    for l in range(DEPTH):
        lam_init = 0.8 - 0.6 * math.exp(-0.3 * l)
        lp = {'rwkv_mix': rwkv_mix[l], 'rwkv_w0': rwkv_w0[l], 'rwkv_w2': rwkv_w2[l], 'rwkv_a0': rwkv_a0[l],
              'rwkv_a2': rwkv_a2[l], 'rwkv_g2': rwkv_g2[l], 'rwkv_kk': rwkv_kk[l], 'rwkv_ka': rwkv_ka[l],
              'rwkv_rk': rwkv_rk[l], 'rwkv_lnx_g': rwkv_lnx_g[l], 'rwkv_lnx_b': rwkv_lnx_b[l]}
        p_att = _matmul_w32(h, w_in, l, F32, col0=0, n=ATT_WIDTH, name="in_att")
        p_rw = _matmul_w32(h, w_in, l, F32, col0=ATT_WIDTH, n=RWKV_FEAT, tn=RWKV_FEAT // 3, name="in_rwkv")
        gates = _matmul_w32(h, w_in, l, BF16, col0=c1, n=GATE_WIDTH, name="in_gate", sigmoid=True)

        oa, *na_stacks = _na_ctx(p_att, l, na_stacks)
        oa = _na_lat(p_att, cache_na_k, cache_na_v, l, _na_bias_table(na_rpb[l]), oa)
        ob, *diff_stacks = _diff_attn(p_att, diff_lambda[l], diff_subln[l], lam_init, l, t=SEQ, n_seq=BATCH, blk0=0,
                                      stacks=diff_stacks)
        ob, = _diff_attn(p_att, diff_lambda[l], diff_subln[l], lam_init, l, t=DEC_SEQ, n_seq=DEC_BATCH,
                         blk0=N_CTX // DEC_SEQ, cache=(cache_diff_k, cache_diff_v), tables=rope_tables, ob=ob)
        oc, sf_c, sb_c = _rwkv_mixer(p_rw, state_rwkv_fwd[:, l], state_rwkv_bwd[:, l], lp)

        merged = _merge(oa, ob, oc, gates, w_branch16, l)
        x, h2 = _proj_res_ln(merged, w_out16, x, mods, l, 2, ln1_g[l], ln1_b[l], 3)
        hid = _ffn_in(h2, w_ffn_in, l)
        ff = _matmul(hid, w_ffn_out16, l, F32, tn=512, name="ffn_out", vmem_mb=56)
        new_f.append(sf_c)
        new_b.append(sb_c)
        if l + 1 < DEPTH:
            x, h = _res_ln(x, ff, mods, l, 5, ln2_g[l], ln2_b[l], l + 1, 0)
        else:
            y_ctx, _ = _res_ln(x, ff, mods, l, 5, ln2_g[l], ln2_b[l], None, 0, rows=(0, N_CTX))
            y_lat, _ = _res_ln(x, ff, mods, l, 5, ln2_g[l], ln2_b[l], None, 0, rows=(N_CTX, N_LAT))

    y_prompt = y_ctx.reshape(BATCH, SEQ, D_MODEL)
    y_sample = y_lat.reshape(DEC_BATCH, DEC_SEQ, D_MODEL)
    new_na_k, new_na_v = na_stacks
    new_diff_k = diff_stacks[0].reshape(BATCH, DEPTH, SEQ, DIFF_HEADS, 2, DIFF_QK_DIM)
    new_diff_v = diff_stacks[1]
    return (y_prompt, y_sample, new_na_k, new_na_v, new_diff_k, new_diff_v,
            jnp.stack(new_f, axis=1), jnp.stack(new_b, axis=1))
```

```python
import functools
import math

import numpy as np
import jax
import jax.numpy as jnp
from jax import lax
from jax.experimental import pallas as pl
from jax.experimental.pallas import tpu as pltpu

F32 = jnp.float32
BF16 = jnp.bfloat16

D_MODEL = 2048
BATCH = 16
SEQ = 256
DEPTH = 4
DEC_BATCH = 4
DEC_SEQ = 1024
PAST_LEN = 256
GRID_W = 64
GRID_ROWS = DEC_SEQ // GRID_W
BRANCH_WIDTH = 512
N_BRANCH = 3
NA_HEADS = 4
NA_HEAD_DIM = 128
NA_WIN_ROWS = 8
NA_WIN_COLS = 16
NA_REL_ROWS = 2 * NA_WIN_ROWS - 1
NA_REL_COLS = 2 * NA_WIN_COLS - 1
DIFF_HEADS = 4
DIFF_QK_DIM = 64
DIFF_V_DIM = 128
RWKV_HEADS = 8
RWKV_HEAD_DIM = 64
RWKV_WIDTH = RWKV_HEADS * RWKV_HEAD_DIM
RWKV_DECAY_RANK = 64
RWKV_ICL_RANK = 64
RWKV_GATE_RANK = 128
RWKV_FEAT = 3 * RWKV_WIDTH + 2 * RWKV_DECAY_RANK + 2 * RWKV_ICL_RANK + RWKV_GATE_RANK
RWKV_GN_EPS = 64e-5
ATT_WIDTH = 6 * BRANCH_WIDTH
GATE_WIDTH = N_BRANCH * D_MODEL
FFN_HIDDEN = -(-8 * D_MODEL // (3 * 256)) * 256
ROPE_THETA = 10000.0
LN_EPS = 1e-5
ALPHA = (2.0 * DEPTH) ** 0.25

N_CTX = BATCH * SEQ
N_LAT = DEC_BATCH * DEC_SEQ
N_TOK = N_CTX + N_LAT
N_COND = 8
RWKV_CHUNK = 64
RWKV_SUPER = 256
RWKV_HEADS_PER_STEP = 4
RWKV_PASSES_PAIR = 1
RWKV_PASSES_SOLVE = 1
RWKV_PASSES_APPLY = 3
RWKV_PASSES_OUT = 1
RWKV_PASSES_STATE = 3
NEG_BIG = -1e30

NN = ((1,), (0,))
NT = ((1,), (1,))
TN = ((0,), (0,))


def _cparams(sem, vmem_mb=48):
    return pltpu.CompilerParams(dimension_semantics=sem, vmem_limit_bytes=vmem_mb * 1024 * 1024)


def _dot(a, b, dims=NN):
    return lax.dot_general(a, b, (dims, ((), ())), preferred_element_type=F32)


def _split2(x):
    hi = x.astype(BF16)
    lo = (x - hi.astype(F32)).astype(BF16)
    return hi, lo


def _dot3(a, b, dims=NN):
    ah, al = _split2(a)
    bh, bl = _split2(b)
    return _dot(ah, bh, dims) + (_dot(ah, bl, dims) + _dot(al, bh, dims))


def _dotp(a, b, dims, passes):
    if passes == 1:
        return _dot(a.astype(BF16), b.astype(BF16), dims)
    assert passes == 3
    return _dot3(a, b, dims)


def _dot_exact_lhs(a_bf16, b, dims=NN):
    b1 = b.astype(BF16)
    r1 = b - b1.astype(F32)
    b2 = r1.astype(BF16)
    b3 = (r1 - b2.astype(F32)).astype(BF16)
    return _dot(a_bf16, b1, dims) + (_dot(a_bf16, b2, dims) + _dot(a_bf16, b3, dims))


def _dot_exact_rhs(a, b_bf16, dims=NN):
    a1 = a.astype(BF16)
    r1 = a - a1.astype(F32)
    a2 = r1.astype(BF16)
    a3 = (r1 - a2.astype(F32)).astype(BF16)
    return _dot(a1, b_bf16, dims) + (_dot(a2, b_bf16, dims) + _dot(a3, b_bf16, dims))


def _normalize(x):
    mu = jnp.mean(x, axis=-1, keepdims=True)
    xc = x - mu
    var = jnp.mean(xc * xc, axis=-1, keepdims=True)
    return xc * lax.rsqrt(var + LN_EPS)


def _cond_of_row(row):
    return jnp.where(row < N_CTX, 0, 1 + (row - N_CTX) // DEC_SEQ)


def _adaln_kernel(c_ref, w_ref, b_ref, o_ref):
    c = c_ref[...]
    s = (c * jax.nn.sigmoid(c)).astype(BF16)
    o_ref[...] = _dot(s, w_ref[...].astype(BF16)) + b_ref[...]


def _adaln_all(cond, w_ada, b_ada):
    tn = 1024
    n = 6 * D_MODEL
    return pl.pallas_call(
        _adaln_kernel,
        grid=(DEPTH, n // tn),
        in_specs=[
            pl.BlockSpec((N_COND, D_MODEL), lambda l, j: (0, 0)),
            pl.BlockSpec((None, D_MODEL, tn), lambda l, j: (l, 0, j)),
            pl.BlockSpec((None, 1, tn), lambda l, j: (l, 0, j)),
        ],
        out_specs=pl.BlockSpec((None, N_COND, tn), lambda l, j: (l, 0, j)),
        out_shape=jax.ShapeDtypeStruct((DEPTH, N_COND, n), F32),
        compiler_params=_cparams(("parallel", "parallel")),
        name="adaln",
    )(cond, w_ada, b_ada.reshape(DEPTH, 1, n))


def _mod_spec(layer, which, tm, blk0=0):
    base = layer * 6 * N_COND + which * N_COND
    return pl.BlockSpec((None, 1, D_MODEL), lambda i: (base + _cond_of_row((blk0 + i) * tm), 0, 0))


def _modulate_kernel(x_ref, sh_ref, sc_ref, h_ref):
    h_ref[...] = (_normalize(x_ref[...]) * (1.0 + sc_ref[...]) + sh_ref[...]).astype(h_ref.dtype)


def _modulate(x, mods, layer):
    tm = 256
    return pl.pallas_call(
        _modulate_kernel,
        grid=(N_TOK // tm,),
        in_specs=[
            pl.BlockSpec((tm, D_MODEL), lambda i: (i, 0)),
            _mod_spec(layer, 0, tm),
            _mod_spec(layer, 1, tm),
        ],
        out_specs=pl.BlockSpec((tm, D_MODEL), lambda i: (i, 0)),
        out_shape=jax.ShapeDtypeStruct((N_TOK, D_MODEL), BF16),
        compiler_params=_cparams(("parallel",)),
        name="modulate",
    )(x, mods, mods)


def _proj_res_ln_kernel(m_ref, w_ref, x_ref, gate_ref, g_ref, b_ref, *rest, with_mod, n_sub):
    if with_mod:
        sh_ref, sc_ref, xo_ref, h_ref = rest
    else:
        (xo_ref,) = rest
    sub = m_ref.shape[0] // n_sub

    def norm_rows(s, z):
        rows = slice(s * sub, (s + 1) * sub)
        y = ALPHA * x_ref[rows, :] + gate_ref[...] * z
        xn = _normalize(y) * g_ref[...] + b_ref[...]
        xo_ref[rows, :] = xn
        if with_mod:
            h_ref[rows, :] = (_normalize(xn) * (1.0 + sc_ref[...]) + sh_ref[...]).astype(h_ref.dtype)

    z_prev = _dot(m_ref[0:sub, :], w_ref[...])
    for s in range(1, n_sub):
        z_next = _dot(m_ref[s * sub:(s + 1) * sub, :], w_ref[...])
        norm_rows(s - 1, z_prev)
        z_prev = z_next
    norm_rows(n_sub - 1, z_prev)


def _proj_res_ln(m, w16, x, mods, layer, gate_idx, ln_g, ln_b, mod_layer, mod_idx, *, tm, n_sub, name, rows=(0, N_TOK)):
    k = m.shape[1]
    with_mod = mod_layer is not None
    blk0, n_rows = rows[0] // tm, rows[1]
    vec = pl.BlockSpec((1, D_MODEL), lambda i: (0, 0))
    row = pl.BlockSpec((tm, D_MODEL), lambda i: (i, 0))
    in_specs = [
        pl.BlockSpec((tm, k), lambda i: (blk0 + i, 0)),
        pl.BlockSpec((None, k, D_MODEL), lambda i: (layer, 0, 0), pipeline_mode=pl.Buffered(1)),
        pl.BlockSpec((tm, D_MODEL), lambda i: (blk0 + i, 0)),
        _mod_spec(layer, gate_idx, tm, blk0), vec, vec,
    ]
    args = [m, w16, x, mods, ln_g.reshape(1, D_MODEL), ln_b.reshape(1, D_MODEL)]
    out_specs = [row]
    out_shape = [jax.ShapeDtypeStruct((n_rows, D_MODEL), F32)]
    if with_mod:
        in_specs += [_mod_spec(mod_layer, mod_idx, tm, blk0), _mod_spec(mod_layer, mod_idx + 1, tm, blk0)]
        args += [mods, mods]
        out_specs.append(row)
        out_shape.append(jax.ShapeDtypeStruct((n_rows, D_MODEL), BF16))
    out = pl.pallas_call(
        functools.partial(_proj_res_ln_kernel, with_mod=with_mod, n_sub=n_sub),
        grid=(n_rows // tm,),
        in_specs=in_specs,
        out_specs=out_specs,
        out_shape=out_shape,
        compiler_params=_cparams(("parallel",), 60),
        name=name,
    )(*args)
    return (out[0], out[1]) if with_mod else (out[0], None)


def _mm_w32_kernel(a_ref, w_ref, o_ref, w16_ref, *, sigmoid):
    @pl.when(pl.program_id(1) == 0)
    def _():
        w16_ref[...] = w_ref[0].astype(BF16)

    acc = _dot(a_ref[...], w16_ref[...])
    if sigmoid:
        acc = jax.nn.sigmoid(acc)
    o_ref[...] = acc.astype(o_ref.dtype)


def _matmul_w32(a, w, layer, out_dtype, col0=0, n=None, tm=1024, tn=1024, name="matmul", sigmoid=False):
    m, k = a.shape
    n = w.shape[2] if n is None else n
    assert m % tm == 0 and n % tn == 0
    return pl.pallas_call(
        functools.partial(_mm_w32_kernel, sigmoid=sigmoid),
        grid=(n // tn, m // tm),
        in_specs=[
            pl.BlockSpec((tm, k), lambda j, i: (i, 0)),
            pl.BlockSpec((pl.Element(1), pl.Element(k), pl.Element(tn)),
                         lambda j, i: (layer, 0, pl.multiple_of(col0 + j * tn, 128))),
        ],
        out_specs=pl.BlockSpec((tm, tn), lambda j, i: (i, j)),
        out_shape=jax.ShapeDtypeStruct((m, n), out_dtype),
        scratch_shapes=[pltpu.VMEM((k, tn), BF16)],
        compiler_params=_cparams(("parallel", "arbitrary")),
        name=name,
    )(a, w)


def _swiglu_kernel(a_ref, wg_ref, wu_ref, o_ref, wg16_ref, wu16_ref):
    @pl.when(pl.program_id(1) == 0)
    def _():
        wg16_ref[...] = wg_ref[...].astype(BF16)
        wu16_ref[...] = wu_ref[...].astype(BF16)

    a = a_ref[...]
    g = _dot(a, wg16_ref[...])
    u = _dot(a, wu16_ref[...])
    o_ref[...] = (g * jax.nn.sigmoid(g) * u).astype(o_ref.dtype)


def _ffn_in(h, w_ffn_in, layer):
    tm, tn = 1024, 512
    nb = FFN_HIDDEN // tn
    return pl.pallas_call(
        _swiglu_kernel,
        grid=(nb, N_TOK // tm),
        in_specs=[
            pl.BlockSpec((tm, D_MODEL), lambda j, i: (i, 0)),
            pl.BlockSpec((None, D_MODEL, tn), lambda j, i: (layer, 0, j)),
            pl.BlockSpec((None, D_MODEL, tn), lambda j, i: (layer, 0, j + nb)),
        ],
        out_specs=pl.BlockSpec((tm, tn), lambda j, i: (i, j)),
        out_shape=jax.ShapeDtypeStruct((N_TOK, FFN_HIDDEN), BF16),
        scratch_shapes=[pltpu.VMEM((D_MODEL, tn), BF16), pltpu.VMEM((D_MODEL, tn), BF16)],
        compiler_params=_cparams(("parallel", "arbitrary")),
        name="ffn_in",
    )(h, w_ffn_in, w_ffn_in)


def _merge_kernel(oa_ref, ob_ref, oc_ref, ga_ref, gb_ref, gc_ref, wb_ref, o_ref):
    acc = ga_ref[...].astype(F32) * _dot(oa_ref[...], wb_ref[0])
    acc += gb_ref[...].astype(F32) * _dot(ob_ref[...], wb_ref[1])
    acc += gc_ref[...].astype(F32) * _dot(oc_ref[...], wb_ref[2])
    o_ref[...] = acc.astype(o_ref.dtype)


def _merge(oa, ob, oc, gate_pre, w_branch, layer):
    tm, tn = 1024, 512
    nb = D_MODEL // tn
    o_spec = pl.BlockSpec((tm, BRANCH_WIDTH), lambda i, j: (i, 0))
    return pl.pallas_call(
        _merge_kernel,
        grid=(N_TOK // tm, nb),
        in_specs=[
            o_spec, o_spec, o_spec,
            pl.BlockSpec((tm, tn), lambda i, j: (i, j)),
            pl.BlockSpec((tm, tn), lambda i, j: (i, j + nb)),
            pl.BlockSpec((tm, tn), lambda i, j: (i, j + 2 * nb)),
            pl.BlockSpec((None, N_BRANCH, BRANCH_WIDTH, tn), lambda i, j: (layer, 0, 0, j)),
        ],
        out_specs=pl.BlockSpec((tm, tn), lambda i, j: (i, j)),
        out_shape=jax.ShapeDtypeStruct((N_TOK, D_MODEL), BF16),
        compiler_params=_cparams(("parallel", "parallel")),
        name="merge",
    )(oa, ob, oc, gate_pre, gate_pre, gate_pre, w_branch)


def _softmax_rows(s):
    m = jnp.max(s, axis=-1, keepdims=True)
    e = jnp.exp(s - m)
    return e * (1.0 / jnp.sum(e, axis=-1, keepdims=True))


def _na_ctx_kernel(q_ref, k_ref, v_ref, *rest):
    o_ref, new_k_ref, new_v_ref = rest[-3:]
    scale = NA_HEAD_DIM ** -0.5
    _store_heads(new_k_ref, k_ref)
    _store_heads(new_v_ref, v_ref)
    heads = range(NA_HEADS)
    sls = [slice(h * NA_HEAD_DIM, (h + 1) * NA_HEAD_DIM) for h in heads]
    s = [_dot(q_ref[:, sl].astype(BF16), k_ref[:, sl].astype(BF16), NT) * scale for sl in sls]
    p = [_softmax_rows(s[h]).astype(BF16) for h in heads]
    o = [_dot(p[h], v_ref[:, sls[h]].astype(BF16)) for h in heads]
    for h in heads:
        o_ref[:, sls[h]] = o[h].astype(o_ref.dtype)


_ANY_SPEC = pl.BlockSpec(memory_space=pl.ANY)


def _cache_stack_shape(split_heads):
    tail = (NA_HEADS, NA_HEAD_DIM) if split_heads else (BRANCH_WIDTH,)
    return jax.ShapeDtypeStruct((BATCH, DEPTH, SEQ) + tail, F32)


def _cache_stack_spec(split_heads, index_map):
    tail = (NA_HEADS, NA_HEAD_DIM) if split_heads else (BRANCH_WIDTH,)
    return pl.BlockSpec((None, None, SEQ) + tail, index_map)


def _store_heads(dst_ref, src_ref):
    for h in range(NA_HEADS):
        dst_ref[:, h, :] = src_ref[:, h * NA_HEAD_DIM:(h + 1) * NA_HEAD_DIM]


def _na_ctx(p_att, layer, stacks):
    w = BRANCH_WIDTH
    stack_spec = _cache_stack_spec(True, lambda n: (n, layer, 0, 0, 0))
    extra = [] if stacks is None else list(stacks)
    return pl.pallas_call(
        _na_ctx_kernel,
        grid=(BATCH,),
        in_specs=[
            pl.BlockSpec((SEQ, w), lambda n: (n, 0)),
            pl.BlockSpec((SEQ, w), lambda n: (n, 1)),
            pl.BlockSpec((SEQ, w), lambda n: (n, 2)),
        ] + [_ANY_SPEC] * len(extra),
        out_specs=[pl.BlockSpec((SEQ, w), lambda n: (n, 0)), stack_spec, stack_spec],
        out_shape=[jax.ShapeDtypeStruct((N_TOK, w), BF16), _cache_stack_shape(True), _cache_stack_shape(True)],
        input_output_aliases={3: 1, 4: 2} if extra else {},
        compiler_params=_cparams(("parallel",)),
        name="na_ctx",
    )(p_att, p_att, p_att, *extra)


def _na_row_start(r):
    return jnp.clip(r - NA_WIN_ROWS // 2, 0, GRID_ROWS - NA_WIN_ROWS)


def _na_lat_kernel(q_ref, k_ref, v_ref, kc_ref, vc_ref, bias_ref, _, o_ref):
    scale = NA_HEAD_DIM ** -0.5
    n_loc = NA_WIN_ROWS * GRID_W
    start = pl.multiple_of(_na_row_start(pl.program_id(1)) * GRID_W, GRID_W)
    k_loc = k_ref[pl.ds(start, n_loc), :].astype(BF16)
    v_loc = v_ref[pl.ds(start, n_loc), :].astype(BF16)
    heads = range(NA_HEADS)
    sls = [slice(h * NA_HEAD_DIM, (h + 1) * NA_HEAD_DIM) for h in heads]
    q = [q_ref[:, sl].astype(BF16) for sl in sls]
    s_loc = [_dot(q[h], k_loc[:, sls[h]], NT) * scale + bias_ref[h] for h in heads]
    s_ctx = [_dot(q[h], kc_ref[:, sls[h]].astype(BF16), NT) * scale for h in heads]
    m = [jnp.maximum(jnp.max(s_loc[h], axis=-1, keepdims=True), jnp.max(s_ctx[h], axis=-1, keepdims=True))
         for h in heads]
    e_loc = [jnp.exp(s_loc[h] - m[h]) for h in heads]
    e_ctx = [jnp.exp(s_ctx[h] - m[h]) for h in heads]
    inv = [1.0 / (jnp.sum(e_loc[h], axis=-1, keepdims=True) + jnp.sum(e_ctx[h], axis=-1, keepdims=True))
           for h in heads]
    o_loc = [_dot((e_loc[h] * inv[h]).astype(BF16), v_loc[:, sls[h]]) for h in heads]
    o_ctx = [_dot((e_ctx[h] * inv[h]).astype(BF16), vc_ref[:, sls[h]].astype(BF16)) for h in heads]
    for h in heads:
        o_ref[:, sls[h]] = (o_loc[h] + o_ctx[h]).astype(o_ref.dtype)


def _na_bias_kernel(rows_ref, onehot_ref, mask_ref, o_ref):
    o_ref[...] = _dot_exact_rhs(rows_ref[...], onehot_ref[...]) + mask_ref[...]


def _na_bias_table(rpb):
    nr, nd = NA_WIN_ROWS, NA_REL_COLS + 1
    rows = jnp.stack([rpb[:, nr - 1 - p:2 * nr - 1 - p, :] for p in range(nr)], axis=1)
    rows = jnp.pad(rows.astype(F32), ((0, 0), (0, 0), (0, 0), (0, nd - NA_REL_COLS)))
    qc = np.arange(GRID_W)[:, None]
    kc = np.arange(GRID_W)[None, :]
    rel_c = np.clip(kc - qc, -(NA_WIN_COLS - 1), NA_WIN_COLS - 1) + NA_WIN_COLS - 1
    onehot = (rel_c[None] == np.arange(nd)[:, None, None]).reshape(nd, GRID_W * GRID_W)
    win_c0 = np.clip(qc - NA_WIN_COLS // 2, 0, GRID_W - NA_WIN_COLS)
    valid = ((kc >= win_c0) & (kc < win_c0 + NA_WIN_COLS)).reshape(1, GRID_W * GRID_W)
    n_rows = NA_HEADS * nr * nr
    full = lambda shape: pl.BlockSpec(shape, lambda: (0,) * len(shape))
    bias = pl.pallas_call(
        _na_bias_kernel,
        in_specs=[full((n_rows, nd)), full((nd, GRID_W * GRID_W)), full((1, GRID_W * GRID_W))],
        out_specs=full((n_rows, GRID_W * GRID_W)),
        out_shape=jax.ShapeDtypeStruct((n_rows, GRID_W * GRID_W), F32),
        name="na_bias",
    )(rows.reshape(n_rows, nd), jnp.asarray(onehot, BF16), jnp.asarray(np.where(valid, 0.0, NEG_BIG), F32))
    bias = bias.reshape(NA_HEADS, nr, nr, GRID_W, GRID_W).transpose(0, 1, 3, 2, 4)
    return bias.reshape(NA_HEADS, nr, GRID_W, nr * GRID_W)


def _na_lat(p_att, cache_k, cache_v, layer, bias, oa):
    w = BRANCH_WIDTH
    blk0 = N_CTX // DEC_SEQ
    q0 = N_CTX // GRID_W
    rows_per = DEC_SEQ // GRID_W
    kv_cache = pl.BlockSpec((None, None, PAST_LEN, w), lambda n, r: (n, layer, 0, 0))
    return pl.pallas_call(
        _na_lat_kernel,
        grid=(DEC_BATCH, GRID_ROWS),
        in_specs=[
            pl.BlockSpec((GRID_W, w), lambda n, r: (q0 + n * rows_per + r, 0)),
            pl.BlockSpec((DEC_SEQ, w), lambda n, r: (blk0 + n, 1)),
            pl.BlockSpec((DEC_SEQ, w), lambda n, r: (blk0 + n, 2)),
            kv_cache, kv_cache,
            pl.BlockSpec((NA_HEADS, None, GRID_W, NA_WIN_ROWS * GRID_W),
                         lambda n, r: (0, r - _na_row_start(r), 0, 0)),
            _ANY_SPEC,
        ],
        out_specs=pl.BlockSpec((GRID_W, w), lambda n, r: (q0 + n * rows_per + r, 0)),
        out_shape=jax.ShapeDtypeStruct((N_TOK, w), BF16),
        input_output_aliases={6: 0},
        compiler_params=_cparams(("parallel", "arbitrary")),
        name="na_lat",
    )(p_att, p_att, p_att, cache_k, cache_v, bias, oa)


def _rope(x, cos, sin):
    lane = lax.broadcasted_iota(jnp.int32, x.shape, 1)
    first = (lane % 32) < 16
    rot = jnp.where(first, -pltpu.roll(x, 128 - 16, 1), pltpu.roll(x, 16, 1))
    return x * cos + rot * sin


def _diff_kernel(*refs, rope, cache, lam_init, tq):
    if cache:
        q_ref, k_ref, v_ref, kc_ref, vc_ref, cos_ref, sin_ref, lam_ref, g_ref, _, o_ref = refs
    else:
        q_ref, k_ref, v_ref, lam_ref, g_ref = refs[:5]
        o_ref, new_k_ref, new_v_ref = refs[-3:]
        new_k_ref[...] = k_ref[...]
        _store_heads(new_v_ref, v_ref)
    scale = DIFF_QK_DIM ** -0.5
    lp = lam_ref[...]
    lam = (jnp.exp(jnp.sum(lp[0:1] * lp[1:2], axis=-1, keepdims=True))
           - jnp.exp(jnp.sum(lp[2:3] * lp[3:4], axis=-1, keepdims=True)) + lam_init)
    hd = DIFF_V_DIM
    t = q_ref.shape[0]
    n_heads = q_ref.shape[1] // hd
    ks, vs = [], []
    for hh in range(n_heads):
        sl = slice(hh * hd, (hh + 1) * hd)
        k = k_ref[:, sl]
        v = v_ref[:, sl]
        if rope:
            k = _rope(k, cos_ref[...], sin_ref[...])
        if cache:
            k = jnp.concatenate([k, kc_ref[:, sl]], axis=0)
            v = jnp.concatenate([v, vc_ref[:, sl]], axis=0)
        ks.append(k.astype(BF16))
        vs.append(v.astype(BF16))
    units = [(hh, i) for hh in range(n_heads) for i in range(t // tq)]
    rows = {u: slice(u[1] * tq, (u[1] + 1) * tq) for u in units}
    cols = {u: slice(u[0] * hd, (u[0] + 1) * hd) for u in units}
    q = {u: q_ref[rows[u], cols[u]] for u in units}
    if rope:
        q = {u: _rope(q[u], cos_ref[rows[u], :], sin_ref[rows[u], :]) for u in units}
    lane = lax.broadcasted_iota(jnp.int32, (tq, hd), 1)
    s1 = {u: _dot(jnp.where(lane < DIFF_QK_DIM, q[u], 0.0).astype(BF16), ks[u[0]], NT) * scale for u in units}
    s2 = {u: _dot(jnp.where(lane >= DIFF_QK_DIM, q[u], 0.0).astype(BF16), ks[u[0]], NT) * scale for u in units}
    w = {u: (_softmax_rows(s1[u]) - lam * _softmax_rows(s2[u])).astype(BF16) for u in units}
    o = {u: _dot(w[u], vs[u[0]]) for u in units}
    for u in units:
        on = o[u] * lax.rsqrt(jnp.mean(o[u] * o[u], axis=-1, keepdims=True) + LN_EPS)
        o_ref[rows[u], cols[u]] = (on * g_ref[:, cols[u]] * (1.0 - lam_init)).astype(o_ref.dtype)


def _diff_attn(p_att, lam_p, subln, lam_init, layer, *, t, n_seq, blk0, cache=None, tables=None, stacks=None, ob=None):
    hd = DIFF_V_DIM * (1 if cache is not None else DIFF_HEADS)
    q_col, k_col, v_col = (z * BRANCH_WIDTH // hd for z in (3, 4, 5))
    in_specs = [
        pl.BlockSpec((t, hd), lambda n, h: (blk0 + n, q_col + h)),
        pl.BlockSpec((t, hd), lambda n, h: (blk0 + n, k_col + h)),
        pl.BlockSpec((t, hd), lambda n, h: (blk0 + n, v_col + h)),
    ]
    args = [p_att, p_att, p_att]
    if cache is not None:
        cache_k, cache_v = cache
        tab = pl.BlockSpec((t, hd), lambda n, h: (0, 0))
        cache_spec = pl.BlockSpec((None, None, PAST_LEN, hd), lambda n, h: (n, layer, 0, h))
        in_specs += [cache_spec, cache_spec, tab, tab]
        args += [cache_k, cache_v, tables[0], tables[1]]
    in_specs += [
        pl.BlockSpec((4, DIFF_QK_DIM), lambda n, h: (0, 0)),
        pl.BlockSpec((1, hd), lambda n, h: (0, h)),
    ]
    args += [lam_p, subln.reshape(1, BRANCH_WIDTH)]
    out_specs = [pl.BlockSpec((t, hd), lambda n, h: (blk0 + n, h))]
    out_shape = [jax.ShapeDtypeStruct((N_TOK, BRANCH_WIDTH), BF16)]
    aliases = {}
    if cache is not None:
        aliases = {len(args): 0}
        in_specs.append(_ANY_SPEC)
        args.append(ob)
    else:
        out_specs += [_cache_stack_spec(False, lambda n, h: (n, layer, 0, 0)),
                      _cache_stack_spec(True, lambda n, h: (n, layer, 0, 0, 0))]
        out_shape += [_cache_stack_shape(False), _cache_stack_shape(True)]
        if stacks is not None:
            aliases = {len(args): 1, len(args) + 1: 2}
            in_specs += [_ANY_SPEC, _ANY_SPEC]
            args += list(stacks)
    return pl.pallas_call(
        functools.partial(_diff_kernel, rope=cache is not None, cache=cache is not None,
                          lam_init=lam_init, tq=256),
        grid=(n_seq, BRANCH_WIDTH // hd),
        in_specs=in_specs,
        out_specs=out_specs,
        out_shape=out_shape,
        input_output_aliases=aliases,
        compiler_params=_cparams(("parallel", "parallel")),
        name="diff_lat" if cache is not None else "diff_ctx",
    )(*args)


def _rope_tables():
    t = np.arange(DEC_SEQ)
    rows = (t // GRID_W).astype(np.float32)
    cols = (t % GRID_W).astype(np.float32)
    half = DIFF_QK_DIM // 2
    inv = jnp.asarray(ROPE_THETA, F32) ** (-jnp.arange(0, half, 2, dtype=F32) / half)
    ang_r = jnp.asarray(rows)[:, None] * inv
    ang_c = jnp.asarray(cols)[:, None] * inv
    ang = jnp.concatenate([ang_r, ang_r, ang_c, ang_c] * 2, axis=-1)
    return jnp.cos(ang), jnp.sin(ang)


def _head_sum(x, ones_bd):
    return _dot_exact_rhs(x, ones_bd)


def _rwkv_prep_kernel(cur_ref, prev_ref, next_ref, mix_ref, w0_ref, w2_ref, a0_ref, a2_ref, g2_ref,
                      kk_ref, ka_ref, rk_ref, ones_ref,
                      r_ref, v_ref, kkn_ref, lwf_ref, lwb_ref, kf_ref, kb_ref, bf_ref, bb_ref, bonus_ref, g_ref,
                      *, tm):
    i = pl.program_id(0)
    blocks_per_seq = DEC_SEQ // tm
    j = i - N_CTX // tm
    is_ctx = i < N_CTX // tm
    at_start = jnp.logical_or(is_ctx, j % blocks_per_seq == 0)
    at_end = jnp.logical_or(is_ctx, j % blocks_per_seq == blocks_per_seq - 1)
    cur = cur_ref[...]
    row = lax.broadcasted_iota(jnp.int32, cur.shape, 0)
    prev_row = jnp.where(at_start, 0.0, prev_ref[7:8, :])
    next_row = jnp.where(at_end, 0.0, next_ref[0:1, :])
    prev = jnp.where(row == 0, prev_row, pltpu.roll(cur, 1, 0))
    nxt = jnp.where(row == tm - 1, next_row, pltpu.roll(cur, tm - 1, 0))
    f = cur + mix_ref[0:1, :] * (prev - cur) + mix_ref[1:2, :] * (nxt - cur)

    w = RWKV_WIDTH
    r = f[:, 0:w]
    k = f[:, w:2 * w]
    v = f[:, 2 * w:3 * w]
    c0 = 3 * w
    wd = f[:, c0:c0 + 2 * RWKV_DECAY_RANK]
    ad = f[:, c0 + 2 * RWKV_DECAY_RANK:c0 + 2 * RWKV_DECAY_RANK + 2 * RWKV_ICL_RANK]
    gd = f[:, c0 + 2 * RWKV_DECAY_RANK + 2 * RWKV_ICL_RANK:]
    ones_bd = ones_ref[...]

    kkv = k * kk_ref[...]
    norm = jnp.sqrt(_head_sum(kkv * kkv, ones_bd))
    kkn = kkv / jnp.maximum(norm, 1e-12)
    wlin = _dot(jnp.tanh(wd).astype(BF16), w2_ref[...]) + w0_ref[...]
    alin = _dot(ad.astype(BF16), a2_ref[...]) + a0_ref[...]
    g = _dot(jax.nn.sigmoid(gd).astype(BF16), g2_ref[...])

    r_ref[...] = r
    v_ref[...] = v
    kkn_ref[...] = kkn
    g_ref[...] = g
    bonus = jnp.zeros_like(v)
    for d, (lw_ref, kd_ref, bd_ref) in enumerate(((lwf_ref, kf_ref, bf_ref), (lwb_ref, kb_ref, bb_ref))):
        z = -wlin[:, d * w:(d + 1) * w]
        softplus = jnp.maximum(z, 0.0) + jnp.log(1.0 + jnp.exp(-jnp.abs(z)))
        lw_ref[...] = -jnp.exp(-softplus - 0.5)
        a = jax.nn.sigmoid(alin[:, d * w:(d + 1) * w])
        k_d = k * (1.0 + (a - 1.0) * ka_ref[...])
        kd_ref[...] = k_d
        bd_ref[...] = kkn * a
        bonus += _head_sum(r * k_d * rk_ref[...], ones_bd) * v
    bonus_ref[...] = bonus


def _block_diag2(a, b):
    za = jnp.zeros((a.shape[0], b.shape[1]), a.dtype)
    zb = jnp.zeros((b.shape[0], a.shape[1]), b.dtype)
    return jnp.concatenate([jnp.concatenate([a, za], axis=1), jnp.concatenate([zb, b], axis=1)], axis=0)


def _rwkv_prep(p_rw, lp):
    tm = 256
    w = RWKV_WIDTH
    sub = tm // 8
    full = lambda shape: pl.BlockSpec(shape, lambda i: (0,) * len(shape))
    out_spec = pl.BlockSpec((tm, w), lambda i: (i, 0))
    ones_bd = jnp.asarray(np.kron(np.eye(RWKV_HEADS), np.ones((RWKV_HEAD_DIM, RWKV_HEAD_DIM))), BF16)
    w2_bd = _block_diag2(lp['rwkv_w2'][0], lp['rwkv_w2'][1]).astype(BF16)
    a2_bd = _block_diag2(lp['rwkv_a2'][0], lp['rwkv_a2'][1]).astype(BF16)
    n_out = 11
    return pl.pallas_call(
        functools.partial(_rwkv_prep_kernel, tm=tm),
        grid=(N_TOK // tm,),
        in_specs=[
            pl.BlockSpec((tm, RWKV_FEAT), lambda i: (i, 0)),
            pl.BlockSpec((8, RWKV_FEAT), lambda i: (jnp.maximum(i * sub - 1, 0), 0)),
            pl.BlockSpec((8, RWKV_FEAT), lambda i: (jnp.minimum((i + 1) * sub, N_TOK // 8 - 1), 0)),
            full((2, RWKV_FEAT)),
            full((1, 2 * w)), full((2 * RWKV_DECAY_RANK, 2 * w)),
            full((1, 2 * w)), full((2 * RWKV_ICL_RANK, 2 * w)),
            full((RWKV_GATE_RANK, w)),
            full((1, w)), full((1, w)), full((1, w)),
            full((w, w)),
        ],
        out_specs=[out_spec] * n_out,
        out_shape=[jax.ShapeDtypeStruct((N_TOK, w), F32)] * n_out,
        compiler_params=_cparams(("parallel",)),
        name="rwkv_prep",
    )(p_rw, p_rw, p_rw, lp['rwkv_mix'],
      lp['rwkv_w0'].reshape(1, 2 * w), w2_bd, lp['rwkv_a0'].reshape(1, 2 * w), a2_bd,
      lp['rwkv_g2'].astype(BF16),
      lp['rwkv_kk'].reshape(1, w), lp['rwkv_ka'].reshape(1, w), lp['rwkv_rk'].reshape(1, w),
      ones_bd)


def _rwkv_chunk_maps(chains):
    n = len(chains)
    rs, ks, vs, kkns, bs, lws, fwds = (list(z) for z in zip(*chains))
    c, nk = rs[0].shape
    row = lax.broadcasted_iota(jnp.int32, (c, c), 0)
    col = lax.broadcasted_iota(jnp.int32, (c, c), 1)
    eye = row == col
    incl_d = {True: row >= col, False: row <= col}
    strict_d = {True: row > col, False: row < col}
    tri_d = {f: jnp.where(m, 1.0, 0.0).astype(BF16) for f, m in incl_d.items()}
    idx = range(n)
    cum = [_dot_exact_lhs(tri_d[fwds[i]], lws[i]) for i in idx]
    tot = [cum[i][c - 1:c, :] if fwds[i] else cum[i][0:1, :] for i in idx]
    e_neg = [jnp.exp(-cum[i]) for i in idx]
    a_t = [-kkns[i] * jnp.exp(cum[i] - lws[i]) for i in idx]
    r_t = [rs[i] * jnp.exp(cum[i]) for i in idx]
    ar = [jnp.concatenate([a_t[i], r_t[i]], axis=0) for i in idx]
    with_b = [_dotp(ar[i], bs[i] * e_neg[i], NT, RWKV_PASSES_PAIR) for i in idx]
    with_k = [_dotp(ar[i], ks[i] * e_neg[i], NT, RWKV_PASSES_PAIR) for i in idx]
    l_ab = [jnp.where(strict_d[fwds[i]], with_b[i][:c], 0.0) for i in idx]
    l_ak = [jnp.where(strict_d[fwds[i]], with_k[i][:c], 0.0) for i in idx]
    m_rb = [jnp.where(incl_d[fwds[i]], with_b[i][c:], 0.0) for i in idx]
    m_rk = [jnp.where(incl_d[fwds[i]], with_k[i][c:], 0.0) for i in idx]
    lakv = [_dotp(l_ak[i], vs[i], NN, RWKV_PASSES_APPLY) for i in idx]
    mrkv = [_dotp(m_rk[i], vs[i], NN, RWKV_PASSES_OUT) for i in idx]
    e_tail = [jnp.exp(tot[i] - cum[i]) for i in idx]
    kwv = [_dotp(ks[i] * e_tail[i], vs[i], TN, RWKV_PASSES_APPLY) for i in idx]
    same = lambda s: (row // s) == (col // s)
    inv = [jnp.where(eye, 1.0, 0.0) + jnp.where(same(2), l_ab[i], 0.0) for i in idx]
    size = 4
    while size <= c:
        part = jnp.logical_and(same(size), jnp.logical_not(same(size // 2)))
        half = [_dotp(inv[i], jnp.where(part, l_ab[i], 0.0), NN, RWKV_PASSES_SOLVE) for i in idx]
        inv = [inv[i] + _dotp(half[i], inv[i], NN, RWKV_PASSES_SOLVE) for i in idx]
        size *= 2
    pq1 = [_dotp(inv[i], jnp.concatenate([a_t[i], lakv[i]], axis=1), NN, RWKV_PASSES_APPLY) for i in idx]
    rb = [_dotp(m_rb[i], pq1[i], NN, RWKV_PASSES_OUT) for i in idx]
    sb = [_dotp(bs[i] * e_tail[i], pq1[i], TN, RWKV_PASSES_APPLY) for i in idx]
    out = []
    for i in idx:
        p2 = r_t[i] + rb[i][:, :nk]
        q2 = rb[i][:, nk:] + mrkv[i]
        p3 = jnp.where(eye, jnp.exp(tot[i]), 0.0) + sb[i][:, :nk]
        q3 = sb[i][:, nk:] + kwv[i]
        out.append((p2, p3, q2, q3))
    return out


def _rwkv_scan_kernel(rf, vf, af, wf, kf, bf, rb, vb, ab, wb, kb, bb, s0f, s0b, *rest, n_super, n_chunk):
    yf, yb, sff, sfb, st_ref = rest[-5:]
    sup = pl.program_id(2)
    hd = RWKV_HEAD_DIM
    cs = RWKV_CHUNK
    hp = RWKV_HEADS_PER_STEP

    @pl.when(sup == 0)
    def _():
        st_ref[0:hp] = s0f[...]
        st_ref[hp:2 * hp] = s0b[...]

    dirs = ((rf, vf, af, wf, kf, bf, yf, True), (rb, vb, ab, wb, kb, bb, yb, False))
    keys, chains = [], []
    for d, (r_ref, v_ref, a_ref, w_ref, k_ref, b_ref, _, forward) in enumerate(dirs):
        for ci in range(n_chunk):
            rows = slice(ci * cs, (ci + 1) * cs)
            for hh in range(hp):
                sl = slice(hh * hd, (hh + 1) * hd)
                keys.append((d, ci, hh))
                chains.append((r_ref[rows, sl], k_ref[rows, sl], v_ref[rows, sl],
                               a_ref[rows, sl], b_ref[rows, sl], w_ref[rows, sl], forward))
    maps = dict(zip(keys, _rwkv_chunk_maps(chains)))

    seqs = [(d, hh) for d in range(2) for hh in range(hp)]
    st = {s: st_ref[hp * s[0] + s[1]] for s in seqs}
    ys = {}
    for step in range(n_chunk):
        for d, hh in seqs:
            ci = step if dirs[d][-1] else n_chunk - 1 - step
            p2, p3, q2, q3 = maps[d, ci, hh]
            ys[d, ci, hh] = _dotp(p2, st[d, hh], NN, RWKV_PASSES_OUT) + q2
            st[d, hh] = _dotp(p3, st[d, hh], NN, RWKV_PASSES_STATE) + q3
    for d, hh in seqs:
        st_ref[hp * d + hh] = st[d, hh]
    for d in range(2):
        y_ref = dirs[d][6]
        for ci in range(n_chunk):
            y_ref[ci * cs:(ci + 1) * cs, :] = jnp.concatenate([ys[d, ci, hh] for hh in range(hp)], axis=1)

    @pl.when(sup == n_super - 1)
    def _():
        sff[...] = st_ref[0:hp]
        sfb[...] = st_ref[hp:2 * hp]


def _rwkv_scan(feats, s0f_t, s0b_t, *, t, n_seq, row0, y_prev=None):
    r, v, kkn, lwf, lwb, kf, kb, bf, bb = feats
    rows = RWKV_SUPER
    n_super = t // rows
    blk0 = row0 // rows
    hp = RWKV_HEADS_PER_STEP
    width = hp * RWKV_HEAD_DIM
    fwd = pl.BlockSpec((rows, width), lambda n, p, s: (blk0 + n * n_super + s, p))
    bwd = pl.BlockSpec((rows, width), lambda n, p, s: (blk0 + n * n_super + n_super - 1 - s, p))
    st_spec = pl.BlockSpec((None, hp, RWKV_HEAD_DIM, RWKV_HEAD_DIM), lambda n, p, s: (n, p, 0, 0))
    st_shape = jax.ShapeDtypeStruct((n_seq, RWKV_HEADS, RWKV_HEAD_DIM, RWKV_HEAD_DIM), F32)
    y_shape = jax.ShapeDtypeStruct((N_TOK, RWKV_WIDTH), F32)
    extra = [] if y_prev is None else list(y_prev)
    n_in = 14
    return pl.pallas_call(
        functools.partial(_rwkv_scan_kernel, n_super=n_super, n_chunk=rows // RWKV_CHUNK),
        grid=(n_seq, RWKV_HEADS // hp, n_super),
        in_specs=[fwd] * 6 + [bwd] * 6 + [st_spec, st_spec] + [_ANY_SPEC] * len(extra),
        out_specs=[fwd, bwd, st_spec, st_spec],
        out_shape=[y_shape, y_shape, st_shape, st_shape],
        input_output_aliases={n_in: 0, n_in + 1: 1} if extra else {},
        scratch_shapes=[pltpu.VMEM((2 * hp, RWKV_HEAD_DIM, RWKV_HEAD_DIM), F32)],
        compiler_params=_cparams(("parallel", "parallel", "arbitrary")),
        name="rwkv_scan",
    )(r, v, kkn, lwf, kf, bf, r, v, kkn, lwb, kb, bb, s0f_t, s0b_t, *extra)


def _rwkv_mixer(p_rw, s_lat_f, s_lat_b, lp):
    r, v, kkn, lwf, lwb, kf, kb, bf, bb, bonus, g = _rwkv_prep(p_rw, lp)
    feats = (r, v, kkn, lwf, lwb, kf, kb, bf, bb)
    s_zero = jnp.zeros((BATCH, RWKV_HEADS, RWKV_HEAD_DIM, RWKV_HEAD_DIM), F32)
    y_f, y_b, sf_c, sb_c = _rwkv_scan(feats, s_zero, s_zero, t=SEQ, n_seq=BATCH, row0=0)
    y_f, y_b, _, _ = _rwkv_scan(feats, jnp.swapaxes(s_lat_f, -1, -2), jnp.swapaxes(s_lat_b, -1, -2),
                                t=DEC_SEQ, n_seq=DEC_BATCH, row0=N_CTX, y_prev=(y_f, y_b))
    oc = _rwkv_out(y_f, y_b, bonus, g, lp['rwkv_lnx_g'], lp['rwkv_lnx_b'])
    return oc, jnp.swapaxes(sf_c, -1, -2), jnp.swapaxes(sb_c, -1, -2)


def _rwkv_out_kernel(yf_ref, yb_ref, bonus_ref, g_ref, lg_ref, lb_ref, ones_ref, o_ref):
    ones_bd = ones_ref[...]
    y = yf_ref[...] + yb_ref[...]
    mu = _head_sum(y, ones_bd) * (1.0 / RWKV_HEAD_DIM)
    yc = y - mu
    var = _head_sum(yc * yc, ones_bd) * (1.0 / RWKV_HEAD_DIM)
    yn = yc * lax.rsqrt(var + RWKV_GN_EPS) * lg_ref[...] + lb_ref[...]
    o_ref[...] = ((yn + bonus_ref[...]) * g_ref[...]).astype(o_ref.dtype)


def _rwkv_out(y_f, y_b, bonus, g, lnx_g, lnx_b):
    tm = 512
    w = RWKV_WIDTH
    row = pl.BlockSpec((tm, w), lambda i: (i, 0))
    vec = pl.BlockSpec((1, w), lambda i: (0, 0))
    ones_bd = jnp.asarray(np.kron(np.eye(RWKV_HEADS), np.ones((RWKV_HEAD_DIM, RWKV_HEAD_DIM))), BF16)
    return pl.pallas_call(
        _rwkv_out_kernel,
        grid=(N_TOK // tm,),
        in_specs=[row, row, row, row, vec, vec, pl.BlockSpec((w, w), lambda i: (0, 0))],
        out_specs=row,
        out_shape=jax.ShapeDtypeStruct((N_TOK, w), BF16),
        compiler_params=_cparams(("parallel",)),
        name="rwkv_out",
    )(y_f, y_b, bonus, g, lnx_g.reshape(1, w), lnx_b.reshape(1, w), ones_bd)


def kernel(x_prompt, x_sample, cache_na_k, cache_na_v, cache_diff_k, cache_diff_v, state_rwkv_fwd, state_rwkv_bwd, c, c_ctx, w_ada, b_ada, w_in, na_rpb, diff_lambda, diff_subln, rwkv_mix, rwkv_w0, rwkv_w2, rwkv_a0, rwkv_a2, rwkv_g2, rwkv_kk, rwkv_ka, rwkv_rk, rwkv_lnx_g, rwkv_lnx_b, w_branch, w_out, ln1_g, ln1_b, w_ffn_in, w_ffn_out, ln2_g, ln2_b):
    x = jnp.concatenate([x_prompt.reshape(N_CTX, D_MODEL), x_sample.reshape(N_LAT, D_MODEL)], axis=0)
    cond = jnp.concatenate([c_ctx[None, :], c, jnp.zeros((N_COND - 1 - DEC_BATCH, D_MODEL), F32)], axis=0)
    mods = _adaln_all(cond, w_ada, b_ada)
    mods = mods.reshape(DEPTH, N_COND, 6, D_MODEL).transpose(0, 2, 1, 3).reshape(DEPTH * 6 * N_COND, 1, D_MODEL)

    cache_na_k = cache_na_k.reshape(DEC_BATCH, DEPTH, PAST_LEN, BRANCH_WIDTH)
    cache_na_v = cache_na_v.reshape(DEC_BATCH, DEPTH, PAST_LEN, BRANCH_WIDTH)
    cache_diff_k = cache_diff_k.reshape(DEC_BATCH, DEPTH, PAST_LEN, BRANCH_WIDTH)
    cache_diff_v = cache_diff_v.reshape(DEC_BATCH, DEPTH, PAST_LEN, BRANCH_WIDTH)
    rope_tables = _rope_tables()

    c1 = ATT_WIDTH + RWKV_FEAT
    w_branch16 = w_branch.astype(BF16)
    w_out16 = w_out.astype(BF16)
    w_ffn_out16 = w_ffn_out.astype(BF16)

    h = _modulate(x, mods, 0)
    na_stacks, diff_stacks, new_f, new_b = None, None, [], []
    for l in range(DEPTH):
        lam_init = 0.8 - 0.6 * math.exp(-0.3 * l)
        lp = {'rwkv_mix': rwkv_mix[l], 'rwkv_w0': rwkv_w0[l], 'rwkv_w2': rwkv_w2[l], 'rwkv_a0': rwkv_a0[l],
              'rwkv_a2': rwkv_a2[l], 'rwkv_g2': rwkv_g2[l], 'rwkv_kk': rwkv_kk[l], 'rwkv_ka': rwkv_ka[l],
              'rwkv_rk': rwkv_rk[l], 'rwkv_lnx_g': rwkv_lnx_g[l], 'rwkv_lnx_b': rwkv_lnx_b[l]}
        p_att = _matmul_w32(h, w_in, l, F32, col0=0, n=ATT_WIDTH, name="in_att")
        p_rw = _matmul_w32(h, w_in, l, F32, col0=ATT_WIDTH, n=RWKV_FEAT, tn=RWKV_FEAT // 3, name="in_rwkv")
        gates = _matmul_w32(h, w_in, l, BF16, col0=c1, n=GATE_WIDTH, name="in_gate", sigmoid=True)

        oa, *na_stacks = _na_ctx(p_att, l, na_stacks)
        oa = _na_lat(p_att, cache_na_k, cache_na_v, l, _na_bias_table(na_rpb[l]), oa)
        ob, *diff_stacks = _diff_attn(p_att, diff_lambda[l], diff_subln[l], lam_init, l, t=SEQ, n_seq=BATCH, blk0=0,
                                      stacks=diff_stacks)
        ob, = _diff_attn(p_att, diff_lambda[l], diff_subln[l], lam_init, l, t=DEC_SEQ, n_seq=DEC_BATCH,
                         blk0=N_CTX // DEC_SEQ, cache=(cache_diff_k, cache_diff_v), tables=rope_tables, ob=ob)
        oc, sf_c, sb_c = _rwkv_mixer(p_rw, state_rwkv_fwd[:, l], state_rwkv_bwd[:, l], lp)

        merged = _merge(oa, ob, oc, gates, w_branch16, l)
        x, h2 = _proj_res_ln(merged, w_out16, x, mods, l, 2, ln1_g[l], ln1_b[l], l, 3, tm=512, n_sub=4, name="w_out_res_ln")
        hid = _ffn_in(h2, w_ffn_in, l)
        new_f.append(sf_c)
        new_b.append(sb_c)
        ffn_out = functools.partial(_proj_res_ln, hid, w_ffn_out16, x, mods, l, 5, ln2_g[l], ln2_b[l],
                                    tm=256, n_sub=1, name="ffn_out_res_ln")
        if l + 1 < DEPTH:
            x, h = ffn_out(l + 1, 0)
        else:
            y_ctx, _ = ffn_out(None, 0, rows=(0, N_CTX))
            y_lat, _ = ffn_out(None, 0, rows=(N_CTX, N_LAT))

    y_prompt = y_ctx.reshape(BATCH, SEQ, D_MODEL)
    y_sample = y_lat.reshape(DEC_BATCH, DEC_SEQ, D_MODEL)
    new_na_k, new_na_v = na_stacks
    new_diff_k = diff_stacks[0].reshape(BATCH, DEPTH, SEQ, DIFF_HEADS, 2, DIFF_QK_DIM)
    new_diff_v = diff_stacks[1]
    return (y_prompt, y_sample, new_na_k, new_na_v, new_diff_k, new_diff_v,
            jnp.stack(new_f, axis=1), jnp.stack(new_b, axis=1))
```

```python
import functools
import math

import numpy as np
import jax
import jax.numpy as jnp
from jax import lax
from jax.experimental import pallas as pl
from jax.experimental.pallas import tpu as pltpu

F32 = jnp.float32
BF16 = jnp.bfloat16

D_MODEL = 2048
BATCH = 16
SEQ = 256
DEPTH = 4
DEC_BATCH = 4
DEC_SEQ = 1024
PAST_LEN = 256
GRID_W = 64
GRID_ROWS = DEC_SEQ // GRID_W
BRANCH_WIDTH = 512
N_BRANCH = 3
NA_HEADS = 4
NA_HEAD_DIM = 128
NA_WIN_ROWS = 8
NA_WIN_COLS = 16
NA_REL_ROWS = 2 * NA_WIN_ROWS - 1
NA_REL_COLS = 2 * NA_WIN_COLS - 1
DIFF_HEADS = 4
DIFF_QK_DIM = 64
DIFF_V_DIM = 128
RWKV_HEADS = 8
RWKV_HEAD_DIM = 64
RWKV_WIDTH = RWKV_HEADS * RWKV_HEAD_DIM
RWKV_DECAY_RANK = 64
RWKV_ICL_RANK = 64
RWKV_GATE_RANK = 128
RWKV_FEAT = 3 * RWKV_WIDTH + 2 * RWKV_DECAY_RANK + 2 * RWKV_ICL_RANK + RWKV_GATE_RANK
RWKV_GN_EPS = 64e-5
ATT_WIDTH = 6 * BRANCH_WIDTH
GATE_WIDTH = N_BRANCH * D_MODEL
FFN_HIDDEN = -(-8 * D_MODEL // (3 * 256)) * 256
ROPE_THETA = 10000.0
LN_EPS = 1e-5
ALPHA = (2.0 * DEPTH) ** 0.25

N_CTX = BATCH * SEQ
N_LAT = DEC_BATCH * DEC_SEQ
N_TOK = N_CTX + N_LAT
N_COND = 8
RWKV_CHUNK = 64
RWKV_SUPER = 256
RWKV_HEADS_PER_STEP = 4
RWKV_PASSES_PAIR = 1
RWKV_PASSES_SOLVE = 1
RWKV_PASSES_APPLY = 3
RWKV_PASSES_OUT = 1
RWKV_PASSES_STATE = 3
NEG_BIG = -1e30

NN = ((1,), (0,))
NT = ((1,), (1,))
TN = ((0,), (0,))


def _cparams(sem, vmem_mb=48):
    return pltpu.CompilerParams(dimension_semantics=sem, vmem_limit_bytes=vmem_mb * 1024 * 1024)


def _dot(a, b, dims=NN):
    return lax.dot_general(a, b, (dims, ((), ())), preferred_element_type=F32)


def _split2(x):
    hi = x.astype(BF16)
    lo = (x - hi.astype(F32)).astype(BF16)
    return hi, lo


def _dot3(a, b, dims=NN):
    ah, al = _split2(a)
    bh, bl = _split2(b)
    return _dot(ah, bh, dims) + (_dot(ah, bl, dims) + _dot(al, bh, dims))


def _dotp(a, b, dims, passes):
    if passes == 1:
        return _dot(a.astype(BF16), b.astype(BF16), dims)
    assert passes == 3
    return _dot3(a, b, dims)


def _dot_exact_lhs(a_bf16, b, dims=NN):
    b1 = b.astype(BF16)
    r1 = b - b1.astype(F32)
    b2 = r1.astype(BF16)
    b3 = (r1 - b2.astype(F32)).astype(BF16)
    return _dot(a_bf16, b1, dims) + (_dot(a_bf16, b2, dims) + _dot(a_bf16, b3, dims))


def _dot_exact_rhs(a, b_bf16, dims=NN):
    a1 = a.astype(BF16)
    r1 = a - a1.astype(F32)
    a2 = r1.astype(BF16)
    a3 = (r1 - a2.astype(F32)).astype(BF16)
    return _dot(a1, b_bf16, dims) + (_dot(a2, b_bf16, dims) + _dot(a3, b_bf16, dims))


def _normalize(x):
    mu = jnp.mean(x, axis=-1, keepdims=True)
    xc = x - mu
    var = jnp.mean(xc * xc, axis=-1, keepdims=True)
    return xc * lax.rsqrt(var + LN_EPS)


def _cond_of_row(row):
    return jnp.where(row < N_CTX, 0, 1 + (row - N_CTX) // DEC_SEQ)


def _adaln_kernel(c_ref, w_ref, b_ref, o_ref):
    c = c_ref[...]
    s = (c * jax.nn.sigmoid(c)).astype(BF16)
    o_ref[...] = _dot(s, w_ref[...].astype(BF16)) + b_ref[...]


def _adaln_all(cond, w_ada, b_ada):
    tn = 1024
    n = 6 * D_MODEL
    return pl.pallas_call(
        _adaln_kernel,
        grid=(DEPTH, n // tn),
        in_specs=[
            pl.BlockSpec((N_COND, D_MODEL), lambda l, j: (0, 0)),
            pl.BlockSpec((None, D_MODEL, tn), lambda l, j: (l, 0, j)),
            pl.BlockSpec((None, 1, tn), lambda l, j: (l, 0, j)),
        ],
        out_specs=pl.BlockSpec((None, N_COND, tn), lambda l, j: (l, 0, j)),
        out_shape=jax.ShapeDtypeStruct((DEPTH, N_COND, n), F32),
        compiler_params=_cparams(("parallel", "parallel")),
        name="adaln",
    )(cond, w_ada, b_ada.reshape(DEPTH, 1, n))


def _mod_spec(layer, which, tm, blk0=0):
    base = layer * 6 * N_COND + which * N_COND
    return pl.BlockSpec((None, 1, D_MODEL), lambda i: (base + _cond_of_row((blk0 + i) * tm), 0, 0))


def _modulate_kernel(x_ref, sh_ref, sc_ref, h_ref):
    h_ref[...] = (_normalize(x_ref[...]) * (1.0 + sc_ref[...]) + sh_ref[...]).astype(h_ref.dtype)


def _modulate(x, mods, layer):
    tm = 256
    return pl.pallas_call(
        _modulate_kernel,
        grid=(N_TOK // tm,),
        in_specs=[
            pl.BlockSpec((tm, D_MODEL), lambda i: (i, 0)),
            _mod_spec(layer, 0, tm),
            _mod_spec(layer, 1, tm),
        ],
        out_specs=pl.BlockSpec((tm, D_MODEL), lambda i: (i, 0)),
        out_shape=jax.ShapeDtypeStruct((N_TOK, D_MODEL), BF16),
        compiler_params=_cparams(("parallel",)),
        name="modulate",
    )(x, mods, mods)


def _proj_res_ln_kernel(m_ref, w_ref, x_ref, gate_ref, g_ref, b_ref, *rest, with_mod, n_sub):
    if with_mod:
        sh_ref, sc_ref, xo_ref, h_ref = rest
    else:
        (xo_ref,) = rest
    sub = m_ref.shape[0] // n_sub

    def norm_rows(s, z):
        rows = slice(s * sub, (s + 1) * sub)
        y = ALPHA * x_ref[rows, :] + gate_ref[...] * z
        xn = _normalize(y) * g_ref[...] + b_ref[...]
        xo_ref[rows, :] = xn
        if with_mod:
            h_ref[rows, :] = (_normalize(xn) * (1.0 + sc_ref[...]) + sh_ref[...]).astype(h_ref.dtype)

    z_prev = _dot(m_ref[0:sub, :], w_ref[...])
    for s in range(1, n_sub):
        z_next = _dot(m_ref[s * sub:(s + 1) * sub, :], w_ref[...])
        norm_rows(s - 1, z_prev)
        z_prev = z_next
    norm_rows(n_sub - 1, z_prev)


def _proj_res_ln(m, w16, x, mods, layer, gate_idx, ln_g, ln_b, mod_layer, mod_idx, *, tm, n_sub, name, rows=(0, N_TOK)):
    k = m.shape[1]
    with_mod = mod_layer is not None
    blk0, n_rows = rows[0] // tm, rows[1]
    vec = pl.BlockSpec((1, D_MODEL), lambda i: (0, 0))
    row = pl.BlockSpec((tm, D_MODEL), lambda i: (i, 0))
    in_specs = [
        pl.BlockSpec((tm, k), lambda i: (blk0 + i, 0)),
        pl.BlockSpec((None, k, D_MODEL), lambda i: (layer, 0, 0), pipeline_mode=pl.Buffered(1)),
        pl.BlockSpec((tm, D_MODEL), lambda i: (blk0 + i, 0)),
        _mod_spec(layer, gate_idx, tm, blk0), vec, vec,
    ]
    args = [m, w16, x, mods, ln_g.reshape(1, D_MODEL), ln_b.reshape(1, D_MODEL)]
    out_specs = [row]
    out_shape = [jax.ShapeDtypeStruct((n_rows, D_MODEL), F32)]
    if with_mod:
        in_specs += [_mod_spec(mod_layer, mod_idx, tm, blk0), _mod_spec(mod_layer, mod_idx + 1, tm, blk0)]
        args += [mods, mods]
        out_specs.append(row)
        out_shape.append(jax.ShapeDtypeStruct((n_rows, D_MODEL), BF16))
    out = pl.pallas_call(
        functools.partial(_proj_res_ln_kernel, with_mod=with_mod, n_sub=n_sub),
        grid=(n_rows // tm,),
        in_specs=in_specs,
        out_specs=out_specs,
        out_shape=out_shape,
        compiler_params=_cparams(("parallel",), 60),
        name=name,
    )(*args)
    return (out[0], out[1]) if with_mod else (out[0], None)


def _mm_w32_kernel(a_ref, w_ref, o_ref, w16_ref, *, sigmoid):
    @pl.when(pl.program_id(1) == 0)
    def _():
        w16_ref[...] = w_ref[0].astype(BF16)

    acc = _dot(a_ref[...], w16_ref[...])
    if sigmoid:
        acc = jax.nn.sigmoid(acc)
    o_ref[...] = acc.astype(o_ref.dtype)


def _matmul_w32(a, w, layer, out_dtype, col0=0, n=None, tm=1024, tn=1024, name="matmul", sigmoid=False):
    m, k = a.shape
    n = w.shape[2] if n is None else n
    assert m % tm == 0 and n % tn == 0
    return pl.pallas_call(
        functools.partial(_mm_w32_kernel, sigmoid=sigmoid),
        grid=(n // tn, m // tm),
        in_specs=[
            pl.BlockSpec((tm, k), lambda j, i: (i, 0)),
            pl.BlockSpec((pl.Element(1), pl.Element(k), pl.Element(tn)),
                         lambda j, i: (layer, 0, pl.multiple_of(col0 + j * tn, 128))),
        ],
        out_specs=pl.BlockSpec((tm, tn), lambda j, i: (i, j)),
        out_shape=jax.ShapeDtypeStruct((m, n), out_dtype),
        scratch_shapes=[pltpu.VMEM((k, tn), BF16)],
        compiler_params=_cparams(("parallel", "arbitrary")),
        name=name,
    )(a, w)


def _swiglu_kernel(a_ref, wg_ref, wu_ref, o_ref, wg16_ref, wu16_ref):
    @pl.when(pl.program_id(1) == 0)
    def _():
        wg16_ref[...] = wg_ref[...].astype(BF16)
        wu16_ref[...] = wu_ref[...].astype(BF16)

    a = a_ref[...]
    g = _dot(a, wg16_ref[...])
    u = _dot(a, wu16_ref[...])
    o_ref[...] = (g * jax.nn.sigmoid(g) * u).astype(o_ref.dtype)


def _ffn_in(h, w_ffn_in, layer):
    tm, tn = 1024, 512
    nb = FFN_HIDDEN // tn
    return pl.pallas_call(
        _swiglu_kernel,
        grid=(nb, N_TOK // tm),
        in_specs=[
            pl.BlockSpec((tm, D_MODEL), lambda j, i: (i, 0)),
            pl.BlockSpec((None, D_MODEL, tn), lambda j, i: (layer, 0, j)),
            pl.BlockSpec((None, D_MODEL, tn), lambda j, i: (layer, 0, j + nb)),
        ],
        out_specs=pl.BlockSpec((tm, tn), lambda j, i: (i, j)),
        out_shape=jax.ShapeDtypeStruct((N_TOK, FFN_HIDDEN), BF16),
        scratch_shapes=[pltpu.VMEM((D_MODEL, tn), BF16), pltpu.VMEM((D_MODEL, tn), BF16)],
        compiler_params=_cparams(("parallel", "arbitrary")),
        name="ffn_in",
    )(h, w_ffn_in, w_ffn_in)


def _merge_kernel(oa_ref, ob_ref, oc_ref, ga_ref, gb_ref, gc_ref, wb_ref, o_ref):
    acc = ga_ref[...].astype(F32) * _dot(oa_ref[...], wb_ref[0])
    acc += gb_ref[...].astype(F32) * _dot(ob_ref[...], wb_ref[1])
    acc += gc_ref[...].astype(F32) * _dot(oc_ref[...], wb_ref[2])
    o_ref[...] = acc.astype(o_ref.dtype)


def _merge(oa, ob, oc, gate_pre, w_branch, layer):
    tm, tn = 1024, 512
    nb = D_MODEL // tn
    o_spec = pl.BlockSpec((tm, BRANCH_WIDTH), lambda i, j: (i, 0))
    return pl.pallas_call(
        _merge_kernel,
        grid=(N_TOK // tm, nb),
        in_specs=[
            o_spec, o_spec, o_spec,
            pl.BlockSpec((tm, tn), lambda i, j: (i, j)),
            pl.BlockSpec((tm, tn), lambda i, j: (i, j + nb)),
            pl.BlockSpec((tm, tn), lambda i, j: (i, j + 2 * nb)),
            pl.BlockSpec((None, N_BRANCH, BRANCH_WIDTH, tn), lambda i, j: (layer, 0, 0, j)),
        ],
        out_specs=pl.BlockSpec((tm, tn), lambda i, j: (i, j)),
        out_shape=jax.ShapeDtypeStruct((N_TOK, D_MODEL), BF16),
        compiler_params=_cparams(("parallel", "parallel")),
        name="merge",
    )(oa, ob, oc, gate_pre, gate_pre, gate_pre, w_branch)


def _softmax_rows(s):
    m = jnp.max(s, axis=-1, keepdims=True)
    e = jnp.exp(s - m)
    return e * (1.0 / jnp.sum(e, axis=-1, keepdims=True))


def _na_ctx_kernel(q_ref, k_ref, v_ref, *rest):
    o_ref, new_k_ref, new_v_ref = rest[-3:]
    scale = NA_HEAD_DIM ** -0.5
    _store_heads(new_k_ref, k_ref)
    _store_heads(new_v_ref, v_ref)
    heads = range(NA_HEADS)
    sls = [slice(h * NA_HEAD_DIM, (h + 1) * NA_HEAD_DIM) for h in heads]
    s = [_dot(q_ref[:, sl].astype(BF16), k_ref[:, sl].astype(BF16), NT) * scale for sl in sls]
    p = [_softmax_rows(s[h]).astype(BF16) for h in heads]
    o = [_dot(p[h], v_ref[:, sls[h]].astype(BF16)) for h in heads]
    for h in heads:
        o_ref[:, sls[h]] = o[h].astype(o_ref.dtype)


_ANY_SPEC = pl.BlockSpec(memory_space=pl.ANY)


def _cache_stack_shape(split_heads):
    tail = (NA_HEADS, NA_HEAD_DIM) if split_heads else (BRANCH_WIDTH,)
    return jax.ShapeDtypeStruct((BATCH, DEPTH, SEQ) + tail, F32)


def _cache_stack_spec(split_heads, index_map):
    tail = (NA_HEADS, NA_HEAD_DIM) if split_heads else (BRANCH_WIDTH,)
    return pl.BlockSpec((None, None, SEQ) + tail, index_map)


def _store_heads(dst_ref, src_ref):
    for h in range(NA_HEADS):
        dst_ref[:, h, :] = src_ref[:, h * NA_HEAD_DIM:(h + 1) * NA_HEAD_DIM]


def _na_ctx(p_att, layer, oa, stacks):
    w = BRANCH_WIDTH
    stack_spec = _cache_stack_spec(True, lambda n: (n, layer, 0, 0, 0))
    return pl.pallas_call(
        _na_ctx_kernel,
        grid=(BATCH,),
        in_specs=[
            pl.BlockSpec((SEQ, w), lambda n: (n, 0)),
            pl.BlockSpec((SEQ, w), lambda n: (n, 1)),
            pl.BlockSpec((SEQ, w), lambda n: (n, 2)),
            _ANY_SPEC, _ANY_SPEC, _ANY_SPEC,
        ],
        out_specs=[pl.BlockSpec((SEQ, w), lambda n: (n, 0)), stack_spec, stack_spec],
        out_shape=[jax.ShapeDtypeStruct((N_TOK, w), BF16), _cache_stack_shape(True), _cache_stack_shape(True)],
        input_output_aliases={3: 0, 4: 1, 5: 2},
        compiler_params=_cparams(("parallel",)),
        name="na_ctx",
    )(p_att, p_att, p_att, oa, *stacks)


def _na_row_start(r):
    return jnp.clip(r - NA_WIN_ROWS // 2, 0, GRID_ROWS - NA_WIN_ROWS)


def _na_lat_kernel(q_ref, k_ref, v_ref, kc_ref, vc_ref, bias_ref, _, o_ref):
    scale = NA_HEAD_DIM ** -0.5
    n_loc = NA_WIN_ROWS * GRID_W
    start = pl.multiple_of(_na_row_start(pl.program_id(1)) * GRID_W, GRID_W)
    k_loc = k_ref[pl.ds(start, n_loc), :].astype(BF16)
    v_loc = v_ref[pl.ds(start, n_loc), :].astype(BF16)
    heads = range(NA_HEADS)
    sls = [slice(h * NA_HEAD_DIM, (h + 1) * NA_HEAD_DIM) for h in heads]
    q = [q_ref[:, sl].astype(BF16) for sl in sls]
    s_loc = [_dot(q[h], k_loc[:, sls[h]], NT) * scale + bias_ref[h] for h in heads]
    s_ctx = [_dot(q[h], kc_ref[:, sls[h]].astype(BF16), NT) * scale for h in heads]
    m = [jnp.maximum(jnp.max(s_loc[h], axis=-1, keepdims=True), jnp.max(s_ctx[h], axis=-1, keepdims=True))
         for h in heads]
    e_loc = [jnp.exp(s_loc[h] - m[h]) for h in heads]
    e_ctx = [jnp.exp(s_ctx[h] - m[h]) for h in heads]
    inv = [1.0 / (jnp.sum(e_loc[h], axis=-1, keepdims=True) + jnp.sum(e_ctx[h], axis=-1, keepdims=True))
           for h in heads]
    o_loc = [_dot((e_loc[h] * inv[h]).astype(BF16), v_loc[:, sls[h]]) for h in heads]
    o_ctx = [_dot((e_ctx[h] * inv[h]).astype(BF16), vc_ref[:, sls[h]].astype(BF16)) for h in heads]
    for h in heads:
        o_ref[:, sls[h]] = (o_loc[h] + o_ctx[h]).astype(o_ref.dtype)


def _na_bias_kernel(rows_ref, onehot_ref, mask_ref, o_ref):
    o_ref[...] = _dot_exact_rhs(rows_ref[...], onehot_ref[...]) + mask_ref[...]


def _na_bias_table(rpb):
    nr, nd = NA_WIN_ROWS, NA_REL_COLS + 1
    rows = jnp.stack([rpb[:, nr - 1 - p:2 * nr - 1 - p, :] for p in range(nr)], axis=1)
    rows = jnp.pad(rows.astype(F32), ((0, 0), (0, 0), (0, 0), (0, nd - NA_REL_COLS)))
    qc = np.arange(GRID_W)[:, None]
    kc = np.arange(GRID_W)[None, :]
    rel_c = np.clip(kc - qc, -(NA_WIN_COLS - 1), NA_WIN_COLS - 1) + NA_WIN_COLS - 1
    onehot = (rel_c[None] == np.arange(nd)[:, None, None]).reshape(nd, GRID_W * GRID_W)
    win_c0 = np.clip(qc - NA_WIN_COLS // 2, 0, GRID_W - NA_WIN_COLS)
    valid = ((kc >= win_c0) & (kc < win_c0 + NA_WIN_COLS)).reshape(1, GRID_W * GRID_W)
    n_rows = NA_HEADS * nr * nr
    full = lambda shape: pl.BlockSpec(shape, lambda: (0,) * len(shape))
    bias = pl.pallas_call(
        _na_bias_kernel,
        in_specs=[full((n_rows, nd)), full((nd, GRID_W * GRID_W)), full((1, GRID_W * GRID_W))],
        out_specs=full((n_rows, GRID_W * GRID_W)),
        out_shape=jax.ShapeDtypeStruct((n_rows, GRID_W * GRID_W), F32),
        name="na_bias",
    )(rows.reshape(n_rows, nd), jnp.asarray(onehot, BF16), jnp.asarray(np.where(valid, 0.0, NEG_BIG), F32))
    bias = bias.reshape(NA_HEADS, nr, nr, GRID_W, GRID_W).transpose(0, 1, 3, 2, 4)
    return bias.reshape(NA_HEADS, nr, GRID_W, nr * GRID_W)


def _na_lat(p_att, cache_k, cache_v, layer, bias, oa):
    w = BRANCH_WIDTH
    blk0 = N_CTX // DEC_SEQ
    q0 = N_CTX // GRID_W
    rows_per = DEC_SEQ // GRID_W
    kv_cache = pl.BlockSpec((None, None, PAST_LEN, w), lambda n, r: (n, layer, 0, 0))
    return pl.pallas_call(
        _na_lat_kernel,
        grid=(DEC_BATCH, GRID_ROWS),
        in_specs=[
            pl.BlockSpec((GRID_W, w), lambda n, r: (q0 + n * rows_per + r, 0)),
            pl.BlockSpec((DEC_SEQ, w), lambda n, r: (blk0 + n, 1)),
            pl.BlockSpec((DEC_SEQ, w), lambda n, r: (blk0 + n, 2)),
            kv_cache, kv_cache,
            pl.BlockSpec((NA_HEADS, None, GRID_W, NA_WIN_ROWS * GRID_W),
                         lambda n, r: (0, r - _na_row_start(r), 0, 0)),
            _ANY_SPEC,
        ],
        out_specs=pl.BlockSpec((GRID_W, w), lambda n, r: (q0 + n * rows_per + r, 0)),
        out_shape=jax.ShapeDtypeStruct((N_TOK, w), BF16),
        input_output_aliases={6: 0},
        compiler_params=_cparams(("parallel", "arbitrary")),
        name="na_lat",
    )(p_att, p_att, p_att, cache_k, cache_v, bias, oa)


def _rope(x, cos, sin):
    lane = lax.broadcasted_iota(jnp.int32, x.shape, 1)
    first = (lane % 32) < 16
    rot = jnp.where(first, -pltpu.roll(x, 128 - 16, 1), pltpu.roll(x, 16, 1))
    return x * cos + rot * sin


def _diff_kernel(*refs, rope, cache, lam_init, tq):
    if cache:
        q_ref, k_ref, v_ref, kc_ref, vc_ref, cos_ref, sin_ref, lam_ref, g_ref, _, o_ref = refs
    else:
        q_ref, k_ref, v_ref, lam_ref, g_ref = refs[:5]
        o_ref, new_k_ref, new_v_ref = refs[-3:]
        new_k_ref[...] = k_ref[...]
        _store_heads(new_v_ref, v_ref)
    scale = DIFF_QK_DIM ** -0.5
    lp = lam_ref[...]
    lam = (jnp.exp(jnp.sum(lp[0:1] * lp[1:2], axis=-1, keepdims=True))
           - jnp.exp(jnp.sum(lp[2:3] * lp[3:4], axis=-1, keepdims=True)) + lam_init)
    hd = DIFF_V_DIM
    t = q_ref.shape[0]
    n_heads = q_ref.shape[1] // hd
    ks, vs = [], []
    for hh in range(n_heads):
        sl = slice(hh * hd, (hh + 1) * hd)
        k = k_ref[:, sl]
        v = v_ref[:, sl]
        if rope:
            k = _rope(k, cos_ref[...], sin_ref[...])
        if cache:
            k = jnp.concatenate([k, kc_ref[:, sl]], axis=0)
            v = jnp.concatenate([v, vc_ref[:, sl]], axis=0)
        ks.append(k.astype(BF16))
        vs.append(v.astype(BF16))
    units = [(hh, i) for hh in range(n_heads) for i in range(t // tq)]
    rows = {u: slice(u[1] * tq, (u[1] + 1) * tq) for u in units}
    cols = {u: slice(u[0] * hd, (u[0] + 1) * hd) for u in units}
    q = {u: q_ref[rows[u], cols[u]] for u in units}
    if rope:
        q = {u: _rope(q[u], cos_ref[rows[u], :], sin_ref[rows[u], :]) for u in units}
    lane = lax.broadcasted_iota(jnp.int32, (tq, hd), 1)
    s1 = {u: _dot(jnp.where(lane < DIFF_QK_DIM, q[u], 0.0).astype(BF16), ks[u[0]], NT) * scale for u in units}
    s2 = {u: _dot(jnp.where(lane >= DIFF_QK_DIM, q[u], 0.0).astype(BF16), ks[u[0]], NT) * scale for u in units}
    w = {u: (_softmax_rows(s1[u]) - lam * _softmax_rows(s2[u])).astype(BF16) for u in units}
    o = {u: _dot(w[u], vs[u[0]]) for u in units}
    for u in units:
        on = o[u] * lax.rsqrt(jnp.mean(o[u] * o[u], axis=-1, keepdims=True) + LN_EPS)
        o_ref[rows[u], cols[u]] = (on * g_ref[:, cols[u]] * (1.0 - lam_init)).astype(o_ref.dtype)


def _diff_attn(p_att, lam_p, subln, lam_init, layer, ob, *, t, n_seq, blk0, cache=None, tables=None, stacks=None):
    hd = DIFF_V_DIM * (1 if cache is not None else DIFF_HEADS)
    q_col, k_col, v_col = (z * BRANCH_WIDTH // hd for z in (3, 4, 5))
    in_specs = [
        pl.BlockSpec((t, hd), lambda n, h: (blk0 + n, q_col + h)),
        pl.BlockSpec((t, hd), lambda n, h: (blk0 + n, k_col + h)),
        pl.BlockSpec((t, hd), lambda n, h: (blk0 + n, v_col + h)),
    ]
    args = [p_att, p_att, p_att]
    if cache is not None:
        cache_k, cache_v = cache
        tab = pl.BlockSpec((t, hd), lambda n, h: (0, 0))
        cache_spec = pl.BlockSpec((None, None, PAST_LEN, hd), lambda n, h: (n, layer, 0, h))
        in_specs += [cache_spec, cache_spec, tab, tab]
        args += [cache_k, cache_v, tables[0], tables[1]]
    in_specs += [
        pl.BlockSpec((4, DIFF_QK_DIM), lambda n, h: (0, 0)),
        pl.BlockSpec((1, hd), lambda n, h: (0, h)),
    ]
    args += [lam_p, subln.reshape(1, BRANCH_WIDTH)]
    out_specs = [pl.BlockSpec((t, hd), lambda n, h: (blk0 + n, h))]
    out_shape = [jax.ShapeDtypeStruct((N_TOK, BRANCH_WIDTH), BF16)]
    aliases = {len(args): 0}
    in_specs.append(_ANY_SPEC)
    args.append(ob)
    if cache is None:
        out_specs += [_cache_stack_spec(False, lambda n, h: (n, layer, 0, 0)),
                      _cache_stack_spec(True, lambda n, h: (n, layer, 0, 0, 0))]
        out_shape += [_cache_stack_shape(False), _cache_stack_shape(True)]
        aliases.update({len(args): 1, len(args) + 1: 2})
        in_specs += [_ANY_SPEC, _ANY_SPEC]
        args += list(stacks)
    return pl.pallas_call(
        functools.partial(_diff_kernel, rope=cache is not None, cache=cache is not None,
                          lam_init=lam_init, tq=256),
        grid=(n_seq, BRANCH_WIDTH // hd),
        in_specs=in_specs,
        out_specs=out_specs,
        out_shape=out_shape,
        input_output_aliases=aliases,
        compiler_params=_cparams(("parallel", "parallel")),
        name="diff_lat" if cache is not None else "diff_ctx",
    )(*args)


def _rope_tables():
    t = np.arange(DEC_SEQ)
    rows = (t // GRID_W).astype(np.float32)
    cols = (t % GRID_W).astype(np.float32)
    half = DIFF_QK_DIM // 2
    inv = jnp.asarray(ROPE_THETA, F32) ** (-jnp.arange(0, half, 2, dtype=F32) / half)
    ang_r = jnp.asarray(rows)[:, None] * inv
    ang_c = jnp.asarray(cols)[:, None] * inv
    ang = jnp.concatenate([ang_r, ang_r, ang_c, ang_c] * 2, axis=-1)
    return jnp.cos(ang), jnp.sin(ang)


def _head_sum(x, ones_bd):
    return _dot_exact_rhs(x, ones_bd)


def _rwkv_prep_kernel(cur_ref, prev_ref, next_ref, mix_ref, w0_ref, w2_ref, a0_ref, a2_ref, g2_ref,
                      kk_ref, ka_ref, rk_ref, ones_ref,
                      r_ref, v_ref, kkn_ref, lwf_ref, lwb_ref, kf_ref, kb_ref, bf_ref, bb_ref, bonus_ref, g_ref,
                      *, tm):
    i = pl.program_id(0)
    blocks_per_seq = DEC_SEQ // tm
    j = i - N_CTX // tm
    is_ctx = i < N_CTX // tm
    at_start = jnp.logical_or(is_ctx, j % blocks_per_seq == 0)
    at_end = jnp.logical_or(is_ctx, j % blocks_per_seq == blocks_per_seq - 1)
    cur = cur_ref[...]
    row = lax.broadcasted_iota(jnp.int32, cur.shape, 0)
    prev_row = jnp.where(at_start, 0.0, prev_ref[7:8, :])
    next_row = jnp.where(at_end, 0.0, next_ref[0:1, :])
    prev = jnp.where(row == 0, prev_row, pltpu.roll(cur, 1, 0))
    nxt = jnp.where(row == tm - 1, next_row, pltpu.roll(cur, tm - 1, 0))
    f = cur + mix_ref[0:1, :] * (prev - cur) + mix_ref[1:2, :] * (nxt - cur)

    w = RWKV_WIDTH
    r = f[:, 0:w]
    k = f[:, w:2 * w]
    v = f[:, 2 * w:3 * w]
    c0 = 3 * w
    wd = f[:, c0:c0 + 2 * RWKV_DECAY_RANK]
    ad = f[:, c0 + 2 * RWKV_DECAY_RANK:c0 + 2 * RWKV_DECAY_RANK + 2 * RWKV_ICL_RANK]
    gd = f[:, c0 + 2 * RWKV_DECAY_RANK + 2 * RWKV_ICL_RANK:]
    ones_bd = ones_ref[...]

    kkv = k * kk_ref[...]
    norm = jnp.sqrt(_head_sum(kkv * kkv, ones_bd))
    kkn = kkv / jnp.maximum(norm, 1e-12)
    wlin = _dot(jnp.tanh(wd).astype(BF16), w2_ref[...]) + w0_ref[...]
    alin = _dot(ad.astype(BF16), a2_ref[...]) + a0_ref[...]
    g = _dot(jax.nn.sigmoid(gd).astype(BF16), g2_ref[...])

    r_ref[...] = r
    v_ref[...] = v
    kkn_ref[...] = kkn
    g_ref[...] = g.astype(g_ref.dtype)
    bonus = jnp.zeros_like(v)
    for d, (lw_ref, kd_ref, bd_ref) in enumerate(((lwf_ref, kf_ref, bf_ref), (lwb_ref, kb_ref, bb_ref))):
        z = -wlin[:, d * w:(d + 1) * w]
        softplus = jnp.maximum(z, 0.0) + jnp.log(1.0 + jnp.exp(-jnp.abs(z)))
        lw_ref[...] = -jnp.exp(-softplus - 0.5)
        a = jax.nn.sigmoid(alin[:, d * w:(d + 1) * w])
        k_d = k * (1.0 + (a - 1.0) * ka_ref[...])
        kd_ref[...] = k_d
        bd_ref[...] = kkn * a
        bonus += _head_sum(r * k_d * rk_ref[...], ones_bd) * v
    bonus_ref[...] = bonus.astype(bonus_ref.dtype)


def _block_diag2(a, b):
    za = jnp.zeros((a.shape[0], b.shape[1]), a.dtype)
    zb = jnp.zeros((b.shape[0], a.shape[1]), b.dtype)
    return jnp.concatenate([jnp.concatenate([a, za], axis=1), jnp.concatenate([zb, b], axis=1)], axis=0)


def _rwkv_prep(p_rw, lp):
    tm = 256
    w = RWKV_WIDTH
    sub = tm // 8
    full = lambda shape: pl.BlockSpec(shape, lambda i: (0,) * len(shape))
    out_spec = pl.BlockSpec((tm, w), lambda i: (i, 0))
    ones_bd = jnp.asarray(np.kron(np.eye(RWKV_HEADS), np.ones((RWKV_HEAD_DIM, RWKV_HEAD_DIM))), BF16)
    w2_bd = _block_diag2(lp['rwkv_w2'][0], lp['rwkv_w2'][1]).astype(BF16)
    a2_bd = _block_diag2(lp['rwkv_a2'][0], lp['rwkv_a2'][1]).astype(BF16)
    n_out = 11
    return pl.pallas_call(
        functools.partial(_rwkv_prep_kernel, tm=tm),
        grid=(N_TOK // tm,),
        in_specs=[
            pl.BlockSpec((tm, RWKV_FEAT), lambda i: (i, 0)),
            pl.BlockSpec((8, RWKV_FEAT), lambda i: (jnp.maximum(i * sub - 1, 0), 0)),
            pl.BlockSpec((8, RWKV_FEAT), lambda i: (jnp.minimum((i + 1) * sub, N_TOK // 8 - 1), 0)),
            full((2, RWKV_FEAT)),
            full((1, 2 * w)), full((2 * RWKV_DECAY_RANK, 2 * w)),
            full((1, 2 * w)), full((2 * RWKV_ICL_RANK, 2 * w)),
            full((RWKV_GATE_RANK, w)),
            full((1, w)), full((1, w)), full((1, w)),
            full((w, w)),
        ],
        out_specs=[out_spec] * n_out,
        out_shape=[jax.ShapeDtypeStruct((N_TOK, w), F32)] * (n_out - 2) + [jax.ShapeDtypeStruct((N_TOK, w), BF16)] * 2,
        compiler_params=_cparams(("parallel",)),
        name="rwkv_prep",
    )(p_rw, p_rw, p_rw, lp['rwkv_mix'],
      lp['rwkv_w0'].reshape(1, 2 * w), w2_bd, lp['rwkv_a0'].reshape(1, 2 * w), a2_bd,
      lp['rwkv_g2'].astype(BF16),
      lp['rwkv_kk'].reshape(1, w), lp['rwkv_ka'].reshape(1, w), lp['rwkv_rk'].reshape(1, w),
      ones_bd)


def _rwkv_chunk_maps(chains):
    n = len(chains)
    rs, ks, vs, kkns, bs, lws, fwds = (list(z) for z in zip(*chains))
    c, nk = rs[0].shape
    row = lax.broadcasted_iota(jnp.int32, (c, c), 0)
    col = lax.broadcasted_iota(jnp.int32, (c, c), 1)
    eye = row == col
    incl_d = {True: row >= col, False: row <= col}
    strict_d = {True: row > col, False: row < col}
    tri_d = {f: jnp.where(m, 1.0, 0.0).astype(BF16) for f, m in incl_d.items()}
    idx = range(n)
    cum = [_dot_exact_lhs(tri_d[fwds[i]], lws[i]) for i in idx]
    tot = [cum[i][c - 1:c, :] if fwds[i] else cum[i][0:1, :] for i in idx]
    e_neg = [jnp.exp(-cum[i]) for i in idx]
    a_t = [-kkns[i] * jnp.exp(cum[i] - lws[i]) for i in idx]
    r_t = [rs[i] * jnp.exp(cum[i]) for i in idx]
    ar = [jnp.concatenate([a_t[i], r_t[i]], axis=0) for i in idx]
    with_b = [_dotp(ar[i], bs[i] * e_neg[i], NT, RWKV_PASSES_PAIR) for i in idx]
    with_k = [_dotp(ar[i], ks[i] * e_neg[i], NT, RWKV_PASSES_PAIR) for i in idx]
    l_ab = [jnp.where(strict_d[fwds[i]], with_b[i][:c], 0.0) for i in idx]
    l_ak = [jnp.where(strict_d[fwds[i]], with_k[i][:c], 0.0) for i in idx]
    m_rb = [jnp.where(incl_d[fwds[i]], with_b[i][c:], 0.0) for i in idx]
    m_rk = [jnp.where(incl_d[fwds[i]], with_k[i][c:], 0.0) for i in idx]
    lakv = [_dotp(l_ak[i], vs[i], NN, RWKV_PASSES_APPLY) for i in idx]
    mrkv = [_dotp(m_rk[i], vs[i], NN, RWKV_PASSES_OUT) for i in idx]
    e_tail = [jnp.exp(tot[i] - cum[i]) for i in idx]
    kwv = [_dotp(ks[i] * e_tail[i], vs[i], TN, RWKV_PASSES_APPLY) for i in idx]
    same = lambda s: (row // s) == (col // s)
    inv = [jnp.where(eye, 1.0, 0.0) + jnp.where(same(2), l_ab[i], 0.0) for i in idx]
    size = 4
    while size <= c:
        part = jnp.logical_and(same(size), jnp.logical_not(same(size // 2)))
        half = [_dotp(inv[i], jnp.where(part, l_ab[i], 0.0), NN, RWKV_PASSES_SOLVE) for i in idx]
        inv = [inv[i] + _dotp(half[i], inv[i], NN, RWKV_PASSES_SOLVE) for i in idx]
        size *= 2
    pq1 = [_dotp(inv[i], jnp.concatenate([a_t[i], lakv[i]], axis=1), NN, RWKV_PASSES_APPLY) for i in idx]
    rb = [_dotp(m_rb[i], pq1[i], NN, RWKV_PASSES_OUT) for i in idx]
    sb = [_dotp(bs[i] * e_tail[i], pq1[i], TN, RWKV_PASSES_APPLY) for i in idx]
    out = []
    for i in idx:
        p2 = r_t[i] + rb[i][:, :nk]
        q2 = rb[i][:, nk:] + mrkv[i]
        p3 = jnp.where(eye, jnp.exp(tot[i]), 0.0) + sb[i][:, :nk]
        q3 = sb[i][:, nk:] + kwv[i]
        out.append((p2, p3, q2, q3))
    return out


def _rwkv_scan_kernel(rf, vf, af, wf, kf, bf, rb, vb, ab, wb, kb, bb, s0f, s0b, *rest, n_super, n_chunk):
    yf, yb, sff, sfb, st_ref = rest[-5:]
    sup = pl.program_id(2)
    hd = RWKV_HEAD_DIM
    cs = RWKV_CHUNK
    hp = RWKV_HEADS_PER_STEP

    @pl.when(sup == 0)
    def _():
        st_ref[0:hp] = s0f[...]
        st_ref[hp:2 * hp] = s0b[...]

    dirs = ((rf, vf, af, wf, kf, bf, yf, True), (rb, vb, ab, wb, kb, bb, yb, False))
    keys, chains = [], []
    for d, (r_ref, v_ref, a_ref, w_ref, k_ref, b_ref, _, forward) in enumerate(dirs):
        for ci in range(n_chunk):
            rows = slice(ci * cs, (ci + 1) * cs)
            for hh in range(hp):
                sl = slice(hh * hd, (hh + 1) * hd)
                keys.append((d, ci, hh))
                chains.append((r_ref[rows, sl], k_ref[rows, sl], v_ref[rows, sl],
                               a_ref[rows, sl], b_ref[rows, sl], w_ref[rows, sl], forward))
    maps = dict(zip(keys, _rwkv_chunk_maps(chains)))

    seqs = [(d, hh) for d in range(2) for hh in range(hp)]
    st = {s: st_ref[hp * s[0] + s[1]] for s in seqs}
    ys = {}
    for step in range(n_chunk):
        for d, hh in seqs:
            ci = step if dirs[d][-1] else n_chunk - 1 - step
            p2, p3, q2, q3 = maps[d, ci, hh]
            ys[d, ci, hh] = _dotp(p2, st[d, hh], NN, RWKV_PASSES_OUT) + q2
            st[d, hh] = _dotp(p3, st[d, hh], NN, RWKV_PASSES_STATE) + q3
    for d, hh in seqs:
        st_ref[hp * d + hh] = st[d, hh]
    for d in range(2):
        y_ref = dirs[d][6]
        for ci in range(n_chunk):
            y_ref[ci * cs:(ci + 1) * cs, :] = jnp.concatenate([ys[d, ci, hh] for hh in range(hp)], axis=1)

    @pl.when(sup == n_super - 1)
    def _():
        sff[...] = st_ref[0:hp]
        sfb[...] = st_ref[hp:2 * hp]


def _rwkv_scan(feats, s0f_t, s0b_t, y_prev, *, t, n_seq, row0):
    r, v, kkn, lwf, lwb, kf, kb, bf, bb = feats
    rows = RWKV_SUPER
    n_super = t // rows
    blk0 = row0 // rows
    hp = RWKV_HEADS_PER_STEP
    width = hp * RWKV_HEAD_DIM
    fwd = pl.BlockSpec((rows, width), lambda n, p, s: (blk0 + n * n_super + s, p))
    bwd = pl.BlockSpec((rows, width), lambda n, p, s: (blk0 + n * n_super + n_super - 1 - s, p))
    st_spec = pl.BlockSpec((None, hp, RWKV_HEAD_DIM, RWKV_HEAD_DIM), lambda n, p, s: (n, p, 0, 0))
    st_shape = jax.ShapeDtypeStruct((n_seq, RWKV_HEADS, RWKV_HEAD_DIM, RWKV_HEAD_DIM), F32)
    y_shape = jax.ShapeDtypeStruct((N_TOK, RWKV_WIDTH), F32)
    extra = list(y_prev)
    n_in = 14
    return pl.pallas_call(
        functools.partial(_rwkv_scan_kernel, n_super=n_super, n_chunk=rows // RWKV_CHUNK),
        grid=(n_seq, RWKV_HEADS // hp, n_super),
        in_specs=[fwd] * 6 + [bwd] * 6 + [st_spec, st_spec] + [_ANY_SPEC] * len(extra),
        out_specs=[fwd, bwd, st_spec, st_spec],
        out_shape=[y_shape, y_shape, st_shape, st_shape],
        input_output_aliases={n_in: 0, n_in + 1: 1},
        scratch_shapes=[pltpu.VMEM((2 * hp, RWKV_HEAD_DIM, RWKV_HEAD_DIM), F32)],
        compiler_params=_cparams(("parallel", "parallel", "arbitrary")),
        name="rwkv_scan",
    )(r, v, kkn, lwf, kf, bf, r, v, kkn, lwb, kb, bb, s0f_t, s0b_t, *extra)


def _rwkv_mixer(p_rw, s_lat_f, s_lat_b, lp, y_bufs):
    r, v, kkn, lwf, lwb, kf, kb, bf, bb, bonus, g = _rwkv_prep(p_rw, lp)
    feats = (r, v, kkn, lwf, lwb, kf, kb, bf, bb)
    s_zero = jnp.zeros((BATCH, RWKV_HEADS, RWKV_HEAD_DIM, RWKV_HEAD_DIM), F32)
    y_f, y_b, sf_c, sb_c = _rwkv_scan(feats, s_zero, s_zero, y_bufs, t=SEQ, n_seq=BATCH, row0=0)
    y_f, y_b, _, _ = _rwkv_scan(feats, jnp.swapaxes(s_lat_f, -1, -2), jnp.swapaxes(s_lat_b, -1, -2), (y_f, y_b),
                                t=DEC_SEQ, n_seq=DEC_BATCH, row0=N_CTX)
    oc = _rwkv_out(y_f, y_b, bonus, g, lp['rwkv_lnx_g'], lp['rwkv_lnx_b'])
    return oc, jnp.swapaxes(sf_c, -1, -2), jnp.swapaxes(sb_c, -1, -2), (y_f, y_b)


def _rwkv_out_kernel(yf_ref, yb_ref, bonus_ref, g_ref, lg_ref, lb_ref, ones_ref, o_ref):
    ones_bd = ones_ref[...]
    y = yf_ref[...] + yb_ref[...]
    mu = _head_sum(y, ones_bd) * (1.0 / RWKV_HEAD_DIM)
    yc = y - mu
    var = _head_sum(yc * yc, ones_bd) * (1.0 / RWKV_HEAD_DIM)
    yn = yc * lax.rsqrt(var + RWKV_GN_EPS) * lg_ref[...] + lb_ref[...]
    o_ref[...] = ((yn + bonus_ref[...].astype(F32)) * g_ref[...].astype(F32)).astype(o_ref.dtype)


def _rwkv_out(y_f, y_b, bonus, g, lnx_g, lnx_b):
    tm = 512
    w = RWKV_WIDTH
    row = pl.BlockSpec((tm, w), lambda i: (i, 0))
    vec = pl.BlockSpec((1, w), lambda i: (0, 0))
    ones_bd = jnp.asarray(np.kron(np.eye(RWKV_HEADS), np.ones((RWKV_HEAD_DIM, RWKV_HEAD_DIM))), BF16)
    return pl.pallas_call(
        _rwkv_out_kernel,
        grid=(N_TOK // tm,),
        in_specs=[row, row, row, row, vec, vec, pl.BlockSpec((w, w), lambda i: (0, 0))],
        out_specs=row,
        out_shape=jax.ShapeDtypeStruct((N_TOK, w), BF16),
        compiler_params=_cparams(("parallel",)),
        name="rwkv_out",
    )(y_f, y_b, bonus, g, lnx_g.reshape(1, w), lnx_b.reshape(1, w), ones_bd)


def kernel(x_prompt, x_sample, cache_na_k, cache_na_v, cache_diff_k, cache_diff_v, state_rwkv_fwd, state_rwkv_bwd, c, c_ctx, w_ada, b_ada, w_in, na_rpb, diff_lambda, diff_subln, rwkv_mix, rwkv_w0, rwkv_w2, rwkv_a0, rwkv_a2, rwkv_g2, rwkv_kk, rwkv_ka, rwkv_rk, rwkv_lnx_g, rwkv_lnx_b, w_branch, w_out, ln1_g, ln1_b, w_ffn_in, w_ffn_out, ln2_g, ln2_b):
    x = jnp.concatenate([x_prompt.reshape(N_CTX, D_MODEL), x_sample.reshape(N_LAT, D_MODEL)], axis=0)
    cond = jnp.concatenate([c_ctx[None, :], c, jnp.zeros((N_COND - 1 - DEC_BATCH, D_MODEL), F32)], axis=0)
    mods = _adaln_all(cond, w_ada, b_ada)
    mods = mods.reshape(DEPTH, N_COND, 6, D_MODEL).transpose(0, 2, 1, 3).reshape(DEPTH * 6 * N_COND, 1, D_MODEL)

    cache_na_k = cache_na_k.reshape(DEC_BATCH, DEPTH, PAST_LEN, BRANCH_WIDTH)
    cache_na_v = cache_na_v.reshape(DEC_BATCH, DEPTH, PAST_LEN, BRANCH_WIDTH)
    cache_diff_k = cache_diff_k.reshape(DEC_BATCH, DEPTH, PAST_LEN, BRANCH_WIDTH)
    cache_diff_v = cache_diff_v.reshape(DEC_BATCH, DEPTH, PAST_LEN, BRANCH_WIDTH)
    rope_tables = _rope_tables()

    c1 = ATT_WIDTH + RWKV_FEAT
    w_branch16 = w_branch.astype(BF16)
    w_out16 = w_out.astype(BF16)
    w_ffn_out16 = w_ffn_out.astype(BF16)

    h = _modulate(x, mods, 0)
    oa = jnp.zeros((N_TOK, BRANCH_WIDTH), BF16)
    ob = jnp.zeros((N_TOK, BRANCH_WIDTH), BF16)
    y_bufs = (jnp.zeros((N_TOK, RWKV_WIDTH), F32), jnp.zeros((N_TOK, RWKV_WIDTH), F32))
    na_stacks = [jnp.zeros(_cache_stack_shape(True).shape, F32) for _ in range(2)]
    diff_stacks = [jnp.zeros(_cache_stack_shape(False).shape, F32), jnp.zeros(_cache_stack_shape(True).shape, F32)]
    new_f, new_b = [], []
    for l in range(DEPTH):
        lam_init = 0.8 - 0.6 * math.exp(-0.3 * l)
        lp = {'rwkv_mix': rwkv_mix[l], 'rwkv_w0': rwkv_w0[l], 'rwkv_w2': rwkv_w2[l], 'rwkv_a0': rwkv_a0[l],
              'rwkv_a2': rwkv_a2[l], 'rwkv_g2': rwkv_g2[l], 'rwkv_kk': rwkv_kk[l], 'rwkv_ka': rwkv_ka[l],
              'rwkv_rk': rwkv_rk[l], 'rwkv_lnx_g': rwkv_lnx_g[l], 'rwkv_lnx_b': rwkv_lnx_b[l]}
        p_att = _matmul_w32(h, w_in, l, F32, col0=0, n=ATT_WIDTH, name="in_att")
        p_rw = _matmul_w32(h, w_in, l, F32, col0=ATT_WIDTH, n=RWKV_FEAT, tn=RWKV_FEAT // 3, name="in_rwkv")
        gates = _matmul_w32(h, w_in, l, BF16, col0=c1, n=GATE_WIDTH, name="in_gate", sigmoid=True)

        oa, *na_stacks = _na_ctx(p_att, l, oa, na_stacks)
        oa = _na_lat(p_att, cache_na_k, cache_na_v, l, _na_bias_table(na_rpb[l]), oa)
        ob, *diff_stacks = _diff_attn(p_att, diff_lambda[l], diff_subln[l], lam_init, l, ob, t=SEQ, n_seq=BATCH,
                                      blk0=0, stacks=diff_stacks)
        ob, = _diff_attn(p_att, diff_lambda[l], diff_subln[l], lam_init, l, ob, t=DEC_SEQ, n_seq=DEC_BATCH,
                         blk0=N_CTX // DEC_SEQ, cache=(cache_diff_k, cache_diff_v), tables=rope_tables)
        oc, sf_c, sb_c, y_bufs = _rwkv_mixer(p_rw, state_rwkv_fwd[:, l], state_rwkv_bwd[:, l], lp, y_bufs)

        merged = _merge(oa, ob, oc, gates, w_branch16, l)
        x, h2 = _proj_res_ln(merged, w_out16, x, mods, l, 2, ln1_g[l], ln1_b[l], l, 3, tm=512, n_sub=4, name="w_out_res_ln")
        hid = _ffn_in(h2, w_ffn_in, l)
        new_f.append(sf_c)
        new_b.append(sb_c)
        ffn_out = functools.partial(_proj_res_ln, hid, w_ffn_out16, x, mods, l, 5, ln2_g[l], ln2_b[l],
                                    tm=256, n_sub=1, name="ffn_out_res_ln")
        if l + 1 < DEPTH:
            x, h = ffn_out(l + 1, 0)
        else:
            y_ctx, _ = ffn_out(None, 0, rows=(0, N_CTX))
            y_lat, _ = ffn_out(None, 0, rows=(N_CTX, N_LAT))

    y_prompt = y_ctx.reshape(BATCH, SEQ, D_MODEL)
    y_sample = y_lat.reshape(DEC_BATCH, DEC_SEQ, D_MODEL)
    new_na_k, new_na_v = na_stacks
    new_diff_k = diff_stacks[0].reshape(BATCH, DEPTH, SEQ, DIFF_HEADS, 2, DIFF_QK_DIM)
    new_diff_v = diff_stacks[1]
    return (y_prompt, y_sample, new_na_k, new_na_v, new_diff_k, new_diff_v,
            jnp.stack(new_f, axis=1), jnp.stack(new_b, axis=1))
```

```python
import functools
import math

import numpy as np
import jax
import jax.numpy as jnp
from jax import lax
from jax.experimental import pallas as pl
from jax.experimental.pallas import tpu as pltpu

F32 = jnp.float32
BF16 = jnp.bfloat16

D_MODEL = 2048
BATCH = 16
SEQ = 256
DEPTH = 4
DEC_BATCH = 4
DEC_SEQ = 1024
PAST_LEN = 256
GRID_W = 64
GRID_ROWS = DEC_SEQ // GRID_W
BRANCH_WIDTH = 512
N_BRANCH = 3
NA_HEADS = 4
NA_HEAD_DIM = 128
NA_WIN_ROWS = 8
NA_WIN_COLS = 16
NA_REL_ROWS = 2 * NA_WIN_ROWS - 1
NA_REL_COLS = 2 * NA_WIN_COLS - 1
DIFF_HEADS = 4
DIFF_QK_DIM = 64
DIFF_V_DIM = 128
RWKV_HEADS = 8
RWKV_HEAD_DIM = 64
RWKV_WIDTH = RWKV_HEADS * RWKV_HEAD_DIM
RWKV_DECAY_RANK = 64
RWKV_ICL_RANK = 64
RWKV_GATE_RANK = 128
RWKV_FEAT = 3 * RWKV_WIDTH + 2 * RWKV_DECAY_RANK + 2 * RWKV_ICL_RANK + RWKV_GATE_RANK
RWKV_GN_EPS = 64e-5
ATT_WIDTH = 6 * BRANCH_WIDTH
GATE_WIDTH = N_BRANCH * D_MODEL
FFN_HIDDEN = -(-8 * D_MODEL // (3 * 256)) * 256
ROPE_THETA = 10000.0
LN_EPS = 1e-5
ALPHA = (2.0 * DEPTH) ** 0.25

N_CTX = BATCH * SEQ
N_LAT = DEC_BATCH * DEC_SEQ
N_TOK = N_CTX + N_LAT
N_COND = 8
RWKV_CHUNK = 64
RWKV_SUPER = 256
RWKV_HEADS_PER_STEP = 4
RWKV_PASSES_PAIR = 1
RWKV_PASSES_SOLVE = 1
RWKV_PASSES_APPLY = 3
RWKV_PASSES_OUT = 1
RWKV_PASSES_STATE = 3
NEG_BIG = -1e30

NN = ((1,), (0,))
NT = ((1,), (1,))
TN = ((0,), (0,))


def _cparams(sem, vmem_mb=48):
    return pltpu.CompilerParams(dimension_semantics=sem, vmem_limit_bytes=vmem_mb * 1024 * 1024)


def _dot(a, b, dims=NN):
    return lax.dot_general(a, b, (dims, ((), ())), preferred_element_type=F32)


def _split2(x):
    hi = x.astype(BF16)
    lo = (x - hi.astype(F32)).astype(BF16)
    return hi, lo


def _dot3(a, b, dims=NN):
    ah, al = _split2(a)
    bh, bl = _split2(b)
    return _dot(ah, bh, dims) + (_dot(ah, bl, dims) + _dot(al, bh, dims))


def _dotp(a, b, dims, passes):
    if passes == 1:
        return _dot(a.astype(BF16), b.astype(BF16), dims)
    assert passes == 3
    return _dot3(a, b, dims)


def _dot_exact_lhs(a_bf16, b, dims=NN):
    b1 = b.astype(BF16)
    r1 = b - b1.astype(F32)
    b2 = r1.astype(BF16)
    b3 = (r1 - b2.astype(F32)).astype(BF16)
    return _dot(a_bf16, b1, dims) + (_dot(a_bf16, b2, dims) + _dot(a_bf16, b3, dims))


def _dot_exact_rhs(a, b_bf16, dims=NN):
    a1 = a.astype(BF16)
    r1 = a - a1.astype(F32)
    a2 = r1.astype(BF16)
    a3 = (r1 - a2.astype(F32)).astype(BF16)
    return _dot(a1, b_bf16, dims) + (_dot(a2, b_bf16, dims) + _dot(a3, b_bf16, dims))


def _normalize(x):
    mu = jnp.mean(x, axis=-1, keepdims=True)
    xc = x - mu
    var = jnp.mean(xc * xc, axis=-1, keepdims=True)
    return xc * lax.rsqrt(var + LN_EPS)


def _cond_of_row(row):
    return jnp.where(row < N_CTX, 0, 1 + (row - N_CTX) // DEC_SEQ)


def _adaln_kernel(c_ref, w_ref, b_ref, o_ref):
    c = c_ref[...]
    s = (c * jax.nn.sigmoid(c)).astype(BF16)
    o_ref[...] = _dot(s, w_ref[...].astype(BF16)) + b_ref[...]


def _adaln_all(cond, w_ada, b_ada):
    tn = 1024
    n = 6 * D_MODEL
    return pl.pallas_call(
        _adaln_kernel,
        grid=(DEPTH, n // tn),
        in_specs=[
            pl.BlockSpec((N_COND, D_MODEL), lambda l, j: (0, 0)),
            pl.BlockSpec((None, D_MODEL, tn), lambda l, j: (l, 0, j)),
            pl.BlockSpec((None, 1, tn), lambda l, j: (l, 0, j)),
        ],
        out_specs=pl.BlockSpec((None, N_COND, tn), lambda l, j: (l, 0, j)),
        out_shape=jax.ShapeDtypeStruct((DEPTH, N_COND, n), F32),
        compiler_params=_cparams(("parallel", "parallel")),
        name="adaln",
    )(cond, w_ada, b_ada.reshape(DEPTH, 1, n))


def _mod_spec(layer, which, tm, blk0=0):
    base = layer * 6 * N_COND + which * N_COND
    return pl.BlockSpec((None, 1, D_MODEL), lambda i: (base + _cond_of_row((blk0 + i) * tm), 0, 0))


def _modulate_kernel(x_ref, sh_ref, sc_ref, h_ref):
    h_ref[...] = (_normalize(x_ref[...]) * (1.0 + sc_ref[...]) + sh_ref[...]).astype(h_ref.dtype)


def _modulate(x, mods, layer):
    tm = 256
    return pl.pallas_call(
        _modulate_kernel,
        grid=(N_TOK // tm,),
        in_specs=[
            pl.BlockSpec((tm, D_MODEL), lambda i: (i, 0)),
            _mod_spec(layer, 0, tm),
            _mod_spec(layer, 1, tm),
        ],
        out_specs=pl.BlockSpec((tm, D_MODEL), lambda i: (i, 0)),
        out_shape=jax.ShapeDtypeStruct((N_TOK, D_MODEL), BF16),
        compiler_params=_cparams(("parallel",)),
        name="modulate",
    )(x, mods, mods)


def _proj_res_ln_kernel(m_ref, w_ref, x_ref, gate_ref, g_ref, b_ref, *rest, with_mod, n_sub):
    if with_mod:
        sh_ref, sc_ref, xo_ref, h_ref = rest
    else:
        (xo_ref,) = rest
    sub = m_ref.shape[0] // n_sub

    def norm_rows(s, z):
        rows = slice(s * sub, (s + 1) * sub)
        y = ALPHA * x_ref[rows, :] + gate_ref[...] * z
        xn = _normalize(y) * g_ref[...] + b_ref[...]
        xo_ref[rows, :] = xn
        if with_mod:
            h_ref[rows, :] = (_normalize(xn) * (1.0 + sc_ref[...]) + sh_ref[...]).astype(h_ref.dtype)

    z_prev = _dot(m_ref[0:sub, :], w_ref[...])
    for s in range(1, n_sub):
        z_next = _dot(m_ref[s * sub:(s + 1) * sub, :], w_ref[...])
        norm_rows(s - 1, z_prev)
        z_prev = z_next
    norm_rows(n_sub - 1, z_prev)


def _proj_res_ln(m, w16, x, mods, layer, gate_idx, ln_g, ln_b, mod_layer, mod_idx, *, tm, n_sub, name, rows=(0, N_TOK)):
    k = m.shape[1]
    with_mod = mod_layer is not None
    blk0, n_rows = rows[0] // tm, rows[1]
    vec = pl.BlockSpec((1, D_MODEL), lambda i: (0, 0))
    row = pl.BlockSpec((tm, D_MODEL), lambda i: (i, 0))
    in_specs = [
        pl.BlockSpec((tm, k), lambda i: (blk0 + i, 0)),
        pl.BlockSpec((None, k, D_MODEL), lambda i: (layer, 0, 0), pipeline_mode=pl.Buffered(1)),
        pl.BlockSpec((tm, D_MODEL), lambda i: (blk0 + i, 0)),
        _mod_spec(layer, gate_idx, tm, blk0), vec, vec,
    ]
    args = [m, w16, x, mods, ln_g.reshape(1, D_MODEL), ln_b.reshape(1, D_MODEL)]
    out_specs = [row]
    out_shape = [jax.ShapeDtypeStruct((n_rows, D_MODEL), F32)]
    if with_mod:
        in_specs += [_mod_spec(mod_layer, mod_idx, tm, blk0), _mod_spec(mod_layer, mod_idx + 1, tm, blk0)]
        args += [mods, mods]
        out_specs.append(row)
        out_shape.append(jax.ShapeDtypeStruct((n_rows, D_MODEL), BF16))
    out = pl.pallas_call(
        functools.partial(_proj_res_ln_kernel, with_mod=with_mod, n_sub=n_sub),
        grid=(n_rows // tm,),
        in_specs=in_specs,
        out_specs=out_specs,
        out_shape=out_shape,
        compiler_params=_cparams(("parallel",), 60),
        name=name,
    )(*args)
    return (out[0], out[1]) if with_mod else (out[0], None)


def _mm_w32_kernel(a_ref, w_ref, o_ref, w16_ref, *, sigmoid):
    @pl.when(pl.program_id(1) == 0)
    def _():
        w16_ref[...] = w_ref[0].astype(BF16)

    acc = _dot(a_ref[...], w16_ref[...])
    if sigmoid:
        acc = jax.nn.sigmoid(acc)
    o_ref[...] = acc.astype(o_ref.dtype)


def _matmul_w32(a, w, layer, out_dtype, col0=0, n=None, tm=1024, tn=1024, name="matmul", sigmoid=False):
    m, k = a.shape
    n = w.shape[2] if n is None else n
    assert m % tm == 0 and n % tn == 0
    return pl.pallas_call(
        functools.partial(_mm_w32_kernel, sigmoid=sigmoid),
        grid=(n // tn, m // tm),
        in_specs=[
            pl.BlockSpec((tm, k), lambda j, i: (i, 0)),
            pl.BlockSpec((pl.Element(1), pl.Element(k), pl.Element(tn)),
                         lambda j, i: (layer, 0, pl.multiple_of(col0 + j * tn, 128))),
        ],
        out_specs=pl.BlockSpec((tm, tn), lambda j, i: (i, j)),
        out_shape=jax.ShapeDtypeStruct((m, n), out_dtype),
        scratch_shapes=[pltpu.VMEM((k, tn), BF16)],
        compiler_params=_cparams(("parallel", "arbitrary")),
        name=name,
    )(a, w)


def _swiglu_kernel(a_ref, wg_ref, wu_ref, o_ref, wg16_ref, wu16_ref):
    @pl.when(pl.program_id(1) == 0)
    def _():
        wg16_ref[...] = wg_ref[...].astype(BF16)
        wu16_ref[...] = wu_ref[...].astype(BF16)

    a = a_ref[...]
    g = _dot(a, wg16_ref[...])
    u = _dot(a, wu16_ref[...])
    o_ref[...] = (g * jax.nn.sigmoid(g) * u).astype(o_ref.dtype)


def _ffn_in(h, w_ffn_in, layer):
    tm, tn = 1024, 512
    nb = FFN_HIDDEN // tn
    return pl.pallas_call(
        _swiglu_kernel,
        grid=(nb, N_TOK // tm),
        in_specs=[
            pl.BlockSpec((tm, D_MODEL), lambda j, i: (i, 0)),
            pl.BlockSpec((None, D_MODEL, tn), lambda j, i: (layer, 0, j)),
            pl.BlockSpec((None, D_MODEL, tn), lambda j, i: (layer, 0, j + nb)),
        ],
        out_specs=pl.BlockSpec((tm, tn), lambda j, i: (i, j)),
        out_shape=jax.ShapeDtypeStruct((N_TOK, FFN_HIDDEN), BF16),
        scratch_shapes=[pltpu.VMEM((D_MODEL, tn), BF16), pltpu.VMEM((D_MODEL, tn), BF16)],
        compiler_params=_cparams(("parallel", "arbitrary")),
        name="ffn_in",
    )(h, w_ffn_in, w_ffn_in)


def _merge_kernel(oa_ref, ob_ref, oc_ref, ga_ref, gb_ref, gc_ref, wb_ref, o_ref):
    acc = ga_ref[...].astype(F32) * _dot(oa_ref[...], wb_ref[0])
    acc += gb_ref[...].astype(F32) * _dot(ob_ref[...], wb_ref[1])
    acc += gc_ref[...].astype(F32) * _dot(oc_ref[...], wb_ref[2])
    o_ref[...] = acc.astype(o_ref.dtype)


def _merge(oa, ob, oc, gate_pre, w_branch, layer):
    tm, tn = 512, 1024
    nb = D_MODEL // tn
    o_spec = pl.BlockSpec((tm, BRANCH_WIDTH), lambda i, j: (i, 0))
    return pl.pallas_call(
        _merge_kernel,
        grid=(N_TOK // tm, nb),
        in_specs=[
            o_spec, o_spec, o_spec,
            pl.BlockSpec((tm, tn), lambda i, j: (i, j)),
            pl.BlockSpec((tm, tn), lambda i, j: (i, j + nb)),
            pl.BlockSpec((tm, tn), lambda i, j: (i, j + 2 * nb)),
            pl.BlockSpec((None, N_BRANCH, BRANCH_WIDTH, tn), lambda i, j: (layer, 0, 0, j)),
        ],
        out_specs=pl.BlockSpec((tm, tn), lambda i, j: (i, j)),
        out_shape=jax.ShapeDtypeStruct((N_TOK, D_MODEL), BF16),
        compiler_params=_cparams(("parallel", "parallel")),
        name="merge",
    )(oa, ob, oc, gate_pre, gate_pre, gate_pre, w_branch)


def _softmax_rows(s):
    m = jnp.max(s, axis=-1, keepdims=True)
    e = jnp.exp(s - m)
    return e * (1.0 / jnp.sum(e, axis=-1, keepdims=True))


def _na_ctx_kernel(q_ref, k_ref, v_ref, *rest):
    o_ref, new_k_ref, new_v_ref = rest[-3:]
    scale = NA_HEAD_DIM ** -0.5
    _store_heads(new_k_ref, k_ref)
    _store_heads(new_v_ref, v_ref)
    heads = range(NA_HEADS)
    sls = [slice(h * NA_HEAD_DIM, (h + 1) * NA_HEAD_DIM) for h in heads]
    s = [_dot(q_ref[:, sl].astype(BF16), k_ref[:, sl].astype(BF16), NT) * scale for sl in sls]
    p = [_softmax_rows(s[h]).astype(BF16) for h in heads]
    o = [_dot(p[h], v_ref[:, sls[h]].astype(BF16)) for h in heads]
    for h in heads:
        o_ref[:, sls[h]] = o[h].astype(o_ref.dtype)


_ANY_SPEC = pl.BlockSpec(memory_space=pl.ANY)


def _cache_stack_shape(split_heads):
    tail = (NA_HEADS, NA_HEAD_DIM) if split_heads else (BRANCH_WIDTH,)
    return jax.ShapeDtypeStruct((BATCH, DEPTH, SEQ) + tail, F32)


def _cache_stack_spec(split_heads, index_map):
    tail = (NA_HEADS, NA_HEAD_DIM) if split_heads else (BRANCH_WIDTH,)
    return pl.BlockSpec((None, None, SEQ) + tail, index_map)


def _store_heads(dst_ref, src_ref):
    for h in range(NA_HEADS):
        dst_ref[:, h, :] = src_ref[:, h * NA_HEAD_DIM:(h + 1) * NA_HEAD_DIM]


def _na_ctx(p_att, layer, oa, stacks):
    w = BRANCH_WIDTH
    stack_spec = _cache_stack_spec(True, lambda n: (n, layer, 0, 0, 0))
    return pl.pallas_call(
        _na_ctx_kernel,
        grid=(BATCH,),
        in_specs=[
            pl.BlockSpec((SEQ, w), lambda n: (n, 0)),
            pl.BlockSpec((SEQ, w), lambda n: (n, 1)),
            pl.BlockSpec((SEQ, w), lambda n: (n, 2)),
            _ANY_SPEC, _ANY_SPEC, _ANY_SPEC,
        ],
        out_specs=[pl.BlockSpec((SEQ, w), lambda n: (n, 0)), stack_spec, stack_spec],
        out_shape=[jax.ShapeDtypeStruct((N_TOK, w), BF16), _cache_stack_shape(True), _cache_stack_shape(True)],
        input_output_aliases={3: 0, 4: 1, 5: 2},
        compiler_params=_cparams(("parallel",)),
        name="na_ctx",
    )(p_att, p_att, p_att, oa, *stacks)


def _na_row_start(r):
    return jnp.clip(r - NA_WIN_ROWS // 2, 0, GRID_ROWS - NA_WIN_ROWS)


def _na_lat_kernel(q_ref, k_ref, v_ref, kc_ref, vc_ref, bias_ref, _, o_ref):
    scale = NA_HEAD_DIM ** -0.5
    n_loc = NA_WIN_ROWS * GRID_W
    start = pl.multiple_of(_na_row_start(pl.program_id(1)) * GRID_W, GRID_W)
    k_loc = k_ref[pl.ds(start, n_loc), :].astype(BF16)
    v_loc = v_ref[pl.ds(start, n_loc), :].astype(BF16)
    heads = range(NA_HEADS)
    sls = [slice(h * NA_HEAD_DIM, (h + 1) * NA_HEAD_DIM) for h in heads]
    q = [q_ref[:, sl].astype(BF16) for sl in sls]
    s_loc = [_dot(q[h], k_loc[:, sls[h]], NT) * scale + bias_ref[h] for h in heads]
    s_ctx = [_dot(q[h], kc_ref[:, sls[h]].astype(BF16), NT) * scale for h in heads]
    m = [jnp.maximum(jnp.max(s_loc[h], axis=-1, keepdims=True), jnp.max(s_ctx[h], axis=-1, keepdims=True))
         for h in heads]
    e_loc = [jnp.exp(s_loc[h] - m[h]) for h in heads]
    e_ctx = [jnp.exp(s_ctx[h] - m[h]) for h in heads]
    inv = [1.0 / (jnp.sum(e_loc[h], axis=-1, keepdims=True) + jnp.sum(e_ctx[h], axis=-1, keepdims=True))
           for h in heads]
    o_loc = [_dot((e_loc[h] * inv[h]).astype(BF16), v_loc[:, sls[h]]) for h in heads]
    o_ctx = [_dot((e_ctx[h] * inv[h]).astype(BF16), vc_ref[:, sls[h]].astype(BF16)) for h in heads]
    for h in heads:
        o_ref[:, sls[h]] = (o_loc[h] + o_ctx[h]).astype(o_ref.dtype)


def _na_bias_kernel(rows_ref, onehot_ref, mask_ref, o_ref):
    o_ref[...] = _dot_exact_rhs(rows_ref[...], onehot_ref[...]) + mask_ref[...]


def _na_bias_table(rpb):
    nr, nd = NA_WIN_ROWS, NA_REL_COLS + 1
    rows = jnp.stack([rpb[:, nr - 1 - p:2 * nr - 1 - p, :] for p in range(nr)], axis=1)
    rows = jnp.pad(rows.astype(F32), ((0, 0), (0, 0), (0, 0), (0, nd - NA_REL_COLS)))
    qc = np.arange(GRID_W)[:, None]
    kc = np.arange(GRID_W)[None, :]
    rel_c = np.clip(kc - qc, -(NA_WIN_COLS - 1), NA_WIN_COLS - 1) + NA_WIN_COLS - 1
    onehot = (rel_c[None] == np.arange(nd)[:, None, None]).reshape(nd, GRID_W * GRID_W)
    win_c0 = np.clip(qc - NA_WIN_COLS // 2, 0, GRID_W - NA_WIN_COLS)
    valid = ((kc >= win_c0) & (kc < win_c0 + NA_WIN_COLS)).reshape(1, GRID_W * GRID_W)
    n_rows = NA_HEADS * nr * nr
    full = lambda shape: pl.BlockSpec(shape, lambda: (0,) * len(shape))
    bias = pl.pallas_call(
        _na_bias_kernel,
        in_specs=[full((n_rows, nd)), full((nd, GRID_W * GRID_W)), full((1, GRID_W * GRID_W))],
        out_specs=full((n_rows, GRID_W * GRID_W)),
        out_shape=jax.ShapeDtypeStruct((n_rows, GRID_W * GRID_W), F32),
        name="na_bias",
    )(rows.reshape(n_rows, nd), jnp.asarray(onehot, BF16), jnp.asarray(np.where(valid, 0.0, NEG_BIG), F32))
    bias = bias.reshape(NA_HEADS, nr, nr, GRID_W, GRID_W).transpose(0, 1, 3, 2, 4)
    return bias.reshape(NA_HEADS, nr, GRID_W, nr * GRID_W)


def _na_lat(p_att, cache_k, cache_v, layer, bias, oa):
    w = BRANCH_WIDTH
    blk0 = N_CTX // DEC_SEQ
    q0 = N_CTX // GRID_W
    rows_per = DEC_SEQ // GRID_W
    kv_cache = pl.BlockSpec((None, None, PAST_LEN, w), lambda n, r: (n, layer, 0, 0))
    return pl.pallas_call(
        _na_lat_kernel,
        grid=(DEC_BATCH, GRID_ROWS),
        in_specs=[
            pl.BlockSpec((GRID_W, w), lambda n, r: (q0 + n * rows_per + r, 0)),
            pl.BlockSpec((DEC_SEQ, w), lambda n, r: (blk0 + n, 1)),
            pl.BlockSpec((DEC_SEQ, w), lambda n, r: (blk0 + n, 2)),
            kv_cache, kv_cache,
            pl.BlockSpec((NA_HEADS, None, GRID_W, NA_WIN_ROWS * GRID_W),
                         lambda n, r: (0, r - _na_row_start(r), 0, 0)),
            _ANY_SPEC,
        ],
        out_specs=pl.BlockSpec((GRID_W, w), lambda n, r: (q0 + n * rows_per + r, 0)),
        out_shape=jax.ShapeDtypeStruct((N_TOK, w), BF16),
        input_output_aliases={6: 0},
        compiler_params=_cparams(("parallel", "arbitrary")),
        name="na_lat",
    )(p_att, p_att, p_att, cache_k, cache_v, bias, oa)


def _rope(x, cos, sin):
    lane = lax.broadcasted_iota(jnp.int32, x.shape, 1)
    first = (lane % 32) < 16
    rot = jnp.where(first, -pltpu.roll(x, 128 - 16, 1), pltpu.roll(x, 16, 1))
    return x * cos + rot * sin


def _diff_kernel(*refs, rope, cache, lam_init, tq):
    if cache:
        q_ref, k_ref, v_ref, kc_ref, vc_ref, cos_ref, sin_ref, lam_ref, g_ref, _, o_ref = refs
    else:
        q_ref, k_ref, v_ref, lam_ref, g_ref = refs[:5]
        o_ref, new_k_ref, new_v_ref = refs[-3:]
        new_k_ref[...] = k_ref[...]
        _store_heads(new_v_ref, v_ref)
    scale = DIFF_QK_DIM ** -0.5
    lp = lam_ref[...]
    lam = (jnp.exp(jnp.sum(lp[0:1] * lp[1:2], axis=-1, keepdims=True))
           - jnp.exp(jnp.sum(lp[2:3] * lp[3:4], axis=-1, keepdims=True)) + lam_init)
    hd = DIFF_V_DIM
    t = q_ref.shape[0]
    n_heads = q_ref.shape[1] // hd
    ks, vs = [], []
    for hh in range(n_heads):
        sl = slice(hh * hd, (hh + 1) * hd)
        k = k_ref[:, sl]
        v = v_ref[:, sl]
        if rope:
            k = _rope(k, cos_ref[...], sin_ref[...])
        if cache:
            k = jnp.concatenate([k, kc_ref[:, sl]], axis=0)
            v = jnp.concatenate([v, vc_ref[:, sl]], axis=0)
        ks.append(k.astype(BF16))
        vs.append(v.astype(BF16))
    units = [(hh, i) for hh in range(n_heads) for i in range(t // tq)]
    rows = {u: slice(u[1] * tq, (u[1] + 1) * tq) for u in units}
    cols = {u: slice(u[0] * hd, (u[0] + 1) * hd) for u in units}
    q = {u: q_ref[rows[u], cols[u]] for u in units}
    if rope:
        q = {u: _rope(q[u], cos_ref[rows[u], :], sin_ref[rows[u], :]) for u in units}
    lane = lax.broadcasted_iota(jnp.int32, (tq, hd), 1)
    assert math.frexp(scale)[0] == 0.5
    q = {u: q[u] * scale for u in units}
    s1 = {u: _dot(jnp.where(lane < DIFF_QK_DIM, q[u], 0.0).astype(BF16), ks[u[0]], NT) for u in units}
    s2 = {u: _dot(jnp.where(lane >= DIFF_QK_DIM, q[u], 0.0).astype(BF16), ks[u[0]], NT) for u in units}
    w = {u: (_softmax_rows(s1[u]) - lam * _softmax_rows(s2[u])).astype(BF16) for u in units}
    o = {u: _dot(w[u], vs[u[0]]) for u in units}
    for u in units:
        on = o[u] * lax.rsqrt(jnp.mean(o[u] * o[u], axis=-1, keepdims=True) + LN_EPS)
        o_ref[rows[u], cols[u]] = (on * g_ref[:, cols[u]] * (1.0 - lam_init)).astype(o_ref.dtype)


def _diff_attn(p_att, lam_p, subln, lam_init, layer, ob, *, t, n_seq, blk0, cache=None, tables=None, stacks=None):
    hd = DIFF_V_DIM * (1 if cache is not None else DIFF_HEADS)
    q_col, k_col, v_col = (z * BRANCH_WIDTH // hd for z in (3, 4, 5))
    in_specs = [
        pl.BlockSpec((t, hd), lambda n, h: (blk0 + n, q_col + h)),
        pl.BlockSpec((t, hd), lambda n, h: (blk0 + n, k_col + h)),
        pl.BlockSpec((t, hd), lambda n, h: (blk0 + n, v_col + h)),
    ]
    args = [p_att, p_att, p_att]
    if cache is not None:
        cache_k, cache_v = cache
        tab = pl.BlockSpec((t, hd), lambda n, h: (0, 0))
        cache_spec = pl.BlockSpec((None, None, PAST_LEN, hd), lambda n, h: (n, layer, 0, h))
        in_specs += [cache_spec, cache_spec, tab, tab]
        args += [cache_k, cache_v, tables[0], tables[1]]
    in_specs += [
        pl.BlockSpec((4, DIFF_QK_DIM), lambda n, h: (0, 0)),
        pl.BlockSpec((1, hd), lambda n, h: (0, h)),
    ]
    args += [lam_p, subln.reshape(1, BRANCH_WIDTH)]
    out_specs = [pl.BlockSpec((t, hd), lambda n, h: (blk0 + n, h))]
    out_shape = [jax.ShapeDtypeStruct((N_TOK, BRANCH_WIDTH), BF16)]
    aliases = {len(args): 0}
    in_specs.append(_ANY_SPEC)
    args.append(ob)
    if cache is None:
        out_specs += [_cache_stack_spec(False, lambda n, h: (n, layer, 0, 0)),
                      _cache_stack_spec(True, lambda n, h: (n, layer, 0, 0, 0))]
        out_shape += [_cache_stack_shape(False), _cache_stack_shape(True)]
        aliases.update({len(args): 1, len(args) + 1: 2})
        in_specs += [_ANY_SPEC, _ANY_SPEC]
        args += list(stacks)
    return pl.pallas_call(
        functools.partial(_diff_kernel, rope=cache is not None, cache=cache is not None,
                          lam_init=lam_init, tq=256),
        grid=(n_seq, BRANCH_WIDTH // hd),
        in_specs=in_specs,
        out_specs=out_specs,
        out_shape=out_shape,
        input_output_aliases=aliases,
        compiler_params=_cparams(("parallel", "parallel")),
        name="diff_lat" if cache is not None else "diff_ctx",
    )(*args)


def _rope_tables():
    t = np.arange(DEC_SEQ)
    rows = (t // GRID_W).astype(np.float32)
    cols = (t % GRID_W).astype(np.float32)
    half = DIFF_QK_DIM // 2
    inv = jnp.asarray(ROPE_THETA, F32) ** (-jnp.arange(0, half, 2, dtype=F32) / half)
    ang_r = jnp.asarray(rows)[:, None] * inv
    ang_c = jnp.asarray(cols)[:, None] * inv
    ang = jnp.concatenate([ang_r, ang_r, ang_c, ang_c] * 2, axis=-1)
    return jnp.cos(ang), jnp.sin(ang)


def _head_sum(x, ones_bd):
    return _dot_exact_rhs(x, ones_bd)


def _rwkv_prep_kernel(cur_ref, prev_ref, next_ref, mix_ref, w0_ref, w2_ref, a0_ref, a2_ref, g2_ref,
                      kk_ref, ka_ref, rk_ref, ones_ref,
                      r_ref, v_ref, kkn_ref, lwf_ref, lwb_ref, kf_ref, kb_ref, bf_ref, bb_ref, bonus_ref, g_ref,
                      *, tm):
    i = pl.program_id(0)
    blocks_per_seq = DEC_SEQ // tm
    j = i - N_CTX // tm
    is_ctx = i < N_CTX // tm
    at_start = jnp.logical_or(is_ctx, j % blocks_per_seq == 0)
    at_end = jnp.logical_or(is_ctx, j % blocks_per_seq == blocks_per_seq - 1)
    cur = cur_ref[...]
    row = lax.broadcasted_iota(jnp.int32, cur.shape, 0)
    prev_row = jnp.where(at_start, 0.0, prev_ref[7:8, :])
    next_row = jnp.where(at_end, 0.0, next_ref[0:1, :])
    prev = jnp.where(row == 0, prev_row, pltpu.roll(cur, 1, 0))
    nxt = jnp.where(row == tm - 1, next_row, pltpu.roll(cur, tm - 1, 0))
    f = cur + mix_ref[0:1, :] * (prev - cur) + mix_ref[1:2, :] * (nxt - cur)

    w = RWKV_WIDTH
    r = f[:, 0:w]
    k = f[:, w:2 * w]
    v = f[:, 2 * w:3 * w]
    c0 = 3 * w
    wd = f[:, c0:c0 + 2 * RWKV_DECAY_RANK]
    ad = f[:, c0 + 2 * RWKV_DECAY_RANK:c0 + 2 * RWKV_DECAY_RANK + 2 * RWKV_ICL_RANK]
    gd = f[:, c0 + 2 * RWKV_DECAY_RANK + 2 * RWKV_ICL_RANK:]
    ones_bd = ones_ref[...]

    kkv = k * kk_ref[...]
    norm = jnp.sqrt(_head_sum(kkv * kkv, ones_bd))
    kkn = kkv / jnp.maximum(norm, 1e-12)
    wlin = _dot(jnp.tanh(wd).astype(BF16), w2_ref[...]) + w0_ref[...]
    alin = _dot(ad.astype(BF16), a2_ref[...]) + a0_ref[...]
    g = _dot(jax.nn.sigmoid(gd).astype(BF16), g2_ref[...])

    r_ref[...] = r
    v_ref[...] = v
    kkn_ref[...] = kkn
    g_ref[...] = g.astype(g_ref.dtype)
    bonus = jnp.zeros_like(v)
    for d, (lw_ref, kd_ref, bd_ref) in enumerate(((lwf_ref, kf_ref, bf_ref), (lwb_ref, kb_ref, bb_ref))):
        z = -wlin[:, d * w:(d + 1) * w]
        softplus = jnp.maximum(z, 0.0) + jnp.log(1.0 + jnp.exp(-jnp.abs(z)))
        lw_ref[...] = -jnp.exp(-softplus - 0.5)
        a = jax.nn.sigmoid(alin[:, d * w:(d + 1) * w])
        k_d = k * (1.0 + (a - 1.0) * ka_ref[...])
        kd_ref[...] = k_d
        bd_ref[...] = kkn * a
        bonus += _head_sum(r * k_d * rk_ref[...], ones_bd) * v
    bonus_ref[...] = bonus.astype(bonus_ref.dtype)


def _block_diag2(a, b):
    za = jnp.zeros((a.shape[0], b.shape[1]), a.dtype)
    zb = jnp.zeros((b.shape[0], a.shape[1]), b.dtype)
    return jnp.concatenate([jnp.concatenate([a, za], axis=1), jnp.concatenate([zb, b], axis=1)], axis=0)


def _rwkv_prep(p_rw, lp):
    tm = 256
    w = RWKV_WIDTH
    sub = tm // 8
    full = lambda shape: pl.BlockSpec(shape, lambda i: (0,) * len(shape))
    out_spec = pl.BlockSpec((tm, w), lambda i: (i, 0))
    ones_bd = jnp.asarray(np.kron(np.eye(RWKV_HEADS), np.ones((RWKV_HEAD_DIM, RWKV_HEAD_DIM))), BF16)
    w2_bd = _block_diag2(lp['rwkv_w2'][0], lp['rwkv_w2'][1]).astype(BF16)
    a2_bd = _block_diag2(lp['rwkv_a2'][0], lp['rwkv_a2'][1]).astype(BF16)
    n_out = 11
    return pl.pallas_call(
        functools.partial(_rwkv_prep_kernel, tm=tm),
        grid=(N_TOK // tm,),
        in_specs=[
            pl.BlockSpec((tm, RWKV_FEAT), lambda i: (i, 0)),
            pl.BlockSpec((8, RWKV_FEAT), lambda i: (jnp.maximum(i * sub - 1, 0), 0)),
            pl.BlockSpec((8, RWKV_FEAT), lambda i: (jnp.minimum((i + 1) * sub, N_TOK // 8 - 1), 0)),
            full((2, RWKV_FEAT)),
            full((1, 2 * w)), full((2 * RWKV_DECAY_RANK, 2 * w)),
            full((1, 2 * w)), full((2 * RWKV_ICL_RANK, 2 * w)),
            full((RWKV_GATE_RANK, w)),
            full((1, w)), full((1, w)), full((1, w)),
            full((w, w)),
        ],
        out_specs=[out_spec] * n_out,
        out_shape=[jax.ShapeDtypeStruct((N_TOK, w), F32)] * (n_out - 2) + [jax.ShapeDtypeStruct((N_TOK, w), BF16)] * 2,
        compiler_params=_cparams(("parallel",)),
        name="rwkv_prep",
    )(p_rw, p_rw, p_rw, lp['rwkv_mix'],
      lp['rwkv_w0'].reshape(1, 2 * w), w2_bd, lp['rwkv_a0'].reshape(1, 2 * w), a2_bd,
      lp['rwkv_g2'].astype(BF16),
      lp['rwkv_kk'].reshape(1, w), lp['rwkv_ka'].reshape(1, w), lp['rwkv_rk'].reshape(1, w),
      ones_bd)


def _rwkv_chunk_maps(chains):
    n = len(chains)
    rs, ks, vs, kkns, bs, lws, cum, fwds = (list(z) for z in zip(*chains))
    c, nk = rs[0].shape
    row = lax.broadcasted_iota(jnp.int32, (c, c), 0)
    col = lax.broadcasted_iota(jnp.int32, (c, c), 1)
    eye = row == col
    incl_d = {True: row >= col, False: row <= col}
    strict_d = {True: row > col, False: row < col}
    idx = range(n)
    tot =[cum[i][c - 1:c, :] if fwds[i] else cum[i][0:1, :] for i in idx]
    e_neg = [jnp.exp(-cum[i]) for i in idx]
    a_t = [-kkns[i] * jnp.exp(cum[i] - lws[i]) for i in idx]
    r_t = [rs[i] * jnp.exp(cum[i]) for i in idx]
    ar = [jnp.concatenate([a_t[i], r_t[i]], axis=0) for i in idx]
    with_b = [_dotp(ar[i], bs[i] * e_neg[i], NT, RWKV_PASSES_PAIR) for i in idx]
    with_k = [_dotp(ar[i], ks[i] * e_neg[i], NT, RWKV_PASSES_PAIR) for i in idx]
    l_ab = [jnp.where(strict_d[fwds[i]], with_b[i][:c], 0.0) for i in idx]
    l_ak = [jnp.where(strict_d[fwds[i]], with_k[i][:c], 0.0) for i in idx]
    m_rb = [jnp.where(incl_d[fwds[i]], with_b[i][c:], 0.0) for i in idx]
    m_rk = [jnp.where(incl_d[fwds[i]], with_k[i][c:], 0.0) for i in idx]
    lakv = [_dotp(l_ak[i], vs[i], NN, RWKV_PASSES_APPLY) for i in idx]
    mrkv = [_dotp(m_rk[i], vs[i], NN, RWKV_PASSES_OUT) for i in idx]
    e_tail = [jnp.exp(tot[i] - cum[i]) for i in idx]
    kwv = [_dotp(ks[i] * e_tail[i], vs[i], TN, RWKV_PASSES_APPLY) for i in idx]
    same = lambda s: (row // s) == (col // s)
    inv = [jnp.where(eye, 1.0, 0.0) + jnp.where(same(2), l_ab[i], 0.0) for i in idx]
    size = 4
    while size <= c:
        part = jnp.logical_and(same(size), jnp.logical_not(same(size // 2)))
        half = [_dotp(inv[i], jnp.where(part, l_ab[i], 0.0), NN, RWKV_PASSES_SOLVE) for i in idx]
        inv = [inv[i] + _dotp(half[i], inv[i], NN, RWKV_PASSES_SOLVE) for i in idx]
        size *= 2
    pq1 = [_dotp(inv[i], jnp.concatenate([a_t[i], lakv[i]], axis=1), NN, RWKV_PASSES_APPLY) for i in idx]
    rb = [_dotp(m_rb[i], pq1[i], NN, RWKV_PASSES_OUT) for i in idx]
    sb = [_dotp(bs[i] * e_tail[i], pq1[i], TN, RWKV_PASSES_APPLY) for i in idx]
    out = []
    for i in idx:
        p2 = r_t[i] + rb[i][:, :nk]
        q2 = rb[i][:, nk:] + mrkv[i]
        p3 = jnp.where(eye, jnp.exp(tot[i]), 0.0) + sb[i][:, :nk]
        q3 = sb[i][:, nk:] + kwv[i]
        out.append((p2, p3, q2, q3))
    return out


def _rwkv_scan_kernel(rf, vf, af, wf, kf, bf, rb, vb, ab, wb, kb, bb, s0f, s0b, *rest, n_super, n_chunk):
    yf, yb, sff, sfb, st_ref = rest[-5:]
    sup = pl.program_id(2)
    hd = RWKV_HEAD_DIM
    cs = RWKV_CHUNK
    hp = RWKV_HEADS_PER_STEP

    @pl.when(sup == 0)
    def _():
        st_ref[0:hp] = s0f[...]
        st_ref[hp:2 * hp] = s0b[...]

    dirs = ((rf, vf, af, wf, kf, bf, yf, True), (rb, vb, ab, wb, kb, bb, yb, False))
    row = lax.broadcasted_iota(jnp.int32, (cs, cs), 0)
    col = lax.broadcasted_iota(jnp.int32, (cs, cs), 1)
    tri = {True: jnp.where(row >= col, 1.0, 0.0).astype(BF16), False: jnp.where(row <= col, 1.0, 0.0).astype(BF16)}
    cums = {(d, ci): _dot_exact_lhs(tri[dirs[d][-1]], dirs[d][3][ci * cs:(ci + 1) * cs, :])
            for d in range(2) for ci in range(n_chunk)}
    keys, chains = [], []
    for d, (r_ref, v_ref, a_ref, w_ref, k_ref, b_ref, _, forward) in enumerate(dirs):
        for ci in range(n_chunk):
            rows = slice(ci * cs, (ci + 1) * cs)
            for hh in range(hp):
                sl = slice(hh * hd, (hh + 1) * hd)
                keys.append((d, ci, hh))
                chains.append((r_ref[rows, sl], k_ref[rows, sl], v_ref[rows, sl], a_ref[rows, sl],
                               b_ref[rows, sl], w_ref[rows, sl], cums[d, ci][:, sl], forward))
    maps = dict(zip(keys, _rwkv_chunk_maps(chains)))

    seqs = [(d, hh) for d in range(2) for hh in range(hp)]
    st = {s: st_ref[hp * s[0] + s[1]] for s in seqs}
    ys = {}
    for step in range(n_chunk):
        for d, hh in seqs:
            ci = step if dirs[d][-1] else n_chunk - 1 - step
            p2, p3, q2, q3 = maps[d, ci, hh]
            ys[d, ci, hh] = _dotp(p2, st[d, hh], NN, RWKV_PASSES_OUT) + q2
            st[d, hh] = _dotp(p3, st[d, hh], NN, RWKV_PASSES_STATE) + q3
    for d, hh in seqs:
        st_ref[hp * d + hh] = st[d, hh]
    for d in range(2):
        y_ref = dirs[d][6]
        for ci in range(n_chunk):
            y_ref[ci * cs:(ci + 1) * cs, :] = jnp.concatenate([ys[d, ci, hh] for hh in range(hp)], axis=1)

    @pl.when(sup == n_super - 1)
    def _():
        sff[...] = st_ref[0:hp]
        sfb[...] = st_ref[hp:2 * hp]


def _rwkv_scan(feats, s0f_t, s0b_t, y_prev, *, t, n_seq, row0):
    r, v, kkn, lwf, lwb, kf, kb, bf, bb = feats
    rows = RWKV_SUPER
    n_super = t // rows
    blk0 = row0 // rows
    hp = RWKV_HEADS_PER_STEP
    width = hp * RWKV_HEAD_DIM
    fwd = pl.BlockSpec((rows, width), lambda n, p, s: (blk0 + n * n_super + s, p))
    bwd = pl.BlockSpec((rows, width), lambda n, p, s: (blk0 + n * n_super + n_super - 1 - s, p))
    st_spec = pl.BlockSpec((None, hp, RWKV_HEAD_DIM, RWKV_HEAD_DIM), lambda n, p, s: (n, p, 0, 0))
    st_shape = jax.ShapeDtypeStruct((n_seq, RWKV_HEADS, RWKV_HEAD_DIM, RWKV_HEAD_DIM), F32)
    y_shape = jax.ShapeDtypeStruct((N_TOK, RWKV_WIDTH), F32)
    extra = list(y_prev)
    n_in = 14
    return pl.pallas_call(
        functools.partial(_rwkv_scan_kernel, n_super=n_super, n_chunk=rows // RWKV_CHUNK),
        grid=(n_seq, RWKV_HEADS // hp, n_super),
        in_specs=[fwd] * 6 + [bwd] * 6 + [st_spec, st_spec] + [_ANY_SPEC] * len(extra),
        out_specs=[fwd, bwd, st_spec, st_spec],
        out_shape=[y_shape, y_shape, st_shape, st_shape],
        input_output_aliases={n_in: 0, n_in + 1: 1},
        scratch_shapes=[pltpu.VMEM((2 * hp, RWKV_HEAD_DIM, RWKV_HEAD_DIM), F32)],
        compiler_params=_cparams(("parallel", "parallel", "arbitrary")),
        name="rwkv_scan",
    )(r, v, kkn, lwf, kf, bf, r, v, kkn, lwb, kb, bb, s0f_t, s0b_t, *extra)


def _rwkv_mixer(p_rw, s_lat_f, s_lat_b, lp, y_bufs):
    r, v, kkn, lwf, lwb, kf, kb, bf, bb, bonus, g = _rwkv_prep(p_rw, lp)
    feats = (r, v, kkn, lwf, lwb, kf, kb, bf, bb)
    s_zero = jnp.zeros((BATCH, RWKV_HEADS, RWKV_HEAD_DIM, RWKV_HEAD_DIM), F32)
    y_f, y_b, sf_c, sb_c = _rwkv_scan(feats, s_zero, s_zero, y_bufs, t=SEQ, n_seq=BATCH, row0=0)
    y_f, y_b, _, _ = _rwkv_scan(feats, jnp.swapaxes(s_lat_f, -1, -2), jnp.swapaxes(s_lat_b, -1, -2), (y_f, y_b),
                                t=DEC_SEQ, n_seq=DEC_BATCH, row0=N_CTX)
    oc = _rwkv_out(y_f, y_b, bonus, g, lp['rwkv_lnx_g'], lp['rwkv_lnx_b'])
    return oc, jnp.swapaxes(sf_c, -1, -2), jnp.swapaxes(sb_c, -1, -2), (y_f, y_b)


def _rwkv_out_kernel(yf_ref, yb_ref, bonus_ref, g_ref, lg_ref, lb_ref, ones_ref, o_ref):
    ones_bd = ones_ref[...]
    y = yf_ref[...] + yb_ref[...]
    mu = _head_sum(y, ones_bd) * (1.0 / RWKV_HEAD_DIM)
    yc = y - mu
    var = _head_sum(yc * yc, ones_bd) * (1.0 / RWKV_HEAD_DIM)
    yn = yc * lax.rsqrt(var + RWKV_GN_EPS) * lg_ref[...] + lb_ref[...]
    o_ref[...] = ((yn + bonus_ref[...].astype(F32)) * g_ref[...].astype(F32)).astype(o_ref.dtype)


def _rwkv_out(y_f, y_b, bonus, g, lnx_g, lnx_b):
    tm = 512
    w = RWKV_WIDTH
    row = pl.BlockSpec((tm, w), lambda i: (i, 0))
    vec = pl.BlockSpec((1, w), lambda i: (0, 0))
    ones_bd = jnp.asarray(np.kron(np.eye(RWKV_HEADS), np.ones((RWKV_HEAD_DIM, RWKV_HEAD_DIM))), BF16)
    return pl.pallas_call(
        _rwkv_out_kernel,
        grid=(N_TOK // tm,),
        in_specs=[row, row, row, row, vec, vec, pl.BlockSpec((w, w), lambda i: (0, 0))],
        out_specs=row,
        out_shape=jax.ShapeDtypeStruct((N_TOK, w), BF16),
        compiler_params=_cparams(("parallel",)),
        name="rwkv_out",
    )(y_f, y_b, bonus, g, lnx_g.reshape(1, w), lnx_b.reshape(1, w), ones_bd)


def kernel(x_prompt, x_sample, cache_na_k, cache_na_v, cache_diff_k, cache_diff_v, state_rwkv_fwd, state_rwkv_bwd, c, c_ctx, w_ada, b_ada, w_in, na_rpb, diff_lambda, diff_subln, rwkv_mix, rwkv_w0, rwkv_w2, rwkv_a0, rwkv_a2, rwkv_g2, rwkv_kk, rwkv_ka, rwkv_rk, rwkv_lnx_g, rwkv_lnx_b, w_branch, w_out, ln1_g, ln1_b, w_ffn_in, w_ffn_out, ln2_g, ln2_b):
    x = jnp.concatenate([x_prompt.reshape(N_CTX, D_MODEL), x_sample.reshape(N_LAT, D_MODEL)], axis=0)
    cond = jnp.concatenate([c_ctx[None, :], c, jnp.zeros((N_COND - 1 - DEC_BATCH, D_MODEL), F32)], axis=0)
    mods = _adaln_all(cond, w_ada, b_ada)
    mods = mods.reshape(DEPTH, N_COND, 6, D_MODEL).transpose(0, 2, 1, 3).reshape(DEPTH * 6 * N_COND, 1, D_MODEL)

    cache_na_k = cache_na_k.reshape(DEC_BATCH, DEPTH, PAST_LEN, BRANCH_WIDTH)
    cache_na_v = cache_na_v.reshape(DEC_BATCH, DEPTH, PAST_LEN, BRANCH_WIDTH)
    cache_diff_k = cache_diff_k.reshape(DEC_BATCH, DEPTH, PAST_LEN, BRANCH_WIDTH)
    cache_diff_v = cache_diff_v.reshape(DEC_BATCH, DEPTH, PAST_LEN, BRANCH_WIDTH)
    rope_tables = _rope_tables()

    c1 = ATT_WIDTH + RWKV_FEAT
    w_branch16 = w_branch.astype(BF16)
    w_out16 = w_out.astype(BF16)
    w_ffn_out16 = w_ffn_out.astype(BF16)

    h = _modulate(x, mods, 0)
    oa = jnp.zeros((N_TOK, BRANCH_WIDTH), BF16)
    ob = jnp.zeros((N_TOK, BRANCH_WIDTH), BF16)
    y_bufs = (jnp.zeros((N_TOK, RWKV_WIDTH), F32), jnp.zeros((N_TOK, RWKV_WIDTH), F32))
    na_stacks = [jnp.zeros(_cache_stack_shape(True).shape, F32) for _ in range(2)]
    diff_stacks = [jnp.zeros(_cache_stack_shape(False).shape, F32), jnp.zeros(_cache_stack_shape(True).shape, F32)]
    new_f, new_b = [], []
    for l in range(DEPTH):
        lam_init = 0.8 - 0.6 * math.exp(-0.3 * l)
        lp = {'rwkv_mix': rwkv_mix[l], 'rwkv_w0': rwkv_w0[l], 'rwkv_w2': rwkv_w2[l], 'rwkv_a0': rwkv_a0[l],
              'rwkv_a2': rwkv_a2[l], 'rwkv_g2': rwkv_g2[l], 'rwkv_kk': rwkv_kk[l], 'rwkv_ka': rwkv_ka[l],
              'rwkv_rk': rwkv_rk[l], 'rwkv_lnx_g': rwkv_lnx_g[l], 'rwkv_lnx_b': rwkv_lnx_b[l]}
        p_att = _matmul_w32(h, w_in, l, F32, col0=0, n=ATT_WIDTH, name="in_att")
        p_rw = _matmul_w32(h, w_in, l, F32, col0=ATT_WIDTH, n=RWKV_FEAT, tn=RWKV_FEAT // 3, name="in_rwkv")
        gates = _matmul_w32(h, w_in, l, BF16, col0=c1, n=GATE_WIDTH, name="in_gate", sigmoid=True)

        oa, *na_stacks = _na_ctx(p_att, l, oa, na_stacks)
        oa = _na_lat(p_att, cache_na_k, cache_na_v, l, _na_bias_table(na_rpb[l]), oa)
        ob, *diff_stacks = _diff_attn(p_att, diff_lambda[l], diff_subln[l], lam_init, l, ob, t=SEQ, n_seq=BATCH,
                                      blk0=0, stacks=diff_stacks)
        ob, = _diff_attn(p_att, diff_lambda[l], diff_subln[l], lam_init, l, ob, t=DEC_SEQ, n_seq=DEC_BATCH,
                         blk0=N_CTX // DEC_SEQ, cache=(cache_diff_k, cache_diff_v), tables=rope_tables)
        oc, sf_c, sb_c, y_bufs = _rwkv_mixer(p_rw, state_rwkv_fwd[:, l], state_rwkv_bwd[:, l], lp, y_bufs)

        merged = _merge(oa, ob, oc, gates, w_branch16, l)
        x, h2 = _proj_res_ln(merged, w_out16, x, mods, l, 2, ln1_g[l], ln1_b[l], l, 3, tm=512, n_sub=4, name="w_out_res_ln")
        hid = _ffn_in(h2, w_ffn_in, l)
        new_f.append(sf_c)
        new_b.append(sb_c)
        ffn_out = functools.partial(_proj_res_ln, hid, w_ffn_out16, x, mods, l, 5, ln2_g[l], ln2_b[l],
                                    tm=256, n_sub=1, name="ffn_out_res_ln")
        if l + 1 < DEPTH:
            x, h = ffn_out(l + 1, 0)
        else:
            y_ctx, _ = ffn_out(None, 0, rows=(0, N_CTX))
            y_lat, _ = ffn_out(None, 0, rows=(N_CTX, N_LAT))

    y_prompt = y_ctx.reshape(BATCH, SEQ, D_MODEL)
    y_sample = y_lat.reshape(DEC_BATCH, DEC_SEQ, D_MODEL)
    new_na_k, new_na_v = na_stacks
    new_diff_k = diff_stacks[0].reshape(BATCH, DEPTH, SEQ, DIFF_HEADS, 2, DIFF_QK_DIM)
    new_diff_v = diff_stacks[1]
    return (y_prompt, y_sample, new_na_k, new_na_v, new_diff_k, new_diff_v,
            jnp.stack(new_f, axis=1), jnp.stack(new_b, axis=1))
```
